```python
import math
import jax, jax.numpy as jnp
from jax import lax
import numpy as np

D_MODEL = 1024
BATCH = 16
SEQ = 256
DEPTH = 2
DEC_BATCH = 8
DEC_SEQ = 1024
PAST_LEN = 256

GRID_W = 64
LRU_W = 1024
LRU_BLOCKS = 16
LRU_BW = LRU_W // LRU_BLOCKS
CONV_W = 4
LRU_C = 8.0
N_HEADS = 16
N_KV = 4
HEAD_DIM = 64
Q_W = N_HEADS * HEAD_DIM
KV_W = N_KV * HEAD_DIM
ROPE_BASE = 10000.0
Q_BLOCK = 128
POOL_WINDOWS = (2, 4, 8, 16)
POOL_W = 1024
POOL_G = POOL_W // 4
D_FF = 3 * D_MODEL
FFN_CONV_W = 3
N_BRANCH = 3
IN_SIZES = (LRU_W, LRU_W, Q_W, KV_W, KV_W, POOL_W, N_BRANCH * D_MODEL)
IN_COLS = 2 * LRU_W + Q_W + 2 * KV_W + POOL_W + N_BRANCH * D_MODEL
EPS = 1e-6

kernel_name = 'hybrid_lru_gqa_pool_prefix_dit_step'

F32 = jnp.float32


def rmsnorm(x, g):
    xf = x.astype(F32)
    y = xf * lax.rsqrt(jnp.mean(xf * xf, axis=-1, keepdims=True) + EPS)
    return (y * g.astype(F32)).astype(x.dtype)


def modulation(cond, w, b):
    return jax.nn.silu(cond) @ w + b


def dwconv(x, w, b):
    width = w.shape[0]
    left = (width - 1) // 2
    S = x.shape[1]
    xp = jnp.pad(x, ((0, 0), (left, width - 1 - left), (0, 0)))
    out = b
    for k in range(width):
        out = out + xp[:, k:k + S] * w[k]
    return out


def _lin_combine(left, right):
    a_l, b_l = left
    a_r, b_r = right
    return a_l * a_r, a_r * b_l + b_r


def rglru(x, w_r, b_r, w_i, b_i, lam, h0, reverse):
    B, S, W = x.shape
    xb = x.reshape(B, S, LRU_BLOCKS, LRU_BW)
    r = jax.nn.sigmoid(jnp.einsum('bsnk,nkj->bsnj', xb, w_r).reshape(B, S, W) + b_r)
    i = jax.nn.sigmoid(jnp.einsum('bsnk,nkj->bsnj', xb, w_i).reshape(B, S, W) + b_i)
    log_a = (LRU_C * r.astype(F32)) * jax.nn.log_sigmoid(lam.astype(F32))
    a = jnp.exp(log_a)
    u = jnp.sqrt(-jnp.expm1(2.0 * log_a)) * (i * x).astype(F32)
    if h0 is not None:
        edge = S - 1 if reverse else 0
        u = u.at[:, edge].add(a[:, edge] * h0.astype(F32))
    _, h = lax.associative_scan(_lin_combine, (a, u), axis=1, reverse=reverse)
    final = h[:, 0] if reverse else h[:, -1]
    return h.astype(x.dtype), final.astype(x.dtype)


def rope2d(x):
    S = x.shape[1]
    rows = S // GRID_W
    row = np.repeat(np.arange(rows), GRID_W).astype(np.float32)
    col = np.tile(np.arange(GRID_W), rows).astype(np.float32)
    half = HEAD_DIM // 2
    quarter = half // 2
    inv = (ROPE_BASE ** (-np.arange(quarter, dtype=np.float32) / quarter)).astype(np.float32)

    def rot(xh, pos):
        ang = jnp.asarray(pos[:, None] * inv[None, :])
        cos = jnp.cos(ang)[None, :, None, :].astype(x.dtype)
        sin = jnp.sin(ang)[None, :, None, :].astype(x.dtype)
        x1, x2 = xh[..., :quarter], xh[..., quarter:]
        return jnp.concatenate([x1 * cos - x2 * sin, x1 * sin + x2 * cos], axis=-1)

    return jnp.concatenate([rot(x[..., :half], row), rot(x[..., half:], col)], axis=-1)


def attend(q, k, v):
    B, Sq = q.shape[0], q.shape[1]
    G = N_HEADS // N_KV
    nb = Sq // Q_BLOCK
    scale = HEAD_DIM ** -0.5
    qb = q.reshape(B, nb, Q_BLOCK, N_KV, G, HEAD_DIM).transpose(1, 0, 2, 3, 4, 5)
    kf = k.astype(F32)

    def block(qi):
        s = jnp.einsum('bqkgd,bskd->bkgqs', qi.astype(F32), kf) * scale
        p = jax.nn.softmax(s, axis=-1)
        return jnp.einsum('bkgqs,bskd->bqkgd', p.astype(v.dtype), v)

    o = lax.map(block, qb)
    return o.transpose(1, 0, 2, 3, 4, 5).reshape(B, Sq, Q_W)


def pool_mixer(u, w_pool, scale):
    B, S, C = u.shape
    uf = u.astype(F32)
    cs = jnp.pad(jnp.cumsum(uf, axis=1), ((0, 0), (1, 0), (0, 0)))
    t = np.arange(S)
    outs = []
    for g, w in enumerate(POOL_WINDOWS):
        lo = np.clip(t - w // 2, 0, S)
        hi = np.clip(t - w // 2 + w, 0, S)
        sl = slice(g * POOL_G, (g + 1) * POOL_G)
        csg = cs[..., sl]
        cnt = jnp.asarray((hi - lo).astype(np.float32))[None, :, None]
        mean = (csg[:, hi] - csg[:, lo]) / cnt
        d = (mean - uf[..., sl]).astype(u.dtype)
        outs.append(d @ w_pool[g])
    return jnp.concatenate(outs, axis=-1) * scale


def mixers(h, p, ctx):
    B, S, _ = h.shape
    splits = [int(s) for s in np.cumsum(IN_SIZES)[:-1]]
    xr, yr, q, k, v, up, gl = jnp.split(h @ p['w_in'], splits, axis=-1)
    xr = dwconv(xr, p['conv_w'], p['conv_b'])
    h0f = None if ctx is None else ctx[2][:, 0]
    h0b = None if ctx is None else ctx[2][:, 1]
    hf, sf = rglru(xr, p['w_rg'][0], p['b_rg'][0], p['w_ig'][0], p['b_ig'][0], p['lru_lambda'][0], h0f, False)
    hb, sb = rglru(xr, p['w_rg'][1], p['b_rg'][1], p['w_ig'][1], p['b_ig'][1], p['lru_lambda'][1], h0b, True)
    branch_a = ((hf + hb) * jax.nn.gelu(yr)) @ p['w_br_a']
    q = rmsnorm(q.reshape(B, S, N_HEADS, HEAD_DIM), p['q_norm_g'])
    k = rmsnorm(k.reshape(B, S, N_KV, HEAD_DIM), p['k_norm_g'])
    v = v.reshape(B, S, N_KV, HEAD_DIM)
    if ctx is None:
        o = attend(q, k, v)
    else:
        o = attend(rope2d(q), jnp.concatenate([ctx[0], rope2d(k)], axis=1), jnp.concatenate([ctx[1], v], axis=1))
    branch_b = o @ p['w_br_b']
    branch_c = pool_mixer(up, p['w_pool'], p['pool_scale']) @ p['w_br_c']
    ga, gb, gc = jnp.split(jax.nn.sigmoid(gl + p['b_gate']), N_BRANCH, axis=-1)
    out = (ga * branch_a + gb * branch_b + gc * branch_c) @ p['w_o']
    new_ctx = (k, v, jnp.stack([sf, sb], axis=1)) if ctx is None else None
    return out, new_ctx


def conv_ffn(h, p):
    z = dwconv(h @ p['w_up'], p['ffn_conv_w'], p['ffn_conv_b'])
    g, val = jnp.split(z, 2, axis=-1)
    return (jax.nn.silu(g) * val) @ p['w_down']


def trunk_block(x, mod, p, ctx):
    sh1, sc1, g1, sh2, sc2, g2 = jnp.split(mod, 6, axis=-1)
    h = rmsnorm(x, p['norm1_g']) * (1.0 + sc1) + sh1
    mix, new_ctx = mixers(h, p, ctx)
    x = x + g1 * mix
    h = rmsnorm(x, p['norm2_g']) * (1.0 + sc2) + sh2
    x = x + g2 * conv_ffn(h, p)
    return x, new_ctx


def setup_inputs(seed: int = 0) -> dict:
    key = jax.random.key(seed)
    ks = jax.random.split(key, 40)
    D = D_MODEL

    def nrm(k, shape, s):
        return jax.random.normal(k, shape, F32) * s

    lam_p = jax.random.uniform(ks[14], (DEPTH, 2, LRU_W), F32, minval=0.9, maxval=0.999)
    return {
        'x_prompt': nrm(ks[0], (BATCH, SEQ, D), 1.0),
        'x_sample': nrm(ks[1], (DEC_BATCH, DEC_SEQ, D), 1.0),
        'cache_k': nrm(ks[2], (DEC_BATCH, DEPTH, PAST_LEN, N_KV, HEAD_DIM), 1.0),
        'cache_v': nrm(ks[3], (DEC_BATCH, DEPTH, PAST_LEN, N_KV, HEAD_DIM), 1.0),
        'state_lru': nrm(ks[4], (DEC_BATCH, DEPTH, 2, LRU_W), 0.5),
        'c': nrm(ks[5], (DEC_BATCH, D), 1.0),
        'c_ctx': nrm(ks[6], (D,), 1.0),
        'norm1_g': 1.0 + nrm(ks[7], (DEPTH, D), 0.02),
        'norm2_g': 1.0 + nrm(ks[8], (DEPTH, D), 0.02),
        'w_mod': nrm(ks[9], (DEPTH, D, 6 * D), 0.5 * D ** -0.5),
        'b_mod': nrm(ks[10], (DEPTH, 6 * D), 0.01),
        'w_in': nrm(ks[11], (DEPTH, D, IN_COLS), D ** -0.5),
        'b_gate': nrm(ks[12], (DEPTH, N_BRANCH * D), 0.01),
        'conv_w': nrm(ks[13], (DEPTH, CONV_W, LRU_W), CONV_W ** -0.5),
        'conv_b': nrm(ks[15], (DEPTH, LRU_W), 0.01),
        'w_rg': nrm(ks[16], (DEPTH, 2, LRU_BLOCKS, LRU_BW, LRU_BW), LRU_BW ** -0.5),
        'b_rg': nrm(ks[17], (DEPTH, 2, LRU_W), 0.01),
        'w_ig': nrm(ks[18], (DEPTH, 2, LRU_BLOCKS, LRU_BW, LRU_BW), LRU_BW ** -0.5),
        'b_ig': nrm(ks[19], (DEPTH, 2, LRU_W), 0.01),
        'lru_lambda': jnp.log(lam_p) - jnp.log1p(-lam_p),
        'q_norm_g': 1.0 + nrm(ks[20], (DEPTH, HEAD_DIM), 0.02),
        'k_norm_g': 1.0 + nrm(ks[21], (DEPTH, HEAD_DIM), 0.02),
        'w_pool': nrm(ks[22], (DEPTH, 4, POOL_G, POOL_G), POOL_G ** -0.5),
        'pool_scale': 1.0 + nrm(ks[23], (DEPTH, POOL_W), 0.02),
        'w_br_a': nrm(ks[24], (DEPTH, LRU_W, D), LRU_W ** -0.5),
        'w_br_b': nrm(ks[25], (DEPTH, Q_W, D), Q_W ** -0.5),
        'w_br_c': nrm(ks[26], (DEPTH, POOL_W, D), POOL_W ** -0.5),
        'w_o': nrm(ks[27], (DEPTH, D, D), D ** -0.5),
        'w_up': nrm(ks[28], (DEPTH, D, 2 * D_FF), D ** -0.5),
        'ffn_conv_w': nrm(ks[29], (DEPTH, FFN_CONV_W, 2 * D_FF), FFN_CONV_W ** -0.5),
        'ffn_conv_b': nrm(ks[30], (DEPTH, 2 * D_FF), 0.01),
        'w_down': nrm(ks[31], (DEPTH, D_FF, D), D_FF ** -0.5),
        'final_norm_g': 1.0 + nrm(ks[32], (D,), 0.02),
    }


def reference(x_prompt, x_sample, cache_k, cache_v, state_lru, c, c_ctx,
              norm1_g, norm2_g, w_mod, b_mod, w_in, b_gate, conv_w, conv_b,
              w_rg, b_rg, w_ig, b_ig, lru_lambda, q_norm_g, k_norm_g,
              w_pool, pool_scale, w_br_a, w_br_b, w_br_c, w_o,
              w_up, ffn_conv_w, ffn_conv_b, w_down, final_norm_g):
    xp = x_prompt
    xs = x_sample
    new_k, new_v, new_s = [], [], []
    for l in range(DEPTH):
        p = dict(norm1_g=norm1_g[l], norm2_g=norm2_g[l], w_in=w_in[l], b_gate=b_gate[l],
                 conv_w=conv_w[l], conv_b=conv_b[l], w_rg=w_rg[l], b_rg=b_rg[l],
                 w_ig=w_ig[l], b_ig=b_ig[l], lru_lambda=lru_lambda[l],
                 q_norm_g=q_norm_g[l], k_norm_g=k_norm_g[l], w_pool=w_pool[l],
                 pool_scale=pool_scale[l], w_br_a=w_br_a[l], w_br_b=w_br_b[l],
                 w_br_c=w_br_c[l], w_o=w_o[l], w_up=w_up[l], ffn_conv_w=ffn_conv_w[l],
                 ffn_conv_b=ffn_conv_b[l], w_down=w_down[l])
        mod_ctx = modulation(c_ctx, w_mod[l], b_mod[l])[None, None, :]
        mod_lat = modulation(c, w_mod[l], b_mod[l])[:, None, :]
        xp, (kc, vc, sc) = trunk_block(xp, mod_ctx, p, None)
        new_k.append(kc)
        new_v.append(vc)
        new_s.append(sc)
        xs, _ = trunk_block(xs, mod_lat, p, (cache_k[:, l], cache_v[:, l], state_lru[:, l]))
    y_prompt = rmsnorm(xp, final_norm_g)
    y_sample = rmsnorm(xs, final_norm_g)
    new_cache_k = jnp.stack(new_k, axis=1)
    new_cache_v = jnp.stack(new_v, axis=1)
    new_state_lru = jnp.stack(new_s, axis=1)
    return (y_prompt, y_sample, new_cache_k, new_cache_v, new_state_lru)
```

```python
import functools

import numpy as np
import jax
import jax.numpy as jnp
from jax import lax
from jax.experimental import pallas as pl
from jax.experimental.pallas import tpu as pltpu

F32 = jnp.float32
BF16 = jnp.bfloat16

D = 1024
DEPTH = 2
N_CTX, S_CTX = 16, 256
N_LAT, S_LAT = 8, 1024
PAST = 256
GRID_W = 64
LRU_W = 1024
LRU_BW = 64
LRU_C = 8.0
N_HEADS, N_KV, HEAD_DIM = 16, 4, 64
KV_W = N_KV * HEAD_DIM
ROPE_BASE = 10000.0
D_FF = 3 * D
EPS = 1e-6
SUB = 8
LANE = 128
ROPE_SHIFT = HEAD_DIM // 4
C_XR, C_YR, C_Q, C_K, C_V, C_UP, C_GL, C_END = 0, 1024, 2048, 3072, 3328, 3584, 4608, 7680

VMEM_LIMIT = 56 * 1024 * 1024


def _cparams(n_axes):
    return pltpu.CompilerParams(dimension_semantics=("arbitrary",) * n_axes,
                                vmem_limit_bytes=VMEM_LIMIT)


def _resident(shape):
    return pl.BlockSpec(shape, lambda *_: (0,) * len(shape), pipeline_mode=pl.Buffered(1))


def _dot(a, b):
    return jnp.dot(a, b, preferred_element_type=F32)


def _dot_nt(a, b):
    return lax.dot_general(a, b, (((1,), (1,)), ((), ())), preferred_element_type=F32)


def _rms(x):
    return x * lax.rsqrt(jnp.mean(x * x, axis=-1, keepdims=True) + EPS)


def _shift_prev(x, rm, n=1):
    r, w = x.shape
    sub = lax.broadcasted_iota(jnp.int32, (SUB, w), 0)
    pieces = []
    for m in range(r // rm):
        base = m * rm
        for t in range(n):
            lo = base + rm - SUB * (n - t)
            pieces.append(jnp.where(sub == 0, 0.0, pltpu.roll(x[lo:lo + SUB], 1, 0)))
        pieces.append(x[base:base + rm - SUB * n])
    return jnp.concatenate(pieces, axis=0)


def _shift_next(x, rm, n=1):
    r, w = x.shape
    sub = lax.broadcasted_iota(jnp.int32, (SUB, w), 0)
    pieces = []
    for m in range(r // rm):
        base = m * rm
        pieces.append(x[base + SUB * n:base + rm])
        for t in range(n):
            lo = base + SUB * t
            pieces.append(jnp.where(sub == SUB - 1, 0.0, pltpu.roll(x[lo:lo + SUB], SUB - 1, 0)))
    return jnp.concatenate(pieces, axis=0)


def _mod_kernel(cond_ref, w_ref, b_ref, out_ref):
    c = cond_ref[...]
    s = (c * jax.nn.sigmoid(c)).astype(BF16)
    out_ref[...] = _dot(s, w_ref[...].astype(BF16)) + b_ref[...]


def _modulation(cond, w_mod, b_mod):
    tn = 1536
    return pl.pallas_call(
        _mod_kernel,
        grid=(DEPTH, 6 * D // tn),
        in_specs=[pl.BlockSpec((16, D), lambda l, j: (0, 0)),
                  pl.BlockSpec((None, D, tn), lambda l, j: (l, 0, j)),
                  pl.BlockSpec((None, 1, tn), lambda l, j: (l, 0, j))],
        out_specs=pl.BlockSpec((None, 16, tn), lambda l, j: (l, 0, j)),
        out_shape=jax.ShapeDtypeStruct((DEPTH, 16, 6 * D), F32),
        compiler_params=_cparams(2),
        name="modulation",
    )(cond, w_mod, b_mod.reshape(DEPTH, 1, 6 * D))


def _inproj_kernel(*refs, rope):
    if rope:
        (x_ref, mod_ref, n1g_ref, w_ref, bgate_ref, qg_ref, kg_ref, gmat_ref, cos_ref, sin_ref,
         xr_ref, gy_ref, q_ref, k_ref, v_ref, up_ref, gate_ref) = refs
    else:
        (x_ref, mod_ref, n1g_ref, w_ref, bgate_ref, qg_ref, kg_ref, gmat_ref,
         xr_ref, gy_ref, q_ref, k_ref, v_ref, up_ref, gate_ref) = refs
    mod = mod_ref[...]
    sh1, sc1 = mod[:, 0:D], mod[:, D:2 * D]
    h = ((_rms(x_ref[...]) * n1g_ref[...]) * (1.0 + sc1) + sh1).astype(BF16)

    def mm(c0, c1):
        return _dot(h, w_ref[:, c0:c1])

    xr_ref[...] = mm(C_XR, C_YR).astype(BF16)
    gy_ref[...] = jax.nn.gelu(mm(C_YR, C_Q)).astype(BF16)

    lane = lax.broadcasted_iota(jnp.int32, (1, LANE), 1)
    first = (lane & ROPE_SHIFT) == 0

    def head_norm(xb, g):
        ms = _dot((xb * xb).astype(BF16), gmat_ref[...])
        y = (xb * lax.rsqrt(ms + EPS)) * g
        if rope:
            partner = jnp.where(first, pltpu.roll(y, LANE - ROPE_SHIFT, 1), pltpu.roll(y, ROPE_SHIFT, 1))
            y = y * cos_ref[...] + partner * sin_ref[...]
        return y

    qa = mm(C_Q, C_K)
    for c in range(N_HEADS * HEAD_DIM // LANE):
        q_ref[:, c * LANE:(c + 1) * LANE] = head_norm(qa[:, c * LANE:(c + 1) * LANE], qg_ref[...]).astype(BF16)
    ka = mm(C_K, C_V)
    for c in range(KV_W // LANE):
        k_ref[:, c * LANE:(c + 1) * LANE] = head_norm(ka[:, c * LANE:(c + 1) * LANE], kg_ref[...])
    v_ref[...] = mm(C_V, C_UP)
    up_ref[...] = mm(C_UP, C_GL).astype(BF16)
    for g in range(3):
        z = mm(C_GL + g * D, C_GL + (g + 1) * D) + bgate_ref[:, g * D:(g + 1) * D]
        gate_ref[:, g * D:(g + 1) * D] = jax.nn.sigmoid(z).astype(BF16)


def _inproj(x, mod_l, n1g, w_in, b_gate, qg, kg, gmat, rope_tabs, *, tm, row_of_tile, name):
    t = x.shape[0]
    rope = rope_tabs is not None
    full = lambda shape: pl.BlockSpec(shape, lambda i: (0,) * len(shape))
    in_specs = [pl.BlockSpec((tm, D), lambda i: (i, 0)),
                pl.BlockSpec((None, 1, 6 * D), lambda i: (row_of_tile(i), 0, 0)),
                full((1, D)), _resident((D, C_END)), full((1, 3 * D)), full((1, LANE)), full((1, LANE)),
                full((LANE, LANE))]
    args = [x, mod_l, n1g, w_in, b_gate, qg, kg, gmat]
    if rope:
        per_seq = S_LAT // tm
        in_specs += [pl.BlockSpec((tm, LANE), lambda i: (i % per_seq, 0))] * 2
        args += list(rope_tabs)
    row = lambda w: pl.BlockSpec((tm, w), lambda i: (i, 0))
    out_specs = [row(D), row(D), row(D), row(KV_W), row(KV_W), row(D), row(3 * D)]
    out_shape = [jax.ShapeDtypeStruct((t, D), BF16), jax.ShapeDtypeStruct((t, D), BF16),
                 jax.ShapeDtypeStruct((t, D), BF16), jax.ShapeDtypeStruct((t, KV_W), F32),
                 jax.ShapeDtypeStruct((t, KV_W), F32), jax.ShapeDtypeStruct((t, D), BF16),
                 jax.ShapeDtypeStruct((t, 3 * D), BF16)]
    return pl.pallas_call(
        functools.partial(_inproj_kernel, rope=rope),
        grid=(t // tm,), in_specs=in_specs, out_specs=out_specs, out_shape=out_shape,
        compiler_params=_cparams(1), name=name,
    )(*args)


GB = 256


def _lru_kernel(xr_ref, gy_ref, h0_ref, cw_ref, cb_ref, wg_ref, bg_ref, lam_ref,
                out_ref, ends_ref, a_s, u_s):
    r = xr_ref.shape[0]
    steps = r // SUB
    sub = lax.broadcasted_iota(jnp.int32, (SUB, GB), 0)
    for cb in range(LRU_W // GB):
        cols = slice(cb * GB, (cb + 1) * GB)
        x = xr_ref[:, cols].astype(F32)
        xn = _shift_next(x, r)
        xc = cb_ref[:, cols] + _shift_prev(x, r) * cw_ref[0:1, cols]
        xc = xc + x * cw_ref[1:2, cols]
        xc = xc + xn * cw_ref[2:3, cols]
        xc = xc + _shift_next(xn, r) * cw_ref[3:4, cols]
        lhs = xc.astype(BF16)
        for d in range(2):
            g = _dot(lhs, wg_ref[d, cb])
            rg = jax.nn.sigmoid(g[:, :GB] + bg_ref[d, 0:1, cols])
            ig = jax.nn.sigmoid(g[:, GB:] + bg_ref[d, 1:2, cols])
            lam = lam_ref[d:d + 1, cols]
            log_sig = jnp.minimum(lam, 0.0) - jnp.log(1.0 + jnp.exp(-jnp.abs(lam)))
            log_a = (LRU_C * rg) * log_sig
            a = jnp.exp(log_a)
            u = jnp.sqrt(jnp.tanh(-log_a) * (1.0 + a * a)) * (ig * xc)
            a_s[d] = a
            u_s[d] = u
            e0 = 0 if d == 0 else r - SUB
            edge = sub == (0 if d == 0 else SUB - 1)
            ae = a_s[d, e0:e0 + SUB, :]
            u_s[d, e0:e0 + SUB, :] = u_s[d, e0:e0 + SUB, :] + jnp.where(edge, ae * h0_ref[d:d + 1, cols], 0.0)
            a_s[d, e0:e0 + SUB, :] = jnp.where(edge, 0.0, ae)

        def step(k, carry):
            hf, pf, hb, pb = carry
            rf = pl.multiple_of(k * SUB, SUB)
            rb = pl.multiple_of((steps - 1 - k) * SUB, SUB)
            af = a_s[0, pl.ds(rf, SUB), :]
            hf = af * hf + u_s[0, pl.ds(rf, SUB), :]
            pf = af * pf
            u_s[0, pl.ds(rf, SUB), :] = hf
            a_s[0, pl.ds(rf, SUB), :] = pf
            ab = a_s[1, pl.ds(rb, SUB), :]
            hb = ab * hb + u_s[1, pl.ds(rb, SUB), :]
            pb = ab * pb
            u_s[1, pl.ds(rb, SUB), :] = hb
            a_s[1, pl.ds(rb, SUB), :] = pb
            return hf, pf, hb, pb

        zero = jnp.zeros((SUB, GB), F32)
        one = jnp.ones((SUB, GB), F32)
        hf, pf, hb, pb = lax.fori_loop(0, steps, step, (zero, one, zero, one), unroll=8)

        ef, eb = hf, hb
        for sh in (1, 2, 4):
            keep_f = sub >= sh
            ef = ef + pf * jnp.where(keep_f, pltpu.roll(ef, sh, 0), 0.0)
            pf = pf * jnp.where(keep_f, pltpu.roll(pf, sh, 0), 1.0)
            keep_b = sub < SUB - sh
            eb = eb + pb * jnp.where(keep_b, pltpu.roll(eb, SUB - sh, 0), 0.0)
            pb = pb * jnp.where(keep_b, pltpu.roll(pb, SUB - sh, 0), 1.0)
        ends_ref[0:SUB, cols] = ef
        ends_ref[SUB:2 * SUB, cols] = eb
        cf = jnp.where(sub >= 1, pltpu.roll(ef, 1, 0), 0.0)
        cbk = jnp.where(sub < SUB - 1, pltpu.roll(eb, SUB - 1, 0), 0.0)
        cf2 = jnp.concatenate([cf, cf], axis=0)
        cb2 = jnp.concatenate([cbk, cbk], axis=0)

        def fix(m, _):
            rows = pl.ds(pl.multiple_of(m * 2 * SUB, 2 * SUB), 2 * SUB)
            hft = u_s[0, rows, :] + a_s[0, rows, :] * cf2
            hbt = u_s[1, rows, :] + a_s[1, rows, :] * cb2
            out_ref[rows, cols] = ((hft + hbt) * gy_ref[rows, cols].astype(F32)).astype(BF16)
            return 0

        lax.fori_loop(0, r // (2 * SUB), fix, 0, unroll=4)


def _lru(xr, gy, h0, conv_w, conv_b, wg, bg, lam, *, seq, name):
    t = xr.shape[0]
    nseq = t // seq
    full = lambda shape: pl.BlockSpec(shape, lambda i: (0,) * len(shape))
    return pl.pallas_call(
        _lru_kernel,
        grid=(nseq,),
        in_specs=[pl.BlockSpec((seq, D), lambda i: (i, 0)), pl.BlockSpec((seq, D), lambda i: (i, 0)),
                  pl.BlockSpec((None, 2, LRU_W), lambda i: (i, 0, 0)),
                  full((4, LRU_W)), full((1, LRU_W)), full((2, LRU_W // GB, GB, 2 * GB)),
                  full((2, 2, LRU_W)), full((2, LRU_W))],
        out_specs=[pl.BlockSpec((seq, D), lambda i: (i, 0)),
                   pl.BlockSpec((None, 2 * SUB, LRU_W), lambda i: (i, 0, 0))],
        out_shape=[jax.ShapeDtypeStruct((t, D), BF16), jax.ShapeDtypeStruct((nseq, 2 * SUB, LRU_W), F32)],
        scratch_shapes=[pltpu.VMEM((2, seq, GB), F32), pltpu.VMEM((2, seq, GB), F32)],
        compiler_params=_cparams(1), name=name,
    )(xr, gy, h0, conv_w, conv_b, wg, bg, lam)


POOL_G = 256


def _pool_kernel(up_ref, wp_ref, scale_ref, out_ref):
    r = up_ref.shape[0]
    steps = r // SUB
    row = lax.broadcasted_iota(jnp.int32, (r, 1), 0)
    t = (row & (SUB - 1)) * steps + jnp.right_shift(row, 3)
    for g in range(4):
        cols = slice(g * POOL_G, (g + 1) * POOL_G)
        x = up_ref[:, cols].astype(F32)
        half = 1 << g
        back, fwd = x, x
        for lvl in range(g):
            n = 1 << lvl
            back = back + _shift_prev(back, r, n)
            fwd = fwd + _shift_next(fwd, r, n)
        win = _shift_prev(back, r) + fwd
        cnt = jnp.minimum(t + half, r) - jnp.maximum(t - half, 0)
        d = (win / cnt.astype(F32) - x).astype(BF16)
        out_ref[:, cols] = (_dot(d, wp_ref[g]) * scale_ref[:, cols]).astype(BF16)


def _pool(up, w_pool, scale, *, seq, name):
    t = up.shape[0]
    return pl.pallas_call(
        _pool_kernel,
        grid=(t // seq,),
        in_specs=[pl.BlockSpec((seq, D), lambda i: (i, 0)),
                  pl.BlockSpec((4, POOL_G, POOL_G), lambda i: (0, 0, 0)),
                  pl.BlockSpec((1, D), lambda i: (0, 0))],
        out_specs=pl.BlockSpec((seq, D), lambda i: (i, 0)),
        out_shape=jax.ShapeDtypeStruct((t, D), BF16),
        compiler_params=_cparams(1), name=name,
    )(up, w_pool, scale)


TQ = 256


def _attn_kernel(*refs, cached):
    if cached:
        q_ref, k_ref, v_ref, ck_ref, cv_ref, o_ref = refs
        kall = jnp.concatenate([ck_ref[...], k_ref[...]], axis=0).astype(BF16)
        vall = jnp.concatenate([cv_ref[...], v_ref[...]], axis=0).astype(BF16)
    else:
        q_ref, k_ref, v_ref, o_ref = refs
        kall = k_ref[...].astype(BF16)
        vall = v_ref[...].astype(BF16)
    sq = q_ref.shape[0]
    half = HEAD_DIM
    lane = lax.broadcasted_iota(jnp.int32, (1, LANE), 1)
    zero = jnp.zeros((), BF16)
    scale = HEAD_DIM ** -0.5
    for kvh in range(N_KV):
        blk = slice((kvh // 2) * LANE, (kvh // 2 + 1) * LANE)
        in_low = (lane < half) if kvh % 2 == 0 else (lane >= half)
        k_own = jnp.where(in_low, kall[:, blk], zero)
        v_own = jnp.where(in_low, vall[:, blk], zero)
        k_oth = pltpu.roll(k_own, half, 1)
        v_oth = pltpu.roll(v_own, half, 1)
        k_lo, k_hi = (k_own, k_oth) if kvh % 2 == 0 else (k_oth, k_own)
        v_lo, v_hi = (v_own, v_oth) if kvh % 2 == 0 else (v_oth, v_own)
        for pair in range(2):
            c0 = kvh * 4 * HEAD_DIM + pair * LANE

            def body(i, _, c0=c0, k_lo=k_lo, k_hi=k_hi, v_lo=v_lo, v_hi=v_hi):
                rows = pl.ds(pl.multiple_of(i * TQ, TQ), TQ)
                qp = q_ref[rows, c0:c0 + LANE]
                acc = jnp.zeros((TQ, LANE), F32)
                for kk, vv in ((k_lo, v_lo), (k_hi, v_hi)):
                    s = _dot_nt(qp, kk) * scale
                    e = jnp.exp(s - jnp.max(s, axis=-1, keepdims=True))
                    l = jnp.sum(e, axis=-1, keepdims=True)
                    acc = acc + _dot(e.astype(BF16), vv) / l
                o_ref[rows, c0:c0 + LANE] = acc.astype(BF16)
                return 0

            lax.fori_loop(0, sq // TQ, body, 0)


def _attention(q, k, v, cache_k, cache_v, layer, *, seq, name):
    t = q.shape[0]
    cached = cache_k is not None
    in_specs = [pl.BlockSpec((seq, D), lambda i: (i, 0)), pl.BlockSpec((seq, KV_W), lambda i: (i, 0)),
                pl.BlockSpec((seq, KV_W), lambda i: (i, 0))]
    args = [q, k, v]
    if cached:
        in_specs += [pl.BlockSpec((None, None, PAST, KV_W), lambda i: (i, layer, 0, 0))] * 2
        args += [cache_k, cache_v]
    return pl.pallas_call(
        functools.partial(_attn_kernel, cached=cached),
        grid=(t // seq,), in_specs=in_specs,
        out_specs=pl.BlockSpec((seq, D), lambda i: (i, 0)),
        out_shape=jax.ShapeDtypeStruct((t, D), BF16),
        compiler_params=_cparams(1), name=name,
    )(*args)


def _merge_kernel(a_ref, o_ref, c_ref, gate_ref, x_ref, mod_ref, n2g_ref, wa_ref, wb_ref, wc_ref, wo_ref,
                  x1_ref, h2_ref):
    mix = gate_ref[:, 0:D].astype(F32) * _dot(a_ref[...], wa_ref[...])
    mix = mix + gate_ref[:, D:2 * D].astype(F32) * _dot(o_ref[...], wb_ref[...])
    mix = mix + gate_ref[:, 2 * D:3 * D].astype(F32) * _dot(c_ref[...], wc_ref[...])
    out = _dot(mix.astype(BF16), wo_ref[...])
    mod = mod_ref[...]
    g1, sh2, sc2 = mod[:, 2 * D:3 * D], mod[:, 3 * D:4 * D], mod[:, 4 * D:5 * D]
    x1 = x_ref[...] + g1 * out
    x1_ref[...] = x1
    h2_ref[...] = ((_rms(x1) * n2g_ref[...]) * (1.0 + sc2) + sh2).astype(BF16)


def _merge(a, o, c, gates, x, mod_l, n2g, wa, wb, wc, wo, *, tm, row_of_tile, name):
    t = x.shape[0]
    row = lambda w: pl.BlockSpec((tm, w), lambda i: (i, 0))
    wspec = _resident((D, D))
    return pl.pallas_call(
        _merge_kernel,
        grid=(t // tm,),
        in_specs=[row(D), row(D), row(D), row(3 * D), row(D),
                  pl.BlockSpec((None, 1, 6 * D), lambda i: (row_of_tile(i), 0, 0)),
                  pl.BlockSpec((1, D), lambda i: (0, 0)), wspec, wspec, wspec, wspec],
        out_specs=[row(D), row(D)],
        out_shape=[jax.ShapeDtypeStruct((t, D), F32), jax.ShapeDtypeStruct((t, D), BF16)],
        compiler_params=_cparams(1), name=name,
    )(a, o, c, gates, x, mod_l, n2g, wa, wb, wc, wo)


FF_TN = 512
FF_ROWS = 1024


def _ffn_up_kernel(h_ref, wg_ref, wv_ref, cwg_ref, cwv_ref, cbg_ref, cbv_ref, out_ref, *, seq):
    h = h_ref[...]

    def conv(z, cw_ref, cb_ref):
        y = cb_ref[...] + _shift_prev(z, seq) * cw_ref[0:1, :]
        y = y + z * cw_ref[1:2, :]
        return y + _shift_next(z, seq) * cw_ref[2:3, :]

    g = conv(_dot(h, wg_ref[...]), cwg_ref, cbg_ref)
    v = conv(_dot(h, wv_ref[...]), cwv_ref, cbv_ref)
    out_ref[...] = ((g * jax.nn.sigmoid(g)) * v).astype(BF16)


def _ffn_up(h2, w_up, cw, cb, *, seq, name):
    t = h2.shape[0]
    nj = D_FF // FF_TN
    return pl.pallas_call(
        functools.partial(_ffn_up_kernel, seq=seq),
        grid=(t // FF_ROWS, nj),
        in_specs=[pl.BlockSpec((FF_ROWS, D), lambda i, j: (i, 0)),
                  pl.BlockSpec((D, FF_TN), lambda i, j: (0, j)),
                  pl.BlockSpec((D, FF_TN), lambda i, j: (0, nj + j)),
                  pl.BlockSpec((3, FF_TN), lambda i, j: (0, j)),
                  pl.BlockSpec((3, FF_TN), lambda i, j: (0, nj + j)),
                  pl.BlockSpec((1, FF_TN), lambda i, j: (0, j)),
                  pl.BlockSpec((1, FF_TN), lambda i, j: (0, nj + j))],
        out_specs=pl.BlockSpec((FF_ROWS, FF_TN), lambda i, j: (i, j)),
        out_shape=jax.ShapeDtypeStruct((t, D_FF), BF16),
        compiler_params=_cparams(2), name=name,
    )(h2, w_up, w_up, cw, cw, cb, cb)


def _ffn_down_kernel(act_ref, w_ref, x1_ref, mod_ref, fg_ref, out_ref, *, final):
    g2 = mod_ref[:, 5 * D:6 * D]
    x2 = x1_ref[...] + g2 * _dot(act_ref[...], w_ref[...])
    out_ref[...] = _rms(x2) * fg_ref[...] if final else x2


def _ffn_down(act, w_down, x1, mod_l, fg, *, tm, row_of_tile, final, name):
    t = x1.shape[0]
    return pl.pallas_call(
        functools.partial(_ffn_down_kernel, final=final),
        grid=(t // tm,),
        in_specs=[pl.BlockSpec((tm, D_FF), lambda i: (i, 0)),
                  _resident((D_FF, D)),
                  pl.BlockSpec((tm, D), lambda i: (i, 0)),
                  pl.BlockSpec((None, 1, 6 * D), lambda i: (row_of_tile(i), 0, 0)),
                  pl.BlockSpec((1, D), lambda i: (0, 0))],
        out_specs=pl.BlockSpec((tm, D), lambda i: (i, 0)),
        out_shape=jax.ShapeDtypeStruct((t, D), F32),
        compiler_params=_cparams(1), name=name,
    )(act, w_down, x1, mod_l, fg)


def _interleave(x):
    b, s, w = x.shape
    return x.reshape(b, SUB, s // SUB, w).transpose(0, 2, 1, 3).reshape(b * s, w)


def _deinterleave(y, b, s):
    return y.reshape(b, s // SUB, SUB, -1).transpose(0, 2, 1, 3).reshape(b, s, -1)


def _rope_tables():
    p = np.arange(S_LAT)
    t = (p % SUB) * (S_LAT // SUB) + p // SUB
    pos = np.stack([t // GRID_W, t % GRID_W], axis=1).astype(np.float32)
    quarter = HEAD_DIM // 4
    inv = (ROPE_BASE ** (-np.arange(quarter, dtype=np.float32) / quarter)).astype(np.float32)
    d = np.arange(LANE) % HEAD_DIM
    which = d // (HEAD_DIM // 2)
    e = d % (HEAD_DIM // 2)
    ang = pos[:, which] * inv[e % quarter][None, :]
    sign = np.where(e < quarter, -1.0, 1.0).astype(np.float32)
    return jnp.asarray(np.cos(ang).astype(np.float32)), jnp.asarray((np.sin(ang) * sign).astype(np.float32))


def _block_diag_gates(w_r, w_i):
    eye = jnp.eye(4, dtype=F32)

    def bd(w):
        w = w.reshape(2, 4, 4, LRU_BW, LRU_BW)
        w = w[:, :, :, :, None, :] * eye[None, None, :, None, :, None]
        return w.reshape(2, 4, GB, GB)

    return jnp.concatenate([bd(w_r), bd(w_i)], axis=-1).astype(BF16)


def _trunk(x_prompt, x_sample, cache_k, cache_v, state_lru, c, c_ctx, norm1_g, norm2_g, w_mod, b_mod, w_in,
           b_gate, conv_w, conv_b, w_rg, b_rg, w_ig, b_ig, lru_lambda, q_norm_g, k_norm_g, w_pool, pool_scale,
           w_br_a, w_br_b, w_br_c, w_o, w_up, ffn_conv_w, ffn_conv_b, w_down, final_norm_g,
           paths=("ctx", "lat")):
    cond = jnp.zeros((16, D), F32).at[0:N_LAT].set(c).at[N_LAT].set(c_ctx)
    mod = _modulation(cond, w_mod, b_mod)
    rope_tabs = _rope_tables()
    gmat = jnp.asarray(np.kron(np.eye(2), np.full((HEAD_DIM, HEAD_DIM), 1.0 / HEAD_DIM)), BF16)
    ck = cache_k.reshape(N_LAT, DEPTH, PAST, KV_W)
    cv = cache_v.reshape(N_LAT, DEPTH, PAST, KV_W)
    fg = final_norm_g.reshape(1, D)

    xs = {"ctx": _interleave(x_prompt), "lat": _interleave(x_sample)}
    cfg = {"ctx": dict(seq=S_CTX, tm=512, row=lambda i: N_LAT),
           "lat": dict(seq=S_LAT, tm=512, row=lambda i: i // (S_LAT // 512))}
    new_k, new_v, new_s = [], [], []
    for l in range(DEPTH):
        mod_l = mod[l].reshape(16, 1, 6 * D)
        w_in_l = w_in[l].astype(BF16)
        wg = _block_diag_gates(w_rg[l], w_ig[l])
        bg = jnp.stack([b_rg[l], b_ig[l]], axis=1)
        wa, wb, wc, wo = (w[l].astype(BF16) for w in (w_br_a, w_br_b, w_br_c, w_o))
        w_up_l, w_down_l, wp = w_up[l].astype(BF16), w_down[l].astype(BF16), w_pool[l].astype(BF16)
        qg = jnp.tile(q_norm_g[l], 2).reshape(1, LANE)
        kg = jnp.tile(k_norm_g[l], 2).reshape(1, LANE)
        for path in paths:
            seq, tm, row = cfg[path]["seq"], cfg[path]["tm"], cfg[path]["row"]
            lat = path == "lat"
            x = xs[path]
            xr, gy, q, k, v, up, gates = _inproj(
                x, mod_l, norm1_g[l].reshape(1, D), w_in_l, b_gate[l].reshape(1, 3 * D), qg, kg, gmat,
                rope_tabs if lat else None, tm=tm, row_of_tile=row, name=f"inproj_{path}{l}")
            h0 = state_lru[:, l] if lat else jnp.zeros((N_CTX, 2, LRU_W), F32)
            a_pre, ends = _lru(xr, gy, h0, conv_w[l], conv_b[l].reshape(1, LRU_W), wg, bg, lru_lambda[l],
                               seq=seq, name=f"lru_{path}{l}")
            c_pre = _pool(up, wp, pool_scale[l].reshape(1, D), seq=seq, name=f"pool_{path}{l}")
            o = _attention(q, k, v, ck if lat else None, cv if lat else None, l, seq=seq,
                           name=f"attn_{path}{l}")
            x1, h2 = _merge(a_pre, o, c_pre, gates, x, mod_l, norm2_g[l].reshape(1, D), wa, wb, wc, wo,
                            tm=tm, row_of_tile=row, name=f"merge_{path}{l}")
            act = _ffn_up(h2, w_up_l, ffn_conv_w[l], ffn_conv_b[l].reshape(1, 2 * D_FF), seq=seq,
                          name=f"ffn_up_{path}{l}")
            xs[path] = _ffn_down(act, w_down_l, x1, mod_l, fg, tm=tm, row_of_tile=row,
                                 final=(l == DEPTH - 1), name=f"ffn_down_{path}{l}")
            if not lat:
                new_k.append(_deinterleave(k, N_CTX, S_CTX).reshape(N_CTX, S_CTX, N_KV, HEAD_DIM))
                new_v.append(_deinterleave(v, N_CTX, S_CTX).reshape(N_CTX, S_CTX, N_KV, HEAD_DIM))
                new_s.append(jnp.stack([ends[:, SUB - 1], ends[:, SUB]], axis=1))
    return xs, new_k, new_v, new_s


def kernel(x_prompt, x_sample, cache_k, cache_v, state_lru, c, c_ctx, norm1_g, norm2_g, w_mod, b_mod, w_in,
           b_gate, conv_w, conv_b, w_rg, b_rg, w_ig, b_ig, lru_lambda, q_norm_g, k_norm_g, w_pool, pool_scale,
           w_br_a, w_br_b, w_br_c, w_o, w_up, ffn_conv_w, ffn_conv_b, w_down, final_norm_g):
    xs, new_k, new_v, new_s = _trunk(
        x_prompt, x_sample, cache_k, cache_v, state_lru, c, c_ctx, norm1_g, norm2_g, w_mod, b_mod, w_in,
        b_gate, conv_w, conv_b, w_rg, b_rg, w_ig, b_ig, lru_lambda, q_norm_g, k_norm_g, w_pool, pool_scale,
        w_br_a, w_br_b, w_br_c, w_o, w_up, ffn_conv_w, ffn_conv_b, w_down, final_norm_g)
    y_prompt = _deinterleave(xs["ctx"], N_CTX, S_CTX)
    y_sample = _deinterleave(xs["lat"], N_LAT, S_LAT)
    return (y_prompt, y_sample, jnp.stack(new_k, axis=1), jnp.stack(new_v, axis=1), jnp.stack(new_s, axis=1))
```

```python
import functools

import numpy as np
import jax
import jax.numpy as jnp
from jax import lax
from jax.experimental import pallas as pl
from jax.experimental.pallas import tpu as pltpu

F32 = jnp.float32
BF16 = jnp.bfloat16

D = 1024
DEPTH = 2
N_CTX, S_CTX = 16, 256
N_LAT, S_LAT = 8, 1024
PAST = 256
GRID_W = 64
LRU_W = 1024
LRU_BW = 64
LRU_C = 8.0
N_HEADS, N_KV, HEAD_DIM = 16, 4, 64
KV_W = N_KV * HEAD_DIM
ROPE_BASE = 10000.0
D_FF = 3 * D
EPS = 1e-6
SUB = 8
LANE = 128
ROPE_SHIFT = HEAD_DIM // 4
C_XR, C_YR, C_Q, C_K, C_V, C_UP, C_GL, C_END = 0, 1024, 2048, 3072, 3328, 3584, 4608, 7680

Q_PRESCALE = float(np.log2(np.e)) * HEAD_DIM ** -0.5

VMEM_LIMIT = 56 * 1024 * 1024


def _cparams(n_axes):
    return pltpu.CompilerParams(dimension_semantics=("arbitrary",) * n_axes,
                                vmem_limit_bytes=VMEM_LIMIT)


def _resident(shape):
    return pl.BlockSpec(shape, lambda *_: (0,) * len(shape), pipeline_mode=pl.Buffered(1))


def _dot(a, b):
    return jnp.dot(a, b, preferred_element_type=F32)


def _dot_nt(a, b):
    return lax.dot_general(a, b, (((1,), (1,)), ((), ())), preferred_element_type=F32)


def _sigmoid(x):
    return 0.5 * jnp.tanh(0.5 * x) + 0.5


def _rms(x):
    return x * lax.rsqrt(jnp.mean(x * x, axis=-1, keepdims=True) + EPS)


def _shift_prev(x, rm, n=1):
    r, w = x.shape
    sub = lax.broadcasted_iota(jnp.int32, (SUB, w), 0)
    pieces = []
    for m in range(r // rm):
        base = m * rm
        for t in range(n):
            lo = base + rm - SUB * (n - t)
            pieces.append(jnp.where(sub == 0, 0.0, pltpu.roll(x[lo:lo + SUB], 1, 0)))
        pieces.append(x[base:base + rm - SUB * n])
    return jnp.concatenate(pieces, axis=0)


def _shift_next(x, rm, n=1):
    r, w = x.shape
    sub = lax.broadcasted_iota(jnp.int32, (SUB, w), 0)
    pieces = []
    for m in range(r // rm):
        base = m * rm
        pieces.append(x[base + SUB * n:base + rm])
        for t in range(n):
            lo = base + SUB * t
            pieces.append(jnp.where(sub == SUB - 1, 0.0, pltpu.roll(x[lo:lo + SUB], SUB - 1, 0)))
    return jnp.concatenate(pieces, axis=0)


def _mod_kernel(cond_ref, w_ref, b_ref, out_ref):
    c = cond_ref[...]
    s = (c * jax.nn.sigmoid(c)).astype(BF16)
    out_ref[...] = _dot(s, w_ref[...].astype(BF16)) + b_ref[...]


def _modulation(cond, w_mod, b_mod):
    tn = 1536
    return pl.pallas_call(
        _mod_kernel,
        grid=(DEPTH, 6 * D // tn),
        in_specs=[pl.BlockSpec((16, D), lambda l, j: (0, 0)),
                  pl.BlockSpec((None, D, tn), lambda l, j: (l, 0, j)),
                  pl.BlockSpec((None, 1, tn), lambda l, j: (l, 0, j))],
        out_specs=pl.BlockSpec((None, 16, tn), lambda l, j: (l, 0, j)),
        out_shape=jax.ShapeDtypeStruct((DEPTH, 16, 6 * D), F32),
        compiler_params=_cparams(2),
        name="modulation",
    )(cond, w_mod, b_mod.reshape(DEPTH, 1, 6 * D))


def _inproj_kernel(*refs, rope):
    if rope:
        (x_ref, mod_ref, n1g_ref, w_ref, bgate_ref, qg_ref, kg_ref, gmat_ref, cos_ref, sin_ref,
         xr_ref, gy_ref, q_ref, k_ref, v_ref, up_ref, gate_ref) = refs
    else:
        (x_ref, mod_ref, n1g_ref, w_ref, bgate_ref, qg_ref, kg_ref, gmat_ref,
         xr_ref, gy_ref, q_ref, k_ref, v_ref, up_ref, gate_ref) = refs
    mod = mod_ref[...]
    sh1, sc1 = mod[:, 0:D], mod[:, D:2 * D]
    h = ((_rms(x_ref[...]) * n1g_ref[...]) * (1.0 + sc1) + sh1).astype(BF16)

    def mm(c0, c1):
        return _dot(h, w_ref[:, c0:c1])

    xr_ref[...] = mm(C_XR, C_YR).astype(BF16)
    gy_ref[...] = jax.nn.gelu(mm(C_YR, C_Q)).astype(BF16)

    lane = lax.broadcasted_iota(jnp.int32, (1, LANE), 1)
    first = (lane & ROPE_SHIFT) == 0

    def head_norm(xb, g):
        ms = _dot((xb * xb).astype(BF16), gmat_ref[...])
        y = (xb * lax.rsqrt(ms + EPS)) * g
        if rope:
            partner = jnp.where(first, pltpu.roll(y, LANE - ROPE_SHIFT, 1), pltpu.roll(y, ROPE_SHIFT, 1))
            y = y * cos_ref[...] + partner * sin_ref[...]
        return y

    qa = mm(C_Q, C_K)
    for c in range(N_HEADS * HEAD_DIM // LANE):
        qn = head_norm(qa[:, c * LANE:(c + 1) * LANE], qg_ref[...])
        q_ref[:, c * LANE:(c + 1) * LANE] = (qn * Q_PRESCALE).astype(BF16)
    ka = mm(C_K, C_V)
    for c in range(KV_W // LANE):
        k_ref[:, c * LANE:(c + 1) * LANE] = head_norm(ka[:, c * LANE:(c + 1) * LANE], kg_ref[...])
    v_ref[...] = mm(C_V, C_UP)
    up_ref[...] = mm(C_UP, C_GL).astype(BF16)
    for g in range(3):
        z = mm(C_GL + g * D, C_GL + (g + 1) * D) + bgate_ref[:, g * D:(g + 1) * D]
        gate_ref[:, g * D:(g + 1) * D] = jax.nn.sigmoid(z).astype(BF16)


def _inproj(x, mod_l, n1g, w_in, b_gate, qg, kg, gmat, rope_tabs, *, tm, row_of_tile, name):
    t = x.shape[0]
    rope = rope_tabs is not None
    full = lambda shape: pl.BlockSpec(shape, lambda i: (0,) * len(shape))
    in_specs = [pl.BlockSpec((tm, D), lambda i: (i, 0)),
                pl.BlockSpec((None, 1, 6 * D), lambda i: (row_of_tile(i), 0, 0)),
                full((1, D)), _resident((D, C_END)), full((1, 3 * D)), full((1, LANE)), full((1, LANE)),
                full((LANE, LANE))]
    args = [x, mod_l, n1g, w_in, b_gate, qg, kg, gmat]
    if rope:
        per_seq = S_LAT // tm
        in_specs += [pl.BlockSpec((tm, LANE), lambda i: (i % per_seq, 0))] * 2
        args += list(rope_tabs)
    row = lambda w: pl.BlockSpec((tm, w), lambda i: (i, 0))
    out_specs = [row(D), row(D), row(D), row(KV_W), row(KV_W), row(D), row(3 * D)]
    out_shape = [jax.ShapeDtypeStruct((t, D), BF16), jax.ShapeDtypeStruct((t, D), BF16),
                 jax.ShapeDtypeStruct((t, D), BF16), jax.ShapeDtypeStruct((t, KV_W), F32),
                 jax.ShapeDtypeStruct((t, KV_W), F32), jax.ShapeDtypeStruct((t, D), BF16),
                 jax.ShapeDtypeStruct((t, 3 * D), BF16)]
    return pl.pallas_call(
        functools.partial(_inproj_kernel, rope=rope),
        grid=(t // tm,), in_specs=in_specs, out_specs=out_specs, out_shape=out_shape,
        compiler_params=_cparams(1), name=name,
    )(*args)


GB = 256


def _lru_kernel(xr_ref, gy_ref, h0_ref, cw_ref, cb_ref, wg_ref, bg_ref, lam_ref,
                out_ref, ends_ref, a_s, u_s):
    r = xr_ref.shape[0]
    steps = r // SUB
    sub = lax.broadcasted_iota(jnp.int32, (SUB, GB), 0)
    for cb in range(LRU_W // GB):
        cols = slice(cb * GB, (cb + 1) * GB)
        x = xr_ref[:, cols].astype(F32)
        xn = _shift_next(x, r)
        xc = cb_ref[:, cols] + _shift_prev(x, r) * cw_ref[0:1, cols]
        xc = xc + x * cw_ref[1:2, cols]
        xc = xc + xn * cw_ref[2:3, cols]
        xc = xc + _shift_next(xn, r) * cw_ref[3:4, cols]
        lhs = xc.astype(BF16)
        for d in range(2):
            g = _dot(lhs, wg_ref[d, cb])
            rg = _sigmoid(g[:, :GB] + bg_ref[d, 0:1, cols])
            ig = _sigmoid(g[:, GB:] + bg_ref[d, 1:2, cols])
            lam = lam_ref[d:d + 1, cols]
            log_sig = jnp.minimum(lam, 0.0) - jnp.log(1.0 + jnp.exp(-jnp.abs(lam)))
            log_a = (LRU_C * rg) * log_sig
            a = jnp.exp(log_a)
            u = jnp.sqrt(jnp.tanh(-log_a) * (1.0 + a * a)) * (ig * xc)
            a_s[d] = a
            u_s[d] = u
            e0 = 0 if d == 0 else r - SUB
            edge = sub == (0 if d == 0 else SUB - 1)
            ae = a_s[d, e0:e0 + SUB, :]
            u_s[d, e0:e0 + SUB, :] = u_s[d, e0:e0 + SUB, :] + jnp.where(edge, ae * h0_ref[d:d + 1, cols], 0.0)
            a_s[d, e0:e0 + SUB, :] = jnp.where(edge, 0.0, ae)

        def step(k, carry):
            hf, pf, hb, pb = carry
            rf = pl.multiple_of(k * SUB, SUB)
            rb = pl.multiple_of((steps - 1 - k) * SUB, SUB)
            af = a_s[0, pl.ds(rf, SUB), :]
            hf = af * hf + u_s[0, pl.ds(rf, SUB), :]
            pf = af * pf
            u_s[0, pl.ds(rf, SUB), :] = hf
            a_s[0, pl.ds(rf, SUB), :] = pf
            ab = a_s[1, pl.ds(rb, SUB), :]
            hb = ab * hb + u_s[1, pl.ds(rb, SUB), :]
            pb = ab * pb
            u_s[1, pl.ds(rb, SUB), :] = hb
            a_s[1, pl.ds(rb, SUB), :] = pb
            return hf, pf, hb, pb

        zero = jnp.zeros((SUB, GB), F32)
        one = jnp.ones((SUB, GB), F32)
        hf, pf, hb, pb = lax.fori_loop(0, steps, step, (zero, one, zero, one), unroll=8)

        ef, eb = hf, hb
        for sh in (1, 2, 4):
            keep_f = sub >= sh
            ef = ef + pf * jnp.where(keep_f, pltpu.roll(ef, sh, 0), 0.0)
            pf = pf * jnp.where(keep_f, pltpu.roll(pf, sh, 0), 1.0)
            keep_b = sub < SUB - sh
            eb = eb + pb * jnp.where(keep_b, pltpu.roll(eb, SUB - sh, 0), 0.0)
            pb = pb * jnp.where(keep_b, pltpu.roll(pb, SUB - sh, 0), 1.0)
        ends_ref[0:SUB, cols] = ef
        ends_ref[SUB:2 * SUB, cols] = eb
        cf = jnp.where(sub >= 1, pltpu.roll(ef, 1, 0), 0.0)
        cbk = jnp.where(sub < SUB - 1, pltpu.roll(eb, SUB - 1, 0), 0.0)
        cf2 = jnp.concatenate([cf, cf], axis=0)
        cb2 = jnp.concatenate([cbk, cbk], axis=0)

        def fix(m, _):
            rows = pl.ds(pl.multiple_of(m * 2 * SUB, 2 * SUB), 2 * SUB)
            hft = u_s[0, rows, :] + a_s[0, rows, :] * cf2
            hbt = u_s[1, rows, :] + a_s[1, rows, :] * cb2
            out_ref[rows, cols] = ((hft + hbt) * gy_ref[rows, cols].astype(F32)).astype(BF16)
            return 0

        lax.fori_loop(0, r // (2 * SUB), fix, 0, unroll=4)


def _lru(xr, gy, h0, conv_w, conv_b, wg, bg, lam, *, seq, name):
    t = xr.shape[0]
    nseq = t // seq
    full = lambda shape: pl.BlockSpec(shape, lambda i: (0,) * len(shape))
    return pl.pallas_call(
        _lru_kernel,
        grid=(nseq,),
        in_specs=[pl.BlockSpec((seq, D), lambda i: (i, 0)), pl.BlockSpec((seq, D), lambda i: (i, 0)),
                  pl.BlockSpec((None, 2, LRU_W), lambda i: (i, 0, 0)),
                  full((4, LRU_W)), full((1, LRU_W)), full((2, LRU_W // GB, GB, 2 * GB)),
                  full((2, 2, LRU_W)), full((2, LRU_W))],
        out_specs=[pl.BlockSpec((seq, D), lambda i: (i, 0)),
                   pl.BlockSpec((None, 2 * SUB, LRU_W), lambda i: (i, 0, 0))],
        out_shape=[jax.ShapeDtypeStruct((t, D), BF16), jax.ShapeDtypeStruct((nseq, 2 * SUB, LRU_W), F32)],
        scratch_shapes=[pltpu.VMEM((2, seq, GB), F32), pltpu.VMEM((2, seq, GB), F32)],
        compiler_params=_cparams(1), name=name,
    )(xr, gy, h0, conv_w, conv_b, wg, bg, lam)


POOL_G = 256


def _pool_kernel(up_ref, wp_ref, scale_ref, out_ref):
    r = up_ref.shape[0]
    steps = r // SUB
    row = lax.broadcasted_iota(jnp.int32, (r, 1), 0)
    t = (row & (SUB - 1)) * steps + jnp.right_shift(row, 3)
    for g in range(4):
        cols = slice(g * POOL_G, (g + 1) * POOL_G)
        x = up_ref[:, cols].astype(F32)
        half = 1 << g
        back, fwd = x, x
        for lvl in range(g):
            n = 1 << lvl
            back = back + _shift_prev(back, r, n)
            fwd = fwd + _shift_next(fwd, r, n)
        win = _shift_prev(back, r) + fwd
        cnt = jnp.minimum(t + half, r) - jnp.maximum(t - half, 0)
        d = (win / cnt.astype(F32) - x).astype(BF16)
        out_ref[:, cols] = (_dot(d, wp_ref[g]) * scale_ref[:, cols]).astype(BF16)


def _pool(up, w_pool, scale, *, seq, name):
    t = up.shape[0]
    return pl.pallas_call(
        _pool_kernel,
        grid=(t // seq,),
        in_specs=[pl.BlockSpec((seq, D), lambda i: (i, 0)),
                  pl.BlockSpec((4, POOL_G, POOL_G), lambda i: (0, 0, 0)),
                  pl.BlockSpec((1, D), lambda i: (0, 0))],
        out_specs=pl.BlockSpec((seq, D), lambda i: (i, 0)),
        out_shape=jax.ShapeDtypeStruct((t, D), BF16),
        compiler_params=_cparams(1), name=name,
    )(up, w_pool, scale)


def _attn_kernel(*refs, cached, TQ):
    if cached:
        q_ref, k_ref, v_ref, ck_ref, cv_ref, o_ref = refs
        kall = jnp.concatenate([ck_ref[...], k_ref[...]], axis=0)
        vall = jnp.concatenate([cv_ref[...], v_ref[...]], axis=0)
    else:
        q_ref, k_ref, v_ref, o_ref = refs
        kall = k_ref[...]
        vall = v_ref[...]
    sq = q_ref.shape[0]
    half = HEAD_DIM
    lane = lax.broadcasted_iota(jnp.int32, (1, LANE), 1)
    for kvh in range(N_KV):
        blk = slice((kvh // 2) * LANE, (kvh // 2 + 1) * LANE)
        in_low = (lane < half) if kvh % 2 == 0 else (lane >= half)
        k_own = jnp.where(in_low, kall[:, blk], 0.0)
        k_oth = pltpu.roll(k_own, half, 1)
        k_lo, k_hi = (k_own, k_oth) if kvh % 2 == 0 else (k_oth, k_own)
        k_lo, k_hi = k_lo.astype(BF16), k_hi.astype(BF16)
        r0 = (kvh % 2) * HEAD_DIM
        v_t = vall[:, blk].T[r0:r0 + HEAD_DIM].astype(BF16)
        for pair in range(2):
            c0 = kvh * 4 * HEAD_DIM + pair * LANE

            def body(i, _, c0=c0, k_lo=k_lo, k_hi=k_hi, v_t=v_t):
                rows = pl.ds(pl.multiple_of(i * TQ, TQ), TQ)
                qp = q_ref[rows, c0:c0 + LANE]
                outs = []
                for kk in (k_lo, k_hi):
                    s = _dot_nt(kk, qp)
                    e = jnp.exp2(s - jnp.max(s, axis=0, keepdims=True))
                    l = jnp.sum(e, axis=0, keepdims=True)
                    outs.append(_dot(v_t, e.astype(BF16)) / l)
                o_ref[rows, c0:c0 + LANE] = jnp.concatenate(outs, axis=0).T.astype(BF16)
                return 0

            lax.fori_loop(0, sq // TQ, body, 0)


def _attention(q, k, v, cache_k, cache_v, layer, *, seq, name):
    t = q.shape[0]
    cached = cache_k is not None
    in_specs = [pl.BlockSpec((seq, D), lambda i: (i, 0)), pl.BlockSpec((seq, KV_W), lambda i: (i, 0)),
                pl.BlockSpec((seq, KV_W), lambda i: (i, 0))]
    args = [q, k, v]
    if cached:
        in_specs += [pl.BlockSpec((None, None, PAST, KV_W), lambda i: (i, layer, 0, 0))] * 2
        args += [cache_k, cache_v]
    return pl.pallas_call(
        functools.partial(_attn_kernel, cached=cached, TQ=min(seq, 1024)),
        grid=(t // seq,), in_specs=in_specs,
        out_specs=pl.BlockSpec((seq, D), lambda i: (i, 0)),
        out_shape=jax.ShapeDtypeStruct((t, D), BF16),
        compiler_params=_cparams(1), name=name,
    )(*args)


def _merge_kernel(a_ref, o_ref, c_ref, gate_ref, x_ref, mod_ref, n2g_ref, wa_ref, wb_ref, wc_ref, wo_ref,
                  x1_ref, h2_ref):
    mix = gate_ref[:, 0:D].astype(F32) * _dot(a_ref[...], wa_ref[...])
    mix = mix + gate_ref[:, D:2 * D].astype(F32) * _dot(o_ref[...], wb_ref[...])
    mix = mix + gate_ref[:, 2 * D:3 * D].astype(F32) * _dot(c_ref[...], wc_ref[...])
    out = _dot(mix.astype(BF16), wo_ref[...])
    mod = mod_ref[...]
    g1, sh2, sc2 = mod[:, 2 * D:3 * D], mod[:, 3 * D:4 * D], mod[:, 4 * D:5 * D]
    x1 = x_ref[...] + g1 * out
    x1_ref[...] = x1
    h2_ref[...] = ((_rms(x1) * n2g_ref[...]) * (1.0 + sc2) + sh2).astype(BF16)


def _merge(a, o, c, gates, x, mod_l, n2g, wa, wb, wc, wo, *, tm, row_of_tile, name):
    t = x.shape[0]
    row = lambda w: pl.BlockSpec((tm, w), lambda i: (i, 0))
    wspec = _resident((D, D))
    return pl.pallas_call(
        _merge_kernel,
        grid=(t // tm,),
        in_specs=[row(D), row(D), row(D), row(3 * D), row(D),
                  pl.BlockSpec((None, 1, 6 * D), lambda i: (row_of_tile(i), 0, 0)),
                  pl.BlockSpec((1, D), lambda i: (0, 0)), wspec, wspec, wspec, wspec],
        out_specs=[row(D), row(D)],
        out_shape=[jax.ShapeDtypeStruct((t, D), F32), jax.ShapeDtypeStruct((t, D), BF16)],
        compiler_params=_cparams(1), name=name,
    )(a, o, c, gates, x, mod_l, n2g, wa, wb, wc, wo)


FF_TN = 512
FF_ROWS = 1024


def _ffn_up_kernel(h_ref, wg_ref, wv_ref, cwg_ref, cwv_ref, cbg_ref, cbv_ref, out_ref, *, seq):
    h = h_ref[...]

    def conv(z, cw_ref, cb_ref):
        y = cb_ref[...] + _shift_prev(z, seq) * cw_ref[0:1, :]
        y = y + z * cw_ref[1:2, :]
        return y + _shift_next(z, seq) * cw_ref[2:3, :]

    g = conv(_dot(h, wg_ref[...]), cwg_ref, cbg_ref)
    v = conv(_dot(h, wv_ref[...]), cwv_ref, cbv_ref)
    out_ref[...] = ((g * jax.nn.sigmoid(g)) * v).astype(BF16)


def _ffn_up(h2, w_up, cw, cb, *, seq, name):
    t = h2.shape[0]
    nj = D_FF // FF_TN
    return pl.pallas_call(
        functools.partial(_ffn_up_kernel, seq=seq),
        grid=(t // FF_ROWS, nj),
        in_specs=[pl.BlockSpec((FF_ROWS, D), lambda i, j: (i, 0)),
                  pl.BlockSpec((D, FF_TN), lambda i, j: (0, j)),
                  pl.BlockSpec((D, FF_TN), lambda i, j: (0, nj + j)),
                  pl.BlockSpec((3, FF_TN), lambda i, j: (0, j)),
                  pl.BlockSpec((3, FF_TN), lambda i, j: (0, nj + j)),
                  pl.BlockSpec((1, FF_TN), lambda i, j: (0, j)),
                  pl.BlockSpec((1, FF_TN), lambda i, j: (0, nj + j))],
        out_specs=pl.BlockSpec((FF_ROWS, FF_TN), lambda i, j: (i, j)),
        out_shape=jax.ShapeDtypeStruct((t, D_FF), BF16),
        compiler_params=_cparams(2), name=name,
    )(h2, w_up, w_up, cw, cw, cb, cb)


def _ffn_down_kernel(act_ref, w_ref, x1_ref, mod_ref, fg_ref, out_ref, *, final):
    g2 = mod_ref[:, 5 * D:6 * D]
    x2 = x1_ref[...] + g2 * _dot(act_ref[...], w_ref[...])
    out_ref[...] = _rms(x2) * fg_ref[...] if final else x2


def _ffn_down(act, w_down, x1, mod_l, fg, *, tm, row_of_tile, final, name):
    t = x1.shape[0]
    return pl.pallas_call(
        functools.partial(_ffn_down_kernel, final=final),
        grid=(t // tm,),
        in_specs=[pl.BlockSpec((tm, D_FF), lambda i: (i, 0)),
                  _resident((D_FF, D)),
                  pl.BlockSpec((tm, D), lambda i: (i, 0)),
                  pl.BlockSpec((None, 1, 6 * D), lambda i: (row_of_tile(i), 0, 0)),
                  pl.BlockSpec((1, D), lambda i: (0, 0))],
        out_specs=pl.BlockSpec((tm, D), lambda i: (i, 0)),
        out_shape=jax.ShapeDtypeStruct((t, D), F32),
        compiler_params=_cparams(1), name=name,
    )(act, w_down, x1, mod_l, fg)


def _interleave(x):
    b, s, w = x.shape
    return x.reshape(b, SUB, s // SUB, w).transpose(0, 2, 1, 3).reshape(b * s, w)


def _deinterleave(y, b, s):
    return y.reshape(b, s // SUB, SUB, -1).transpose(0, 2, 1, 3).reshape(b, s, -1)


def _rope_tables():
    p = np.arange(S_LAT)
    t = (p % SUB) * (S_LAT // SUB) + p // SUB
    pos = np.stack([t // GRID_W, t % GRID_W], axis=1).astype(np.float32)
    quarter = HEAD_DIM // 4
    inv = (ROPE_BASE ** (-np.arange(quarter, dtype=np.float32) / quarter)).astype(np.float32)
    d = np.arange(LANE) % HEAD_DIM
    which = d // (HEAD_DIM // 2)
    e = d % (HEAD_DIM // 2)
    ang = pos[:, which] * inv[e % quarter][None, :]
    sign = np.where(e < quarter, -1.0, 1.0).astype(np.float32)
    return jnp.asarray(np.cos(ang).astype(np.float32)), jnp.asarray((np.sin(ang) * sign).astype(np.float32))


def _block_diag_gates(w_r, w_i):
    eye = jnp.eye(4, dtype=F32)

    def bd(w):
        w = w.reshape(2, 4, 4, LRU_BW, LRU_BW)
        w = w[:, :, :, :, None, :] * eye[None, None, :, None, :, None]
        return w.reshape(2, 4, GB, GB)

    return jnp.concatenate([bd(w_r), bd(w_i)], axis=-1).astype(BF16)


def _trunk(x_prompt, x_sample, cache_k, cache_v, state_lru, c, c_ctx, norm1_g, norm2_g, w_mod, b_mod, w_in,
           b_gate, conv_w, conv_b, w_rg, b_rg, w_ig, b_ig, lru_lambda, q_norm_g, k_norm_g, w_pool, pool_scale,
           w_br_a, w_br_b, w_br_c, w_o, w_up, ffn_conv_w, ffn_conv_b, w_down, final_norm_g,
           paths=("ctx", "lat")):
    cond = jnp.zeros((16, D), F32).at[0:N_LAT].set(c).at[N_LAT].set(c_ctx)
    mod = _modulation(cond, w_mod, b_mod)
    rope_tabs = _rope_tables()
    gmat = jnp.asarray(np.kron(np.eye(2), np.full((HEAD_DIM, HEAD_DIM), 1.0 / HEAD_DIM)), BF16)
    ck = cache_k.reshape(N_LAT, DEPTH, PAST, KV_W)
    cv = cache_v.reshape(N_LAT, DEPTH, PAST, KV_W)
    fg = final_norm_g.reshape(1, D)

    xs = {"ctx": _interleave(x_prompt), "lat": _interleave(x_sample)}
    cfg = {"ctx": dict(seq=S_CTX, tm=512, row=lambda i: N_LAT),
           "lat": dict(seq=S_LAT, tm=512, row=lambda i: i // (S_LAT // 512))}
    new_k, new_v, new_s = [], [], []
    for l in range(DEPTH):
        mod_l = mod[l].reshape(16, 1, 6 * D)
        w_in_l = w_in[l].astype(BF16)
        wg = _block_diag_gates(w_rg[l], w_ig[l])
        bg = jnp.stack([b_rg[l], b_ig[l]], axis=1)
        wa, wb, wc, wo = (w[l].astype(BF16) for w in (w_br_a, w_br_b, w_br_c, w_o))
        w_up_l, w_down_l, wp = w_up[l].astype(BF16), w_down[l].astype(BF16), w_pool[l].astype(BF16)
        qg = jnp.tile(q_norm_g[l], 2).reshape(1, LANE)
        kg = jnp.tile(k_norm_g[l], 2).reshape(1, LANE)
        for path in paths:
            seq, tm, row = cfg[path]["seq"], cfg[path]["tm"], cfg[path]["row"]
            lat = path == "lat"
            x = xs[path]
            xr, gy, q, k, v, up, gates = _inproj(
                x, mod_l, norm1_g[l].reshape(1, D), w_in_l, b_gate[l].reshape(1, 3 * D), qg, kg, gmat,
                rope_tabs if lat else None, tm=tm, row_of_tile=row, name=f"inproj_{path}{l}")
            h0 = state_lru[:, l] if lat else jnp.zeros((N_CTX, 2, LRU_W), F32)
            a_pre, ends = _lru(xr, gy, h0, conv_w[l], conv_b[l].reshape(1, LRU_W), wg, bg, lru_lambda[l],
                               seq=seq, name=f"lru_{path}{l}")
            c_pre = _pool(up, wp, pool_scale[l].reshape(1, D), seq=seq, name=f"pool_{path}{l}")
            o = _attention(q, k, v, ck if lat else None, cv if lat else None, l, seq=seq,
                           name=f"attn_{path}{l}")
            x1, h2 = _merge(a_pre, o, c_pre, gates, x, mod_l, norm2_g[l].reshape(1, D), wa, wb, wc, wo,
                            tm=tm, row_of_tile=row, name=f"merge_{path}{l}")
            act = _ffn_up(h2, w_up_l, ffn_conv_w[l], ffn_conv_b[l].reshape(1, 2 * D_FF), seq=seq,
                          name=f"ffn_up_{path}{l}")
            xs[path] = _ffn_down(act, w_down_l, x1, mod_l, fg, tm=tm, row_of_tile=row,
                                 final=(l == DEPTH - 1), name=f"ffn_down_{path}{l}")
            if not lat:
                new_k.append(_deinterleave(k, N_CTX, S_CTX).reshape(N_CTX, S_CTX, N_KV, HEAD_DIM))
                new_v.append(_deinterleave(v, N_CTX, S_CTX).reshape(N_CTX, S_CTX, N_KV, HEAD_DIM))
                new_s.append(jnp.stack([ends[:, SUB - 1], ends[:, SUB]], axis=1))
    return xs, new_k, new_v, new_s


def kernel(x_prompt, x_sample, cache_k, cache_v, state_lru, c, c_ctx, norm1_g, norm2_g, w_mod, b_mod, w_in,
           b_gate, conv_w, conv_b, w_rg, b_rg, w_ig, b_ig, lru_lambda, q_norm_g, k_norm_g, w_pool, pool_scale,
           w_br_a, w_br_b, w_br_c, w_o, w_up, ffn_conv_w, ffn_conv_b, w_down, final_norm_g):
    xs, new_k, new_v, new_s = _trunk(
        x_prompt, x_sample, cache_k, cache_v, state_lru, c, c_ctx, norm1_g, norm2_g, w_mod, b_mod, w_in,
        b_gate, conv_w, conv_b, w_rg, b_rg, w_ig, b_ig, lru_lambda, q_norm_g, k_norm_g, w_pool, pool_scale,
        w_br_a, w_br_b, w_br_c, w_o, w_up, ffn_conv_w, ffn_conv_b, w_down, final_norm_g)
    y_prompt = _deinterleave(xs["ctx"], N_CTX, S_CTX)
    y_sample = _deinterleave(xs["lat"], N_LAT, S_LAT)
    return (y_prompt, y_sample, jnp.stack(new_k, axis=1), jnp.stack(new_v, axis=1), jnp.stack(new_s, axis=1))
```

```python
import functools

import numpy as np
import jax
import jax.numpy as jnp
from jax import lax
from jax.experimental import pallas as pl
from jax.experimental.pallas import tpu as pltpu

F32 = jnp.float32
BF16 = jnp.bfloat16

D = 1024
DEPTH = 2
N_CTX, S_CTX = 16, 256
N_LAT, S_LAT = 8, 1024
PAST = 256
GRID_W = 64
LRU_W = 1024
LRU_BW = 64
LRU_C = 8.0
N_HEADS, N_KV, HEAD_DIM = 16, 4, 64
KV_W = N_KV * HEAD_DIM
ROPE_BASE = 10000.0
D_FF = 3 * D
EPS = 1e-6
SUB = 8
LANE = 128
ROPE_SHIFT = HEAD_DIM // 4
C_XR, C_YR, C_Q, C_K, C_V, C_UP, C_GL, C_END = 0, 1024, 2048, 3072, 3328, 3584, 4608, 7680

LOG2E = float(np.log2(np.e))
Q_PRESCALE = LOG2E * HEAD_DIM ** -0.5

VMEM_LIMIT = 56 * 1024 * 1024


def _cparams(n_axes):
    return pltpu.CompilerParams(dimension_semantics=("arbitrary",) * n_axes,
                                vmem_limit_bytes=VMEM_LIMIT)


def _resident(shape):
    return pl.BlockSpec(shape, lambda *_: (0,) * len(shape), pipeline_mode=pl.Buffered(1))


def _dot(a, b):
    return jnp.dot(a, b, preferred_element_type=F32)


def _dot_nt(a, b):
    return lax.dot_general(a, b, (((1,), (1,)), ((), ())), preferred_element_type=F32)


def _rms(x):
    return x * lax.rsqrt(jnp.mean(x * x, axis=-1, keepdims=True) + EPS)


def _shift_prev(x, rm, n=1):
    r, w = x.shape
    sub = lax.broadcasted_iota(jnp.int32, (SUB, w), 0)
    pieces = []
    for m in range(r // rm):
        base = m * rm
        for t in range(n):
            lo = base + rm - SUB * (n - t)
            pieces.append(jnp.where(sub == 0, 0.0, pltpu.roll(x[lo:lo + SUB], 1, 0)))
        pieces.append(x[base:base + rm - SUB * n])
    return jnp.concatenate(pieces, axis=0)


def _shift_next(x, rm, n=1):
    r, w = x.shape
    sub = lax.broadcasted_iota(jnp.int32, (SUB, w), 0)
    pieces = []
    for m in range(r // rm):
        base = m * rm
        pieces.append(x[base + SUB * n:base + rm])
        for t in range(n):
            lo = base + SUB * t
            pieces.append(jnp.where(sub == SUB - 1, 0.0, pltpu.roll(x[lo:lo + SUB], SUB - 1, 0)))
    return jnp.concatenate(pieces, axis=0)


def _mod_kernel(cond_ref, w_ref, b_ref, out_ref):
    c = cond_ref[...]
    s = (c * jax.nn.sigmoid(c)).astype(BF16)
    out_ref[...] = _dot(s, w_ref[...].astype(BF16)) + b_ref[...]


def _modulation(cond, w_mod, b_mod):
    tn = 1536
    return pl.pallas_call(
        _mod_kernel,
        grid=(DEPTH, 6 * D // tn),
        in_specs=[pl.BlockSpec((16, D), lambda l, j: (0, 0)),
                  pl.BlockSpec((None, D, tn), lambda l, j: (l, 0, j)),
                  pl.BlockSpec((None, 1, tn), lambda l, j: (l, 0, j))],
        out_specs=pl.BlockSpec((None, 16, tn), lambda l, j: (l, 0, j)),
        out_shape=jax.ShapeDtypeStruct((DEPTH, 16, 6 * D), F32),
        compiler_params=_cparams(2),
        name="modulation",
    )(cond, w_mod, b_mod.reshape(DEPTH, 1, 6 * D))


def _inproj_kernel(*refs, rope):
    if rope:
        (x_ref, mod_ref, n1g_ref, w_ref, bgate_ref, qg_ref, kg_ref, gmat_ref, cos_ref, sin_ref,
         xr_ref, gy_ref, q_ref, k_ref, v_ref, up_ref, gate_ref) = refs
    else:
        (x_ref, mod_ref, n1g_ref, w_ref, bgate_ref, qg_ref, kg_ref, gmat_ref,
         xr_ref, gy_ref, q_ref, k_ref, v_ref, up_ref, gate_ref) = refs
    mod = mod_ref[...]
    sh1, sc1 = mod[:, 0:D], mod[:, D:2 * D]
    h = ((_rms(x_ref[...]) * n1g_ref[...]) * (1.0 + sc1) + sh1).astype(BF16)

    def mm(c0, c1):
        return _dot(h, w_ref[:, c0:c1])

    xr_ref[...] = mm(C_XR, C_YR).astype(BF16)
    gy_ref[...] = jax.nn.gelu(mm(C_YR, C_Q)).astype(BF16)

    lane = lax.broadcasted_iota(jnp.int32, (1, LANE), 1)
    first = (lane & ROPE_SHIFT) == 0

    def head_norm(xb, g):
        ms = _dot((xb * xb).astype(BF16), gmat_ref[...])
        y = (xb * lax.rsqrt(ms + EPS)) * g
        if rope:
            partner = jnp.where(first, pltpu.roll(y, LANE - ROPE_SHIFT, 1), pltpu.roll(y, ROPE_SHIFT, 1))
            y = y * cos_ref[...] + partner * sin_ref[...]
        return y

    qa = mm(C_Q, C_K)
    for c in range(N_HEADS * HEAD_DIM // LANE):
        qn = head_norm(qa[:, c * LANE:(c + 1) * LANE], qg_ref[...])
        q_ref[:, c * LANE:(c + 1) * LANE] = (qn * Q_PRESCALE).astype(BF16)
    ka = mm(C_K, C_V)
    for c in range(KV_W // LANE):
        k_ref[:, c * LANE:(c + 1) * LANE] = head_norm(ka[:, c * LANE:(c + 1) * LANE], kg_ref[...])
    v_ref[...] = mm(C_V, C_UP)
    up_ref[...] = mm(C_UP, C_GL).astype(BF16)
    for g in range(3):
        z = mm(C_GL + g * D, C_GL + (g + 1) * D) + bgate_ref[:, g * D:(g + 1) * D]
        gate_ref[:, g * D:(g + 1) * D] = jax.nn.sigmoid(z).astype(BF16)


def _inproj(x, mod_l, n1g, w_in, b_gate, qg, kg, gmat, rope_tabs, *, tm, row_of_tile, name):
    t = x.shape[0]
    rope = rope_tabs is not None
    full = lambda shape: pl.BlockSpec(shape, lambda i: (0,) * len(shape))
    in_specs = [pl.BlockSpec((tm, D), lambda i: (i, 0)),
                pl.BlockSpec((None, 1, 6 * D), lambda i: (row_of_tile(i), 0, 0)),
                full((1, D)), _resident((D, C_END)), full((1, 3 * D)), full((1, LANE)), full((1, LANE)),
                full((LANE, LANE))]
    args = [x, mod_l, n1g, w_in, b_gate, qg, kg, gmat]
    if rope:
        per_seq = S_LAT // tm
        in_specs += [pl.BlockSpec((tm, LANE), lambda i: (i % per_seq, 0))] * 2
        args += list(rope_tabs)
    row = lambda w: pl.BlockSpec((tm, w), lambda i: (i, 0))
    out_specs = [row(D), row(D), row(D), row(KV_W), row(KV_W), row(D), row(3 * D)]
    out_shape = [jax.ShapeDtypeStruct((t, D), BF16), jax.ShapeDtypeStruct((t, D), BF16),
                 jax.ShapeDtypeStruct((t, D), BF16), jax.ShapeDtypeStruct((t, KV_W), F32),
                 jax.ShapeDtypeStruct((t, KV_W), F32), jax.ShapeDtypeStruct((t, D), BF16),
                 jax.ShapeDtypeStruct((t, 3 * D), BF16)]
    return pl.pallas_call(
        functools.partial(_inproj_kernel, rope=rope),
        grid=(t // tm,), in_specs=in_specs, out_specs=out_specs, out_shape=out_shape,
        compiler_params=_cparams(1), name=name,
    )(*args)


GB = 256


def _lru_kernel(xr_ref, gy_ref, h0_ref, cw_ref, cb_ref, wg_ref, bg_ref, lam_ref,
                out_ref, ends_ref, a_s, u_s):
    r = xr_ref.shape[0]
    steps = r // SUB
    sub = lax.broadcasted_iota(jnp.int32, (SUB, GB), 0)
    for cb in range(LRU_W // GB):
        cols = slice(cb * GB, (cb + 1) * GB)
        x = xr_ref[:, cols].astype(F32)
        xn = _shift_next(x, r)
        xc = cb_ref[:, cols] + _shift_prev(x, r) * cw_ref[0:1, cols]
        xc = xc + x * cw_ref[1:2, cols]
        xc = xc + xn * cw_ref[2:3, cols]
        xc = xc + _shift_next(xn, r) * cw_ref[3:4, cols]
        lhs = xc.astype(BF16)
        xh = 0.5 * xc
        for d in range(2):
            g = _dot(lhs, wg_ref[d, cb])
            tr = jnp.tanh(g[:, :GB] + bg_ref[d, 0:1, cols])
            ti = jnp.tanh(g[:, GB:] + bg_ref[d, 1:2, cols])
            lam = lam_ref[d:d + 1, cols]
            log_sig = jnp.minimum(lam, 0.0) - jnp.log(1.0 + jnp.exp(-jnp.abs(lam)))
            ch = (0.5 * LRU_C) * log_sig
            a = jnp.exp2(tr * (ch * LOG2E) + ch * LOG2E)
            m2 = jnp.tanh(tr * (-ch) - ch) * (1.0 + a * a)
            mult = jnp.where(m2 > 0.0, m2 * lax.rsqrt(m2), 0.0)
            u = mult * (ti * xh + xh)
            a_s[d] = a
            u_s[d] = u
            e0 = 0 if d == 0 else r - SUB
            edge = sub == (0 if d == 0 else SUB - 1)
            ae = a_s[d, e0:e0 + SUB, :]
            u_s[d, e0:e0 + SUB, :] = u_s[d, e0:e0 + SUB, :] + jnp.where(edge, ae * h0_ref[d:d + 1, cols], 0.0)
            a_s[d, e0:e0 + SUB, :] = jnp.where(edge, 0.0, ae)

        def step(k, carry):
            hf, pf, hb, pb = carry
            rf = pl.multiple_of(k * SUB, SUB)
            rb = pl.multiple_of((steps - 1 - k) * SUB, SUB)
            af = a_s[0, pl.ds(rf, SUB), :]
            hf = af * hf + u_s[0, pl.ds(rf, SUB), :]
            pf = af * pf
            u_s[0, pl.ds(rf, SUB), :] = hf
            a_s[0, pl.ds(rf, SUB), :] = pf
            ab = a_s[1, pl.ds(rb, SUB), :]
            hb = ab * hb + u_s[1, pl.ds(rb, SUB), :]
            pb = ab * pb
            u_s[1, pl.ds(rb, SUB), :] = hb
            a_s[1, pl.ds(rb, SUB), :] = pb
            return hf, pf, hb, pb

        zero = jnp.zeros((SUB, GB), F32)
        one = jnp.ones((SUB, GB), F32)
        hf, pf, hb, pb = lax.fori_loop(0, steps, step, (zero, one, zero, one), unroll=8)

        ef, eb = hf, hb
        for sh in (1, 2, 4):
            keep_f = sub >= sh
            ef = ef + pf * jnp.where(keep_f, pltpu.roll(ef, sh, 0), 0.0)
            pf = pf * jnp.where(keep_f, pltpu.roll(pf, sh, 0), 1.0)
            keep_b = sub < SUB - sh
            eb = eb + pb * jnp.where(keep_b, pltpu.roll(eb, SUB - sh, 0), 0.0)
            pb = pb * jnp.where(keep_b, pltpu.roll(pb, SUB - sh, 0), 1.0)
        ends_ref[0:SUB, cols] = ef
        ends_ref[SUB:2 * SUB, cols] = eb
        cf = jnp.where(sub >= 1, pltpu.roll(ef, 1, 0), 0.0)
        cbk = jnp.where(sub < SUB - 1, pltpu.roll(eb, SUB - 1, 0), 0.0)
        cf2 = jnp.concatenate([cf, cf], axis=0)
        cb2 = jnp.concatenate([cbk, cbk], axis=0)

        def fix(m, _):
            rows = pl.ds(pl.multiple_of(m * 2 * SUB, 2 * SUB), 2 * SUB)
            hft = u_s[0, rows, :] + a_s[0, rows, :] * cf2
            hbt = u_s[1, rows, :] + a_s[1, rows, :] * cb2
            out_ref[rows, cols] = ((hft + hbt) * gy_ref[rows, cols].astype(F32)).astype(BF16)
            return 0

        lax.fori_loop(0, r // (2 * SUB), fix, 0, unroll=4)


def _lru(xr, gy, h0, conv_w, conv_b, wg, bg, lam, *, seq, name):
    t = xr.shape[0]
    nseq = t // seq
    full = lambda shape: pl.BlockSpec(shape, lambda i: (0,) * len(shape))
    return pl.pallas_call(
        _lru_kernel,
        grid=(nseq,),
        in_specs=[pl.BlockSpec((seq, D), lambda i: (i, 0)), pl.BlockSpec((seq, D), lambda i: (i, 0)),
                  pl.BlockSpec((None, 2, LRU_W), lambda i: (i, 0, 0)),
                  full((4, LRU_W)), full((1, LRU_W)), full((2, LRU_W // GB, GB, 2 * GB)),
                  full((2, 2, LRU_W)), full((2, LRU_W))],
        out_specs=[pl.BlockSpec((seq, D), lambda i: (i, 0)),
                   pl.BlockSpec((None, 2 * SUB, LRU_W), lambda i: (i, 0, 0))],
        out_shape=[jax.ShapeDtypeStruct((t, D), BF16), jax.ShapeDtypeStruct((nseq, 2 * SUB, LRU_W), F32)],
        scratch_shapes=[pltpu.VMEM((2, seq, GB), F32), pltpu.VMEM((2, seq, GB), F32)],
        compiler_params=_cparams(1), name=name,
    )(xr, gy, h0, conv_w, conv_b, wg, bg, lam)


POOL_G = 256


def _pool_kernel(up_ref, wp_ref, scale_ref, out_ref):
    r = up_ref.shape[0]
    steps = r // SUB
    row = lax.broadcasted_iota(jnp.int32, (r, 1), 0)
    t = (row & (SUB - 1)) * steps + jnp.right_shift(row, 3)
    for g in range(4):
        cols = slice(g * POOL_G, (g + 1) * POOL_G)
        x = up_ref[:, cols].astype(F32)
        half = 1 << g
        back, fwd = x, x
        for lvl in range(g):
            n = 1 << lvl
            back = back + _shift_prev(back, r, n)
            fwd = fwd + _shift_next(fwd, r, n)
        win = _shift_prev(back, r) + fwd
        cnt = jnp.minimum(t + half, r) - jnp.maximum(t - half, 0)
        d = (win / cnt.astype(F32) - x).astype(BF16)
        out_ref[:, cols] = (_dot(d, wp_ref[g]) * scale_ref[:, cols]).astype(BF16)


def _pool(up, w_pool, scale, *, seq, name):
    t = up.shape[0]
    return pl.pallas_call(
        _pool_kernel,
        grid=(t // seq,),
        in_specs=[pl.BlockSpec((seq, D), lambda i: (i, 0)),
                  pl.BlockSpec((4, POOL_G, POOL_G), lambda i: (0, 0, 0)),
                  pl.BlockSpec((1, D), lambda i: (0, 0))],
        out_specs=pl.BlockSpec((seq, D), lambda i: (i, 0)),
        out_shape=jax.ShapeDtypeStruct((t, D), BF16),
        compiler_params=_cparams(1), name=name,
    )(up, w_pool, scale)


def _attn_kernel(*refs, cached):
    transposed = cached
    if cached:
        q_ref, k_ref, v_ref, ck_ref, cv_ref, o_ref = refs
        kall = jnp.concatenate([ck_ref[...], k_ref[...]], axis=0)
        vall = jnp.concatenate([cv_ref[...], v_ref[...]], axis=0)
    else:
        q_ref, k_ref, v_ref, o_ref = refs
        kall = k_ref[...]
        vall = v_ref[...]
    half = HEAD_DIM
    lane = lax.broadcasted_iota(jnp.int32, (1, LANE), 1)

    def both_halves(x, own_low):
        own = jnp.where((lane < half) if own_low else (lane >= half), x, 0.0)
        oth = pltpu.roll(own, half, 1)
        lo, hi = (own, oth) if own_low else (oth, own)
        return lo.astype(BF16), hi.astype(BF16)

    for kvh in range(N_KV):
        blk = slice((kvh // 2) * LANE, (kvh // 2 + 1) * LANE)
        ks = both_halves(kall[:, blk], kvh % 2 == 0)
        if transposed:
            r0 = (kvh % 2) * HEAD_DIM
            v_t = vall[:, blk].T[r0:r0 + HEAD_DIM].astype(BF16)
        else:
            vs = both_halves(vall[:, blk], kvh % 2 == 0)
        for pair in range(2):
            c0 = kvh * 4 * HEAD_DIM + pair * LANE
            qp = q_ref[:, c0:c0 + LANE]
            if transposed:
                outs = []
                for kk in ks:
                    s = _dot_nt(kk, qp)
                    e = jnp.exp2(s - jnp.max(s, axis=0, keepdims=True))
                    l = jnp.sum(e, axis=0, keepdims=True)
                    outs.append(_dot(v_t, e.astype(BF16)) / l)
                o_ref[:, c0:c0 + LANE] = jnp.concatenate(outs, axis=0).T.astype(BF16)
            else:
                acc = jnp.zeros((qp.shape[0], LANE), F32)
                for kk, vv in zip(ks, vs):
                    s = _dot_nt(qp, kk)
                    e = jnp.exp2(s - jnp.max(s, axis=-1, keepdims=True))
                    l = jnp.sum(e, axis=-1, keepdims=True)
                    acc = acc + _dot(e.astype(BF16), vv) / l
                o_ref[:, c0:c0 + LANE] = acc.astype(BF16)


def _attention(q, k, v, cache_k, cache_v, layer, *, seq, name):
    t = q.shape[0]
    cached = cache_k is not None
    in_specs = [pl.BlockSpec((seq, D), lambda i: (i, 0)), pl.BlockSpec((seq, KV_W), lambda i: (i, 0)),
                pl.BlockSpec((seq, KV_W), lambda i: (i, 0))]
    args = [q, k, v]
    if cached:
        in_specs += [pl.BlockSpec((None, None, PAST, KV_W), lambda i: (i, layer, 0, 0))] * 2
        args += [cache_k, cache_v]
    return pl.pallas_call(
        functools.partial(_attn_kernel, cached=cached),
        grid=(t // seq,), in_specs=in_specs,
        out_specs=pl.BlockSpec((seq, D), lambda i: (i, 0)),
        out_shape=jax.ShapeDtypeStruct((t, D), BF16),
        compiler_params=_cparams(1), name=name,
    )(*args)


def _merge_kernel(a_ref, o_ref, c_ref, gate_ref, x_ref, mod_ref, n2g_ref, wa_ref, wb_ref, wc_ref, wo_ref,
                  x1_ref, h2_ref):
    mix = gate_ref[:, 0:D].astype(F32) * _dot(a_ref[...], wa_ref[...])
    mix = mix + gate_ref[:, D:2 * D].astype(F32) * _dot(o_ref[...], wb_ref[...])
    mix = mix + gate_ref[:, 2 * D:3 * D].astype(F32) * _dot(c_ref[...], wc_ref[...])
    out = _dot(mix.astype(BF16), wo_ref[...])
    mod = mod_ref[...]
    g1, sh2, sc2 = mod[:, 2 * D:3 * D], mod[:, 3 * D:4 * D], mod[:, 4 * D:5 * D]
    x1 = x_ref[...] + g1 * out
    x1_ref[...] = x1
    h2_ref[...] = ((_rms(x1) * n2g_ref[...]) * (1.0 + sc2) + sh2).astype(BF16)


def _merge(a, o, c, gates, x, mod_l, n2g, wa, wb, wc, wo, *, tm, row_of_tile, name):
    t = x.shape[0]
    row = lambda w: pl.BlockSpec((tm, w), lambda i: (i, 0))
    wspec = _resident((D, D))
    return pl.pallas_call(
        _merge_kernel,
        grid=(t // tm,),
        in_specs=[row(D), row(D), row(D), row(3 * D), row(D),
                  pl.BlockSpec((None, 1, 6 * D), lambda i: (row_of_tile(i), 0, 0)),
                  pl.BlockSpec((1, D), lambda i: (0, 0)), wspec, wspec, wspec, wspec],
        out_specs=[row(D), row(D)],
        out_shape=[jax.ShapeDtypeStruct((t, D), F32), jax.ShapeDtypeStruct((t, D), BF16)],
        compiler_params=_cparams(1), name=name,
    )(a, o, c, gates, x, mod_l, n2g, wa, wb, wc, wo)


FF_TN = 512
FF_ROWS = 1024


def _ffn_up_kernel(h_ref, wg_ref, wv_ref, cwg_ref, cwv_ref, cbg_ref, cbv_ref, out_ref, *, seq):
    h = h_ref[...]

    def conv(z, cw_ref, cb_ref):
        y = cb_ref[...] + _shift_prev(z, seq) * cw_ref[0:1, :]
        y = y + z * cw_ref[1:2, :]
        return y + _shift_next(z, seq) * cw_ref[2:3, :]

    g = conv(_dot(h, wg_ref[...]), cwg_ref, cbg_ref)
    v = conv(_dot(h, wv_ref[...]), cwv_ref, cbv_ref)
    out_ref[...] = ((g * jax.nn.sigmoid(g)) * v).astype(BF16)


def _ffn_up(h2, w_up, cw, cb, *, seq, name):
    t = h2.shape[0]
    nj = D_FF // FF_TN
    return pl.pallas_call(
        functools.partial(_ffn_up_kernel, seq=seq),
        grid=(t // FF_ROWS, nj),
        in_specs=[pl.BlockSpec((FF_ROWS, D), lambda i, j: (i, 0)),
                  pl.BlockSpec((D, FF_TN), lambda i, j: (0, j)),
                  pl.BlockSpec((D, FF_TN), lambda i, j: (0, nj + j)),
                  pl.BlockSpec((3, FF_TN), lambda i, j: (0, j)),
                  pl.BlockSpec((3, FF_TN), lambda i, j: (0, nj + j)),
                  pl.BlockSpec((1, FF_TN), lambda i, j: (0, j)),
                  pl.BlockSpec((1, FF_TN), lambda i, j: (0, nj + j))],
        out_specs=pl.BlockSpec((FF_ROWS, FF_TN), lambda i, j: (i, j)),
        out_shape=jax.ShapeDtypeStruct((t, D_FF), BF16),
        compiler_params=_cparams(2), name=name,
    )(h2, w_up, w_up, cw, cw, cb, cb)


def _ffn_down_kernel(act_ref, w_ref, x1_ref, mod_ref, fg_ref, out_ref, *, final):
    g2 = mod_ref[:, 5 * D:6 * D]
    x2 = x1_ref[...] + g2 * _dot(act_ref[...], w_ref[...])
    out_ref[...] = _rms(x2) * fg_ref[...] if final else x2


def _ffn_down(act, w_down, x1, mod_l, fg, *, tm, row_of_tile, final, name):
    t = x1.shape[0]
    return pl.pallas_call(
        functools.partial(_ffn_down_kernel, final=final),
        grid=(t // tm,),
        in_specs=[pl.BlockSpec((tm, D_FF), lambda i: (i, 0)),
                  _resident((D_FF, D)),
                  pl.BlockSpec((tm, D), lambda i: (i, 0)),
                  pl.BlockSpec((None, 1, 6 * D), lambda i: (row_of_tile(i), 0, 0)),
                  pl.BlockSpec((1, D), lambda i: (0, 0))],
        out_specs=pl.BlockSpec((tm, D), lambda i: (i, 0)),
        out_shape=jax.ShapeDtypeStruct((t, D), F32),
        compiler_params=_cparams(1), name=name,
    )(act, w_down, x1, mod_l, fg)


def _interleave(x):
    b, s, w = x.shape
    return x.reshape(b, SUB, s // SUB, w).transpose(0, 2, 1, 3).reshape(b * s, w)


def _deinterleave(y, b, s):
    return y.reshape(b, s // SUB, SUB, -1).transpose(0, 2, 1, 3).reshape(b, s, -1)


def _rope_tables():
    p = np.arange(S_LAT)
    t = (p % SUB) * (S_LAT // SUB) + p // SUB
    pos = np.stack([t // GRID_W, t % GRID_W], axis=1).astype(np.float32)
    quarter = HEAD_DIM // 4
    inv = (ROPE_BASE ** (-np.arange(quarter, dtype=np.float32) / quarter)).astype(np.float32)
    d = np.arange(LANE) % HEAD_DIM
    which = d // (HEAD_DIM // 2)
    e = d % (HEAD_DIM // 2)
    ang = pos[:, which] * inv[e % quarter][None, :]
    sign = np.where(e < quarter, -1.0, 1.0).astype(np.float32)
    return jnp.asarray(np.cos(ang).astype(np.float32)), jnp.asarray((np.sin(ang) * sign).astype(np.float32))


def _block_diag_gates(w_r, w_i):
    eye = jnp.eye(4, dtype=F32)

    def bd(w):
        w = w.reshape(2, 4, 4, LRU_BW, LRU_BW)
        w = w[:, :, :, :, None, :] * eye[None, None, :, None, :, None]
        return w.reshape(2, 4, GB, GB)

    return jnp.concatenate([bd(w_r), bd(w_i)], axis=-1).astype(BF16)


def _trunk(x_prompt, x_sample, cache_k, cache_v, state_lru, c, c_ctx, norm1_g, norm2_g, w_mod, b_mod, w_in,
           b_gate, conv_w, conv_b, w_rg, b_rg, w_ig, b_ig, lru_lambda, q_norm_g, k_norm_g, w_pool, pool_scale,
           w_br_a, w_br_b, w_br_c, w_o, w_up, ffn_conv_w, ffn_conv_b, w_down, final_norm_g,
           paths=("ctx", "lat")):
    cond = jnp.zeros((16, D), F32).at[0:N_LAT].set(c).at[N_LAT].set(c_ctx)
    mod = _modulation(cond, w_mod, b_mod)
    rope_tabs = _rope_tables()
    gmat = jnp.asarray(np.kron(np.eye(2), np.full((HEAD_DIM, HEAD_DIM), 1.0 / HEAD_DIM)), BF16)
    ck = cache_k.reshape(N_LAT, DEPTH, PAST, KV_W)
    cv = cache_v.reshape(N_LAT, DEPTH, PAST, KV_W)
    fg = final_norm_g.reshape(1, D)

    xs = {"ctx": _interleave(x_prompt), "lat": _interleave(x_sample)}
    cfg = {"ctx": dict(seq=S_CTX, tm=512, row=lambda i: N_LAT),
           "lat": dict(seq=S_LAT, tm=512, row=lambda i: i // (S_LAT // 512))}
    new_k, new_v, new_s = [], [], []
    for l in range(DEPTH):
        mod_l = mod[l].reshape(16, 1, 6 * D)
        w_in_l = w_in[l].astype(BF16)
        wg = _block_diag_gates(0.5 * w_rg[l], 0.5 * w_ig[l])
        bg = 0.5 * jnp.stack([b_rg[l], b_ig[l]], axis=1)
        wa, wb, wc, wo = (w[l].astype(BF16) for w in (w_br_a, w_br_b, w_br_c, w_o))
        w_up_l, w_down_l, wp = w_up[l].astype(BF16), w_down[l].astype(BF16), w_pool[l].astype(BF16)
        qg = jnp.tile(q_norm_g[l], 2).reshape(1, LANE)
        kg = jnp.tile(k_norm_g[l], 2).reshape(1, LANE)
        for path in paths:
            seq, tm, row = cfg[path]["seq"], cfg[path]["tm"], cfg[path]["row"]
            lat = path == "lat"
            x = xs[path]
            xr, gy, q, k, v, up, gates = _inproj(
                x, mod_l, norm1_g[l].reshape(1, D), w_in_l, b_gate[l].reshape(1, 3 * D), qg, kg, gmat,
                rope_tabs if lat else None, tm=tm, row_of_tile=row, name=f"inproj_{path}{l}")
            h0 = state_lru[:, l] if lat else jnp.zeros((N_CTX, 2, LRU_W), F32)
            a_pre, ends = _lru(xr, gy, h0, conv_w[l], conv_b[l].reshape(1, LRU_W), wg, bg, lru_lambda[l],
                               seq=seq, name=f"lru_{path}{l}")
            c_pre = _pool(up, wp, pool_scale[l].reshape(1, D), seq=seq, name=f"pool_{path}{l}")
            o = _attention(q, k, v, ck if lat else None, cv if lat else None, l, seq=seq,
                           name=f"attn_{path}{l}")
            x1, h2 = _merge(a_pre, o, c_pre, gates, x, mod_l, norm2_g[l].reshape(1, D), wa, wb, wc, wo,
                            tm=tm, row_of_tile=row, name=f"merge_{path}{l}")
            act = _ffn_up(h2, w_up_l, ffn_conv_w[l], ffn_conv_b[l].reshape(1, 2 * D_FF), seq=seq,
                          name=f"ffn_up_{path}{l}")
            xs[path] = _ffn_down(act, w_down_l, x1, mod_l, fg, tm=tm, row_of_tile=row,
                                 final=(l == DEPTH - 1), name=f"ffn_down_{path}{l}")
            if not lat:
                new_k.append(_deinterleave(k, N_CTX, S_CTX).reshape(N_CTX, S_CTX, N_KV, HEAD_DIM))
                new_v.append(_deinterleave(v, N_CTX, S_CTX).reshape(N_CTX, S_CTX, N_KV, HEAD_DIM))
                new_s.append(jnp.stack([ends[:, SUB - 1], ends[:, SUB]], axis=1))
    return xs, new_k, new_v, new_s


def kernel(x_prompt, x_sample, cache_k, cache_v, state_lru, c, c_ctx, norm1_g, norm2_g, w_mod, b_mod, w_in,
           b_gate, conv_w, conv_b, w_rg, b_rg, w_ig, b_ig, lru_lambda, q_norm_g, k_norm_g, w_pool, pool_scale,
           w_br_a, w_br_b, w_br_c, w_o, w_up, ffn_conv_w, ffn_conv_b, w_down, final_norm_g):
    xs, new_k, new_v, new_s = _trunk(
        x_prompt, x_sample, cache_k, cache_v, state_lru, c, c_ctx, norm1_g, norm2_g, w_mod, b_mod, w_in,
        b_gate, conv_w, conv_b, w_rg, b_rg, w_ig, b_ig, lru_lambda, q_norm_g, k_norm_g, w_pool, pool_scale,
        w_br_a, w_br_b, w_br_c, w_o, w_up, ffn_conv_w, ffn_conv_b, w_down, final_norm_g)
    y_prompt = _deinterleave(xs["ctx"], N_CTX, S_CTX)
    y_sample = _deinterleave(xs["lat"], N_LAT, S_LAT)
    return (y_prompt, y_sample, jnp.stack(new_k, axis=1), jnp.stack(new_v, axis=1), jnp.stack(new_s, axis=1))
```

```python
import functools

import numpy as np
import jax
import jax.numpy as jnp
from jax import lax
from jax.experimental import pallas as pl
from jax.experimental.pallas import tpu as pltpu

F32 = jnp.float32
BF16 = jnp.bfloat16

D = 1024
DEPTH = 2
N_CTX, S_CTX = 16, 256
N_LAT, S_LAT = 8, 1024
PAST = 256
GRID_W = 64
LRU_W = 1024
LRU_BW = 64
LRU_C = 8.0
N_HEADS, N_KV, HEAD_DIM = 16, 4, 64
KV_W = N_KV * HEAD_DIM
ROPE_BASE = 10000.0
D_FF = 3 * D
EPS = 1e-6
SUB = 8
LANE = 128
ROPE_SHIFT = HEAD_DIM // 4
C_XR, C_YR, C_Q, C_K, C_V, C_UP, C_GL, C_END = 0, 1024, 2048, 3072, 3328, 3584, 4608, 7680

LOG2E = float(np.log2(np.e))
Q_PRESCALE = LOG2E * HEAD_DIM ** -0.5

VMEM_LIMIT = 56 * 1024 * 1024


def _cparams(n_axes):
    return pltpu.CompilerParams(dimension_semantics=("arbitrary",) * n_axes,
                                vmem_limit_bytes=VMEM_LIMIT)


def _resident(shape, layer):
    return pl.BlockSpec((None,) + shape, lambda *_: (layer,) + (0,) * len(shape), pipeline_mode=pl.Buffered(1))


def _dot(a, b):
    return jnp.dot(a, b, preferred_element_type=F32)


def _dot_nt(a, b):
    return lax.dot_general(a, b, (((1,), (1,)), ((), ())), preferred_element_type=F32)


def _rms(x):
    return x * lax.rsqrt(jnp.mean(x * x, axis=-1, keepdims=True) + EPS)


def _shift_prev(x, rm, n=1):
    r, w = x.shape
    sub = lax.broadcasted_iota(jnp.int32, (SUB, w), 0)
    pieces = []
    for m in range(r // rm):
        base = m * rm
        for t in range(n):
            lo = base + rm - SUB * (n - t)
            pieces.append(jnp.where(sub == 0, 0.0, pltpu.roll(x[lo:lo + SUB], 1, 0)))
        pieces.append(x[base:base + rm - SUB * n])
    return jnp.concatenate(pieces, axis=0)


def _shift_next(x, rm, n=1):
    r, w = x.shape
    sub = lax.broadcasted_iota(jnp.int32, (SUB, w), 0)
    pieces = []
    for m in range(r // rm):
        base = m * rm
        pieces.append(x[base + SUB * n:base + rm])
        for t in range(n):
            lo = base + SUB * t
            pieces.append(jnp.where(sub == SUB - 1, 0.0, pltpu.roll(x[lo:lo + SUB], SUB - 1, 0)))
    return jnp.concatenate(pieces, axis=0)


def _mod_kernel(cond_ref, w_ref, b_ref, out_ref):
    c = cond_ref[...]
    s = (c * jax.nn.sigmoid(c)).astype(BF16)
    out_ref[...] = _dot(s, w_ref[...].astype(BF16)) + b_ref[...]


def _modulation(cond, w_mod, b_mod):
    tn = 1536
    return pl.pallas_call(
        _mod_kernel,
        grid=(DEPTH, 6 * D // tn),
        in_specs=[pl.BlockSpec((16, D), lambda l, j: (0, 0)),
                  pl.BlockSpec((None, D, tn), lambda l, j: (l, 0, j)),
                  pl.BlockSpec((None, 1, tn), lambda l, j: (l, 0, j))],
        out_specs=pl.BlockSpec((None, 16, tn), lambda l, j: (l, 0, j)),
        out_shape=jax.ShapeDtypeStruct((DEPTH, 16, 6 * D), F32),
        compiler_params=_cparams(2),
        name="modulation",
    )(cond, w_mod, b_mod.reshape(DEPTH, 1, 6 * D))


def _inproj_kernel(*refs, rope):
    if rope:
        (x_ref, mod_ref, n1g_ref, w_ref, bgate_ref, qg_ref, kg_ref, gmat_ref, cos_ref, sin_ref,
         xr_ref, gy_ref, q_ref, k_ref, v_ref, up_ref, gate_ref) = refs
    else:
        (x_ref, mod_ref, n1g_ref, w_ref, bgate_ref, qg_ref, kg_ref, gmat_ref,
         xr_ref, gy_ref, q_ref, k_ref, v_ref, up_ref, gate_ref) = refs
    mod = mod_ref[...]
    sh1, sc1 = mod[:, 0:D], mod[:, D:2 * D]
    h = ((_rms(x_ref[...]) * n1g_ref[...]) * (1.0 + sc1) + sh1).astype(BF16)

    def mm(c0, c1):
        return _dot(h, w_ref[:, c0:c1])

    xr_ref[...] = mm(C_XR, C_YR).astype(BF16)
    gy_ref[...] = jax.nn.gelu(mm(C_YR, C_Q)).astype(BF16)

    lane = lax.broadcasted_iota(jnp.int32, (1, LANE), 1)
    first = (lane & ROPE_SHIFT) == 0

    def head_norm(xb, g):
        ms = _dot((xb * xb).astype(BF16), gmat_ref[...])
        y = (xb * lax.rsqrt(ms + EPS)) * g
        if rope:
            partner = jnp.where(first, pltpu.roll(y, LANE - ROPE_SHIFT, 1), pltpu.roll(y, ROPE_SHIFT, 1))
            y = y * cos_ref[...] + partner * sin_ref[...]
        return y

    qa = mm(C_Q, C_K)
    for c in range(N_HEADS * HEAD_DIM // LANE):
        qn = head_norm(qa[:, c * LANE:(c + 1) * LANE], qg_ref[...])
        q_ref[:, c * LANE:(c + 1) * LANE] = (qn * Q_PRESCALE).astype(BF16)
    ka = mm(C_K, C_V)
    for c in range(KV_W // LANE):
        k_ref[:, c * LANE:(c + 1) * LANE] = head_norm(ka[:, c * LANE:(c + 1) * LANE], kg_ref[...])
    v_ref[...] = mm(C_V, C_UP)
    up_ref[...] = mm(C_UP, C_GL).astype(BF16)
    for g in range(3):
        z = mm(C_GL + g * D, C_GL + (g + 1) * D) + bgate_ref[:, g * D:(g + 1) * D]
        gate_ref[:, g * D:(g + 1) * D] = jax.nn.sigmoid(z).astype(BF16)


def _inproj(x, mod_l, n1g, w_in, b_gate, qg, kg, gmat, rope_tabs, *, layer, tm, row_of_tile, name):
    t = x.shape[0]
    rope = rope_tabs is not None
    full = lambda shape: pl.BlockSpec(shape, lambda i: (0,) * len(shape))
    in_specs = [pl.BlockSpec((tm, D), lambda i: (i, 0)),
                pl.BlockSpec((None, 1, 6 * D), lambda i: (row_of_tile(i), 0, 0)),
                full((1, D)), _resident((D, C_END), layer), full((1, 3 * D)), full((1, LANE)), full((1, LANE)),
                full((LANE, LANE))]
    args = [x, mod_l, n1g, w_in, b_gate, qg, kg, gmat]
    if rope:
        per_seq = S_LAT // tm
        in_specs += [pl.BlockSpec((tm, LANE), lambda i: (i % per_seq, 0))] * 2
        args += list(rope_tabs)
    row = lambda w: pl.BlockSpec((tm, w), lambda i: (i, 0))
    out_specs = [row(D), row(D), row(D), row(KV_W), row(KV_W), row(D), row(3 * D)]
    out_shape = [jax.ShapeDtypeStruct((t, D), BF16), jax.ShapeDtypeStruct((t, D), BF16),
                 jax.ShapeDtypeStruct((t, D), BF16), jax.ShapeDtypeStruct((t, KV_W), F32),
                 jax.ShapeDtypeStruct((t, KV_W), F32), jax.ShapeDtypeStruct((t, D), BF16),
                 jax.ShapeDtypeStruct((t, 3 * D), BF16)]
    return pl.pallas_call(
        functools.partial(_inproj_kernel, rope=rope),
        grid=(t // tm,), in_specs=in_specs, out_specs=out_specs, out_shape=out_shape,
        compiler_params=_cparams(1), name=name,
    )(*args)


GB = 256


def _lru_kernel(xr_ref, gy_ref, h0_ref, cw_ref, cb_ref, wg_ref, bg_ref, lam_ref,
                out_ref, ends_ref, a_s, u_s):
    r = xr_ref.shape[0]
    steps = r // SUB
    sub = lax.broadcasted_iota(jnp.int32, (SUB, GB), 0)
    for cb in range(LRU_W // GB):
        cols = slice(cb * GB, (cb + 1) * GB)
        x = xr_ref[:, cols].astype(F32)
        xn = _shift_next(x, r)
        xc = cb_ref[:, cols] + _shift_prev(x, r) * cw_ref[0:1, cols]
        xc = xc + x * cw_ref[1:2, cols]
        xc = xc + xn * cw_ref[2:3, cols]
        xc = xc + _shift_next(xn, r) * cw_ref[3:4, cols]
        lhs = xc.astype(BF16)
        xh = 0.5 * xc
        for d in range(2):
            g = _dot(lhs, wg_ref[d, cb])
            tr = jnp.tanh(g[:, :GB] + bg_ref[d, 0:1, cols])
            ti = jnp.tanh(g[:, GB:] + bg_ref[d, 1:2, cols])
            lam = lam_ref[d:d + 1, cols]
            log_sig = jnp.minimum(lam, 0.0) - jnp.log(1.0 + jnp.exp(-jnp.abs(lam)))
            ch = (0.5 * LRU_C) * log_sig
            a = jnp.exp2(tr * (ch * LOG2E) + ch * LOG2E)
            m2 = jnp.tanh(tr * (-ch) - ch) * (1.0 + a * a)
            mult = jnp.where(m2 > 0.0, m2 * lax.rsqrt(m2), 0.0)
            u = mult * (ti * xh + xh)
            a_s[d] = a
            u_s[d] = u
            e0 = 0 if d == 0 else r - SUB
            edge = sub == (0 if d == 0 else SUB - 1)
            ae = a_s[d, e0:e0 + SUB, :]
            u_s[d, e0:e0 + SUB, :] = u_s[d, e0:e0 + SUB, :] + jnp.where(edge, ae * h0_ref[d:d + 1, cols], 0.0)
            a_s[d, e0:e0 + SUB, :] = jnp.where(edge, 0.0, ae)

        def step(k, carry):
            hf, pf, hb, pb = carry
            rf = pl.multiple_of(k * SUB, SUB)
            rb = pl.multiple_of((steps - 1 - k) * SUB, SUB)
            af = a_s[0, pl.ds(rf, SUB), :]
            hf = af * hf + u_s[0, pl.ds(rf, SUB), :]
            pf = af * pf
            u_s[0, pl.ds(rf, SUB), :] = hf
            a_s[0, pl.ds(rf, SUB), :] = pf
            ab = a_s[1, pl.ds(rb, SUB), :]
            hb = ab * hb + u_s[1, pl.ds(rb, SUB), :]
            pb = ab * pb
            u_s[1, pl.ds(rb, SUB), :] = hb
            a_s[1, pl.ds(rb, SUB), :] = pb
            return hf, pf, hb, pb

        zero = jnp.zeros((SUB, GB), F32)
        one = jnp.ones((SUB, GB), F32)
        hf, pf, hb, pb = lax.fori_loop(0, steps, step, (zero, one, zero, one), unroll=8)

        ef, eb = hf, hb
        for sh in (1, 2, 4):
            keep_f = sub >= sh
            ef = ef + pf * jnp.where(keep_f, pltpu.roll(ef, sh, 0), 0.0)
            pf = pf * jnp.where(keep_f, pltpu.roll(pf, sh, 0), 1.0)
            keep_b = sub < SUB - sh
            eb = eb + pb * jnp.where(keep_b, pltpu.roll(eb, SUB - sh, 0), 0.0)
            pb = pb * jnp.where(keep_b, pltpu.roll(pb, SUB - sh, 0), 1.0)
        ends_ref[0:SUB, cols] = ef
        ends_ref[SUB:2 * SUB, cols] = eb
        cf = jnp.where(sub >= 1, pltpu.roll(ef, 1, 0), 0.0)
        cbk = jnp.where(sub < SUB - 1, pltpu.roll(eb, SUB - 1, 0), 0.0)
        cf2 = jnp.concatenate([cf, cf], axis=0)
        cb2 = jnp.concatenate([cbk, cbk], axis=0)

        def fix(m, _):
            rows = pl.ds(pl.multiple_of(m * 2 * SUB, 2 * SUB), 2 * SUB)
            hft = u_s[0, rows, :] + a_s[0, rows, :] * cf2
            hbt = u_s[1, rows, :] + a_s[1, rows, :] * cb2
            out_ref[rows, cols] = ((hft + hbt) * gy_ref[rows, cols].astype(F32)).astype(BF16)
            return 0

        lax.fori_loop(0, r // (2 * SUB), fix, 0, unroll=4)


def _lru(xr, gy, h0, conv_w, conv_b, wg, bg, lam, *, seq, name):
    t = xr.shape[0]
    nseq = t // seq
    full = lambda shape: pl.BlockSpec(shape, lambda i: (0,) * len(shape))
    return pl.pallas_call(
        _lru_kernel,
        grid=(nseq,),
        in_specs=[pl.BlockSpec((seq, D), lambda i: (i, 0)), pl.BlockSpec((seq, D), lambda i: (i, 0)),
                  pl.BlockSpec((None, 2, LRU_W), lambda i: (i, 0, 0)),
                  full((4, LRU_W)), full((1, LRU_W)), full((2, LRU_W // GB, GB, 2 * GB)),
                  full((2, 2, LRU_W)), full((2, LRU_W))],
        out_specs=[pl.BlockSpec((seq, D), lambda i: (i, 0)),
                   pl.BlockSpec((None, 2 * SUB, LRU_W), lambda i: (i, 0, 0))],
        out_shape=[jax.ShapeDtypeStruct((t, D), BF16), jax.ShapeDtypeStruct((nseq, 2 * SUB, LRU_W), F32)],
        scratch_shapes=[pltpu.VMEM((2, seq, GB), F32), pltpu.VMEM((2, seq, GB), F32)],
        compiler_params=_cparams(1), name=name,
    )(xr, gy, h0, conv_w, conv_b, wg, bg, lam)


POOL_G = 256


def _pool_kernel(up_ref, wp_ref, scale_ref, out_ref):
    r = up_ref.shape[0]
    steps = r // SUB
    row = lax.broadcasted_iota(jnp.int32, (r, 1), 0)
    t = (row & (SUB - 1)) * steps + jnp.right_shift(row, 3)
    for g in range(4):
        cols = slice(g * POOL_G, (g + 1) * POOL_G)
        x = up_ref[:, cols].astype(F32)
        half = 1 << g
        back, fwd = x, x
        for lvl in range(g):
            n = 1 << lvl
            back = back + _shift_prev(back, r, n)
            fwd = fwd + _shift_next(fwd, r, n)
        win = _shift_prev(back, r) + fwd
        cnt = jnp.minimum(t + half, r) - jnp.maximum(t - half, 0)
        d = (win / cnt.astype(F32) - x).astype(BF16)
        out_ref[:, cols] = (_dot(d, wp_ref[g]) * scale_ref[:, cols]).astype(BF16)


def _pool(up, w_pool, scale, *, layer, seq, name):
    t = up.shape[0]
    return pl.pallas_call(
        _pool_kernel,
        grid=(t // seq,),
        in_specs=[pl.BlockSpec((seq, D), lambda i: (i, 0)),
                  pl.BlockSpec((None, 4, POOL_G, POOL_G), lambda i: (layer, 0, 0, 0)),
                  pl.BlockSpec((1, D), lambda i: (0, 0))],
        out_specs=pl.BlockSpec((seq, D), lambda i: (i, 0)),
        out_shape=jax.ShapeDtypeStruct((t, D), BF16),
        compiler_params=_cparams(1), name=name,
    )(up, w_pool, scale)


def _attn_kernel(*refs, cached):
    transposed = cached
    if cached:
        q_ref, k_ref, v_ref, ck_ref, cv_ref, o_ref = refs
        kall = jnp.concatenate([ck_ref[...], k_ref[...]], axis=0)
        vall = jnp.concatenate([cv_ref[...], v_ref[...]], axis=0)
    else:
        q_ref, k_ref, v_ref, o_ref = refs
        kall = k_ref[...]
        vall = v_ref[...]
    half = HEAD_DIM
    lane = lax.broadcasted_iota(jnp.int32, (1, LANE), 1)

    def both_halves(x, own_low):
        own = jnp.where((lane < half) if own_low else (lane >= half), x, 0.0)
        oth = pltpu.roll(own, half, 1)
        lo, hi = (own, oth) if own_low else (oth, own)
        return lo.astype(BF16), hi.astype(BF16)

    if transposed:
        heads = []
        for kvh in range(N_KV):
            blk = slice((kvh // 2) * LANE, (kvh // 2 + 1) * LANE)
            ks = both_halves(kall[:, blk], kvh % 2 == 0)
            r0 = (kvh % 2) * HEAD_DIM
            v_t = vall[:, blk].T[r0:r0 + HEAD_DIM].astype(BF16)
            for pair in range(2):
                heads += [(kk, v_t, kvh * 4 * HEAD_DIM + pair * LANE) for kk in ks]

        def scores(h):
            kk, _, c0 = heads[h]
            return _dot_nt(kk, q_ref[:, c0:c0 + LANE])

        s_next = scores(0)
        outs = []
        for h, (_, v_t, c0) in enumerate(heads):
            s = s_next
            if h + 1 < len(heads):
                s_next = scores(h + 1)
            e = jnp.exp2(s - jnp.max(s, axis=0, keepdims=True))
            l = jnp.sum(e, axis=0, keepdims=True)
            outs.append(_dot(v_t, e.astype(BF16)) / l)
            if h % 2 == 1:
                o_ref[:, c0:c0 + LANE] = jnp.concatenate(outs, axis=0).T.astype(BF16)
                outs = []
        return

    for kvh in range(N_KV):
        blk = slice((kvh // 2) * LANE, (kvh // 2 + 1) * LANE)
        ks = both_halves(kall[:, blk], kvh % 2 == 0)
        vs = both_halves(vall[:, blk], kvh % 2 == 0)
        for pair in range(2):
            c0 = kvh * 4 * HEAD_DIM + pair * LANE
            qp = q_ref[:, c0:c0 + LANE]
            acc = jnp.zeros((qp.shape[0], LANE), F32)
            for kk, vv in zip(ks, vs):
                s = _dot_nt(qp, kk)
                e = jnp.exp2(s - jnp.max(s, axis=-1, keepdims=True))
                l = jnp.sum(e, axis=-1, keepdims=True)
                acc = acc + _dot(e.astype(BF16), vv) / l
            o_ref[:, c0:c0 + LANE] = acc.astype(BF16)


def _attention(q, k, v, cache_k, cache_v, layer, *, seq, name):
    t = q.shape[0]
    cached = cache_k is not None
    in_specs = [pl.BlockSpec((seq, D), lambda i: (i, 0)), pl.BlockSpec((seq, KV_W), lambda i: (i, 0)),
                pl.BlockSpec((seq, KV_W), lambda i: (i, 0))]
    args = [q, k, v]
    if cached:
        in_specs += [pl.BlockSpec((None, None, PAST, KV_W), lambda i: (i, layer, 0, 0))] * 2
        args += [cache_k, cache_v]
    return pl.pallas_call(
        functools.partial(_attn_kernel, cached=cached),
        grid=(t // seq,), in_specs=in_specs,
        out_specs=pl.BlockSpec((seq, D), lambda i: (i, 0)),
        out_shape=jax.ShapeDtypeStruct((t, D), BF16),
        compiler_params=_cparams(1), name=name,
    )(*args)


def _merge_kernel(a_ref, o_ref, c_ref, gate_ref, x_ref, mod_ref, n2g_ref, wa_ref, wb_ref, wc_ref, wo_ref,
                  x1_ref, h2_ref):
    mix = gate_ref[:, 0:D].astype(F32) * _dot(a_ref[...], wa_ref[...])
    mix = mix + gate_ref[:, D:2 * D].astype(F32) * _dot(o_ref[...], wb_ref[...])
    mix = mix + gate_ref[:, 2 * D:3 * D].astype(F32) * _dot(c_ref[...], wc_ref[...])
    out = _dot(mix.astype(BF16), wo_ref[...])
    mod = mod_ref[...]
    g1, sh2, sc2 = mod[:, 2 * D:3 * D], mod[:, 3 * D:4 * D], mod[:, 4 * D:5 * D]
    x1 = x_ref[...] + g1 * out
    x1_ref[...] = x1
    h2_ref[...] = ((_rms(x1) * n2g_ref[...]) * (1.0 + sc2) + sh2).astype(BF16)


def _merge(a, o, c, gates, x, mod_l, n2g, wa, wb, wc, wo, *, layer, tm, row_of_tile, name):
    t = x.shape[0]
    row = lambda w: pl.BlockSpec((tm, w), lambda i: (i, 0))
    wspec = _resident((D, D), layer)
    return pl.pallas_call(
        _merge_kernel,
        grid=(t // tm,),
        in_specs=[row(D), row(D), row(D), row(3 * D), row(D),
                  pl.BlockSpec((None, 1, 6 * D), lambda i: (row_of_tile(i), 0, 0)),
                  pl.BlockSpec((1, D), lambda i: (0, 0)), wspec, wspec, wspec, wspec],
        out_specs=[row(D), row(D)],
        out_shape=[jax.ShapeDtypeStruct((t, D), F32), jax.ShapeDtypeStruct((t, D), BF16)],
        compiler_params=_cparams(1), name=name,
    )(a, o, c, gates, x, mod_l, n2g, wa, wb, wc, wo)


FF_TN = 512
FF_ROWS = 1024


def _ffn_up_kernel(h_ref, wg_ref, wv_ref, cwg_ref, cwv_ref, cbg_ref, cbv_ref, out_ref, *, seq):
    h = h_ref[...]

    def conv(z, cw_ref, cb_ref):
        y = cb_ref[...] + _shift_prev(z, seq) * cw_ref[0:1, :]
        y = y + z * cw_ref[1:2, :]
        return y + _shift_next(z, seq) * cw_ref[2:3, :]

    g = conv(_dot(h, wg_ref[...]), cwg_ref, cbg_ref)
    v = conv(_dot(h, wv_ref[...]), cwv_ref, cbv_ref)
    out_ref[...] = ((g * jax.nn.sigmoid(g)) * v).astype(BF16)


def _ffn_up(h2, w_up, cw, cb, *, layer, seq, name):
    t = h2.shape[0]
    nj = D_FF // FF_TN
    return pl.pallas_call(
        functools.partial(_ffn_up_kernel, seq=seq),
        grid=(t // FF_ROWS, nj),
        in_specs=[pl.BlockSpec((FF_ROWS, D), lambda i, j: (i, 0)),
                  pl.BlockSpec((None, D, FF_TN), lambda i, j: (layer, 0, j)),
                  pl.BlockSpec((None, D, FF_TN), lambda i, j: (layer, 0, nj + j)),
                  pl.BlockSpec((3, FF_TN), lambda i, j: (0, j)),
                  pl.BlockSpec((3, FF_TN), lambda i, j: (0, nj + j)),
                  pl.BlockSpec((1, FF_TN), lambda i, j: (0, j)),
                  pl.BlockSpec((1, FF_TN), lambda i, j: (0, nj + j))],
        out_specs=pl.BlockSpec((FF_ROWS, FF_TN), lambda i, j: (i, j)),
        out_shape=jax.ShapeDtypeStruct((t, D_FF), BF16),
        compiler_params=_cparams(2), name=name,
    )(h2, w_up, w_up, cw, cw, cb, cb)


def _ffn_down_kernel(act_ref, w_ref, x1_ref, mod_ref, fg_ref, out_ref, *, final):
    g2 = mod_ref[:, 5 * D:6 * D]
    x2 = x1_ref[...] + g2 * _dot(act_ref[...], w_ref[...])
    out_ref[...] = _rms(x2) * fg_ref[...] if final else x2


def _ffn_down(act, w_down, x1, mod_l, fg, *, layer, tm, row_of_tile, final, name):
    t = x1.shape[0]
    return pl.pallas_call(
        functools.partial(_ffn_down_kernel, final=final),
        grid=(t // tm,),
        in_specs=[pl.BlockSpec((tm, D_FF), lambda i: (i, 0)),
                  _resident((D_FF, D), layer),
                  pl.BlockSpec((tm, D), lambda i: (i, 0)),
                  pl.BlockSpec((None, 1, 6 * D), lambda i: (row_of_tile(i), 0, 0)),
                  pl.BlockSpec((1, D), lambda i: (0, 0))],
        out_specs=pl.BlockSpec((tm, D), lambda i: (i, 0)),
        out_shape=jax.ShapeDtypeStruct((t, D), F32),
        compiler_params=_cparams(1), name=name,
    )(act, w_down, x1, mod_l, fg)


def _interleave(x):
    b, s, w = x.shape
    return x.reshape(b, SUB, s // SUB, w).transpose(0, 2, 1, 3).reshape(b * s, w)


def _deinterleave(y, b, s):
    return y.reshape(b, s // SUB, SUB, -1).transpose(0, 2, 1, 3).reshape(b, s, -1)


def _rope_tables():
    p = np.arange(S_LAT)
    t = (p % SUB) * (S_LAT // SUB) + p // SUB
    pos = np.stack([t // GRID_W, t % GRID_W], axis=1).astype(np.float32)
    quarter = HEAD_DIM // 4
    inv = (ROPE_BASE ** (-np.arange(quarter, dtype=np.float32) / quarter)).astype(np.float32)
    d = np.arange(LANE) % HEAD_DIM
    which = d // (HEAD_DIM // 2)
    e = d % (HEAD_DIM // 2)
    ang = pos[:, which] * inv[e % quarter][None, :]
    sign = np.where(e < quarter, -1.0, 1.0).astype(np.float32)
    return jnp.asarray(np.cos(ang).astype(np.float32)), jnp.asarray((np.sin(ang) * sign).astype(np.float32))


def _block_diag_gates(w_r, w_i):
    eye = jnp.eye(4, dtype=F32)

    def bd(w):
        w = w.reshape(2, 4, 4, LRU_BW, LRU_BW)
        w = w[:, :, :, :, None, :] * eye[None, None, :, None, :, None]
        return w.reshape(2, 4, GB, GB)

    return jnp.concatenate([bd(w_r), bd(w_i)], axis=-1).astype(BF16)


def _trunk(x_prompt, x_sample, cache_k, cache_v, state_lru, c, c_ctx, norm1_g, norm2_g, w_mod, b_mod, w_in,
           b_gate, conv_w, conv_b, w_rg, b_rg, w_ig, b_ig, lru_lambda, q_norm_g, k_norm_g, w_pool, pool_scale,
           w_br_a, w_br_b, w_br_c, w_o, w_up, ffn_conv_w, ffn_conv_b, w_down, final_norm_g,
           paths=("ctx", "lat")):
    cond = jnp.zeros((16, D), F32).at[0:N_LAT].set(c).at[N_LAT].set(c_ctx)
    mod = _modulation(cond, w_mod, b_mod)
    rope_tabs = _rope_tables()
    gmat = jnp.asarray(np.kron(np.eye(2), np.full((HEAD_DIM, HEAD_DIM), 1.0 / HEAD_DIM)), BF16)
    ck = cache_k.reshape(N_LAT, DEPTH, PAST, KV_W)
    cv = cache_v.reshape(N_LAT, DEPTH, PAST, KV_W)
    fg = final_norm_g.reshape(1, D)

    xs = {"ctx": _interleave(x_prompt), "lat": _interleave(x_sample)}
    cfg = {"ctx": dict(seq=S_CTX, tm=512, row=lambda i: N_LAT),
           "lat": dict(seq=S_LAT, tm=512, row=lambda i: i // (S_LAT // 512))}
    new_k, new_v, new_s = [], [], []
    w_in_b, wa, wb, wc, wo, w_up_b, w_down_b, wp = (
        w.astype(BF16) for w in (w_in, w_br_a, w_br_b, w_br_c, w_o, w_up, w_down, w_pool))
    for l in range(DEPTH):
        mod_l = mod[l].reshape(16, 1, 6 * D)
        wg = _block_diag_gates(0.5 * w_rg[l], 0.5 * w_ig[l])
        bg = 0.5 * jnp.stack([b_rg[l], b_ig[l]], axis=1)
        qg = jnp.tile(q_norm_g[l], 2).reshape(1, LANE)
        kg = jnp.tile(k_norm_g[l], 2).reshape(1, LANE)
        for path in paths:
            seq, tm, row = cfg[path]["seq"], cfg[path]["tm"], cfg[path]["row"]
            lat = path == "lat"
            x = xs[path]
            xr, gy, q, k, v, up, gates = _inproj(
                x, mod_l, norm1_g[l].reshape(1, D), w_in_b, b_gate[l].reshape(1, 3 * D), qg, kg, gmat,
                rope_tabs if lat else None, layer=l, tm=tm, row_of_tile=row, name=f"inproj_{path}{l}")
            h0 = state_lru[:, l] if lat else jnp.zeros((N_CTX, 2, LRU_W), F32)
            a_pre, ends = _lru(xr, gy, h0, conv_w[l], conv_b[l].reshape(1, LRU_W), wg, bg, lru_lambda[l],
                               seq=seq, name=f"lru_{path}{l}")
            c_pre = _pool(up, wp, pool_scale[l].reshape(1, D), layer=l, seq=seq, name=f"pool_{path}{l}")
            o = _attention(q, k, v, ck if lat else None, cv if lat else None, l, seq=seq,
                           name=f"attn_{path}{l}")
            x1, h2 = _merge(a_pre, o, c_pre, gates, x, mod_l, norm2_g[l].reshape(1, D), wa, wb, wc, wo,
                            layer=l, tm=tm, row_of_tile=row, name=f"merge_{path}{l}")
            act = _ffn_up(h2, w_up_b, ffn_conv_w[l], ffn_conv_b[l].reshape(1, 2 * D_FF), layer=l, seq=seq,
                          name=f"ffn_up_{path}{l}")
            xs[path] = _ffn_down(act, w_down_b, x1, mod_l, fg, layer=l, tm=tm, row_of_tile=row,
                                 final=(l == DEPTH - 1), name=f"ffn_down_{path}{l}")
            if not lat:
                new_k.append(_deinterleave(k, N_CTX, S_CTX).reshape(N_CTX, S_CTX, N_KV, HEAD_DIM))
                new_v.append(_deinterleave(v, N_CTX, S_CTX).reshape(N_CTX, S_CTX, N_KV, HEAD_DIM))
                new_s.append(jnp.stack([ends[:, SUB - 1], ends[:, SUB]], axis=1))
    return xs, new_k, new_v, new_s


def kernel(x_prompt, x_sample, cache_k, cache_v, state_lru, c, c_ctx, norm1_g, norm2_g, w_mod, b_mod, w_in,
           b_gate, conv_w, conv_b, w_rg, b_rg, w_ig, b_ig, lru_lambda, q_norm_g, k_norm_g, w_pool, pool_scale,
           w_br_a, w_br_b, w_br_c, w_o, w_up, ffn_conv_w, ffn_conv_b, w_down, final_norm_g):
    xs, new_k, new_v, new_s = _trunk(
        x_prompt, x_sample, cache_k, cache_v, state_lru, c, c_ctx, norm1_g, norm2_g, w_mod, b_mod, w_in,
        b_gate, conv_w, conv_b, w_rg, b_rg, w_ig, b_ig, lru_lambda, q_norm_g, k_norm_g, w_pool, pool_scale,
        w_br_a, w_br_b, w_br_c, w_o, w_up, ffn_conv_w, ffn_conv_b, w_down, final_norm_g)
    y_prompt = _deinterleave(xs["ctx"], N_CTX, S_CTX)
    y_sample = _deinterleave(xs["lat"], N_LAT, S_LAT)
    return (y_prompt, y_sample, jnp.stack(new_k, axis=1), jnp.stack(new_v, axis=1), jnp.stack(new_s, axis=1))
```

```python
import functools

import numpy as np
import jax
import jax.numpy as jnp
from jax import lax
from jax.experimental import pallas as pl
from jax.experimental.pallas import tpu as pltpu

F32 = jnp.float32
BF16 = jnp.bfloat16

D = 1024
DEPTH = 2
N_CTX, S_CTX = 16, 256
N_LAT, S_LAT = 8, 1024
PAST = 256
GRID_W = 64
LRU_W = 1024
LRU_BW = 64
LRU_C = 8.0
N_HEADS, N_KV, HEAD_DIM = 16, 4, 64
KV_W = N_KV * HEAD_DIM
ROPE_BASE = 10000.0
D_FF = 3 * D
EPS = 1e-6
SUB = 8
LANE = 128
ROPE_SHIFT = HEAD_DIM // 4
C_XR, C_YR, C_Q, C_K, C_V, C_UP, C_GL, C_END = 0, 1024, 2048, 3072, 3328, 3584, 4608, 7680

LOG2E = float(np.log2(np.e))
Q_PRESCALE = LOG2E * HEAD_DIM ** -0.5

VMEM_LIMIT = 56 * 1024 * 1024


def _cparams(n_axes):
    return pltpu.CompilerParams(dimension_semantics=("arbitrary",) * n_axes,
                                vmem_limit_bytes=VMEM_LIMIT)


def _resident(shape, layer):
    return pl.BlockSpec((None,) + shape, lambda *_: (layer,) + (0,) * len(shape), pipeline_mode=pl.Buffered(1))


def _dot(a, b):
    return jnp.dot(a, b, preferred_element_type=F32)


def _dot_nt(a, b):
    return lax.dot_general(a, b, (((1,), (1,)), ((), ())), preferred_element_type=F32)


def _rms(x):
    return x * lax.rsqrt(jnp.mean(x * x, axis=-1, keepdims=True) + EPS)


def _shift_prev(x, rm, n=1):
    r, w = x.shape
    sub = lax.broadcasted_iota(jnp.int32, (SUB, w), 0)
    pieces = []
    for m in range(r // rm):
        base = m * rm
        for t in range(n):
            lo = base + rm - SUB * (n - t)
            pieces.append(jnp.where(sub == 0, 0.0, pltpu.roll(x[lo:lo + SUB], 1, 0)))
        pieces.append(x[base:base + rm - SUB * n])
    return jnp.concatenate(pieces, axis=0)


def _shift_next(x, rm, n=1):
    r, w = x.shape
    sub = lax.broadcasted_iota(jnp.int32, (SUB, w), 0)
    pieces = []
    for m in range(r // rm):
        base = m * rm
        pieces.append(x[base + SUB * n:base + rm])
        for t in range(n):
            lo = base + SUB * t
            pieces.append(jnp.where(sub == SUB - 1, 0.0, pltpu.roll(x[lo:lo + SUB], SUB - 1, 0)))
    return jnp.concatenate(pieces, axis=0)


def _mod_kernel(cond_ref, w_ref, b_ref, out_ref):
    c = cond_ref[...]
    s = (c * jax.nn.sigmoid(c)).astype(BF16)
    out_ref[...] = _dot(s, w_ref[...].astype(BF16)) + b_ref[...]


def _modulation(cond, w_mod, b_mod):
    tn = 1536
    return pl.pallas_call(
        _mod_kernel,
        grid=(DEPTH, 6 * D // tn),
        in_specs=[pl.BlockSpec((16, D), lambda l, j: (0, 0)),
                  pl.BlockSpec((None, D, tn), lambda l, j: (l, 0, j)),
                  pl.BlockSpec((None, 1, tn), lambda l, j: (l, 0, j))],
        out_specs=pl.BlockSpec((None, 16, tn), lambda l, j: (l, 0, j)),
        out_shape=jax.ShapeDtypeStruct((DEPTH, 16, 6 * D), F32),
        compiler_params=_cparams(2),
        name="modulation",
    )(cond, w_mod, b_mod.reshape(DEPTH, 1, 6 * D))


def _inproj_kernel(*refs, rope):
    if rope:
        (x_ref, mod_ref, n1g_ref, w_ref, bgate_ref, qg_ref, kg_ref, gmat_ref, cos_ref, sin_ref,
         xr_ref, gy_ref, q_ref, k_ref, v_ref, up_ref, gate_ref) = refs
    else:
        (x_ref, mod_ref, n1g_ref, w_ref, bgate_ref, qg_ref, kg_ref, gmat_ref,
         xr_ref, gy_ref, q_ref, k_ref, v_ref, up_ref, gate_ref) = refs
    mod = mod_ref[...]
    sh1, sc1 = mod[:, 0:D], mod[:, D:2 * D]
    h = ((_rms(x_ref[...]) * n1g_ref[...]) * (1.0 + sc1) + sh1).astype(BF16)

    def mm(c0, c1):
        return _dot(h, w_ref[:, c0:c1])

    xr_ref[...] = mm(C_XR, C_YR).astype(BF16)
    gy_ref[...] = jax.nn.gelu(mm(C_YR, C_Q)).astype(BF16)

    lane = lax.broadcasted_iota(jnp.int32, (1, LANE), 1)
    first = (lane & ROPE_SHIFT) == 0

    def head_norm(xb, g):
        ms = _dot((xb * xb).astype(BF16), gmat_ref[...])
        y = (xb * lax.rsqrt(ms + EPS)) * g
        if rope:
            partner = jnp.where(first, pltpu.roll(y, LANE - ROPE_SHIFT, 1), pltpu.roll(y, ROPE_SHIFT, 1))
            y = y * cos_ref[...] + partner * sin_ref[...]
        return y

    qa = mm(C_Q, C_K)
    for c in range(N_HEADS * HEAD_DIM // LANE):
        qn = head_norm(qa[:, c * LANE:(c + 1) * LANE], qg_ref[...])
        q_ref[:, c * LANE:(c + 1) * LANE] = (qn * Q_PRESCALE).astype(BF16)
    ka = mm(C_K, C_V)
    for c in range(KV_W // LANE):
        k_ref[:, c * LANE:(c + 1) * LANE] = head_norm(ka[:, c * LANE:(c + 1) * LANE], kg_ref[...])
    v_ref[...] = mm(C_V, C_UP)
    up_ref[...] = mm(C_UP, C_GL).astype(BF16)
    for g in range(3):
        z = mm(C_GL + g * D, C_GL + (g + 1) * D) + bgate_ref[:, g * D:(g + 1) * D]
        gate_ref[:, g * D:(g + 1) * D] = jax.nn.sigmoid(z).astype(BF16)


def _inproj(x, mod_l, n1g, w_in, b_gate, qg, kg, gmat, rope_tabs, *, layer, tm, row_of_tile, name):
    t = x.shape[0]
    rope = rope_tabs is not None
    full = lambda shape: pl.BlockSpec(shape, lambda i: (0,) * len(shape))
    in_specs = [pl.BlockSpec((tm, D), lambda i: (i, 0)),
                pl.BlockSpec((None, 1, 6 * D), lambda i: (row_of_tile(i), 0, 0)),
                full((1, D)), _resident((D, C_END), layer), full((1, 3 * D)), full((1, LANE)), full((1, LANE)),
                full((LANE, LANE))]
    args = [x, mod_l, n1g, w_in, b_gate, qg, kg, gmat]
    if rope:
        per_seq = S_LAT // tm
        in_specs += [pl.BlockSpec((tm, LANE), lambda i: (i % per_seq, 0))] * 2
        args += list(rope_tabs)
    row = lambda w: pl.BlockSpec((tm, w), lambda i: (i, 0))
    out_specs = [row(D), row(D), row(D), row(KV_W), row(KV_W), row(D), row(3 * D)]
    out_shape = [jax.ShapeDtypeStruct((t, D), BF16), jax.ShapeDtypeStruct((t, D), BF16),
                 jax.ShapeDtypeStruct((t, D), BF16), jax.ShapeDtypeStruct((t, KV_W), F32),
                 jax.ShapeDtypeStruct((t, KV_W), F32), jax.ShapeDtypeStruct((t, D), BF16),
                 jax.ShapeDtypeStruct((t, 3 * D), BF16)]
    return pl.pallas_call(
        functools.partial(_inproj_kernel, rope=rope),
        grid=(t // tm,), in_specs=in_specs, out_specs=out_specs, out_shape=out_shape,
        compiler_params=_cparams(1), name=name,
    )(*args)


GB = 256


def _lru_kernel(xr_ref, gy_ref, h0_ref, cw_ref, cb_ref, wg_ref, bg_ref, lam_ref,
                out_ref, ends_ref, a_s, u_s):
    r = xr_ref.shape[0]
    steps = r // SUB
    sub = lax.broadcasted_iota(jnp.int32, (SUB, GB), 0)
    for cb in range(LRU_W // GB):
        cols = slice(cb * GB, (cb + 1) * GB)
        x = xr_ref[:, cols].astype(F32)
        xn = _shift_next(x, r)
        xc = cb_ref[:, cols] + _shift_prev(x, r) * cw_ref[0:1, cols]
        xc = xc + x * cw_ref[1:2, cols]
        xc = xc + xn * cw_ref[2:3, cols]
        xc = xc + _shift_next(xn, r) * cw_ref[3:4, cols]
        lhs = xc.astype(BF16)
        xh = 0.5 * xc
        for d in range(2):
            g = _dot(lhs, wg_ref[d, cb])
            tr = jnp.tanh(g[:, :GB] + bg_ref[d, 0:1, cols])
            ti = jnp.tanh(g[:, GB:] + bg_ref[d, 1:2, cols])
            lam = lam_ref[d:d + 1, cols]
            log_sig = jnp.minimum(lam, 0.0) - jnp.log(1.0 + jnp.exp(-jnp.abs(lam)))
            ch = (0.5 * LRU_C) * log_sig
            a = jnp.exp2(tr * (ch * LOG2E) + ch * LOG2E)
            m2 = jnp.tanh(tr * (-ch) - ch) * (1.0 + a * a)
            mult = jnp.where(m2 > 0.0, m2 * lax.rsqrt(m2), 0.0)
            u = mult * (ti * xh + xh)
            a_s[d] = a
            u_s[d] = u
            e0 = 0 if d == 0 else r - SUB
            edge = sub == (0 if d == 0 else SUB - 1)
            ae = a_s[d, e0:e0 + SUB, :]
            u_s[d, e0:e0 + SUB, :] = u_s[d, e0:e0 + SUB, :] + jnp.where(edge, ae * h0_ref[d:d + 1, cols], 0.0)
            a_s[d, e0:e0 + SUB, :] = jnp.where(edge, 0.0, ae)

        def step(k, carry):
            hf, pf, hb, pb = carry
            rf = pl.multiple_of(k * SUB, SUB)
            rb = pl.multiple_of((steps - 1 - k) * SUB, SUB)
            af = a_s[0, pl.ds(rf, SUB), :]
            hf = af * hf + u_s[0, pl.ds(rf, SUB), :]
            pf = af * pf
            u_s[0, pl.ds(rf, SUB), :] = hf
            a_s[0, pl.ds(rf, SUB), :] = pf
            ab = a_s[1, pl.ds(rb, SUB), :]
            hb = ab * hb + u_s[1, pl.ds(rb, SUB), :]
            pb = ab * pb
            u_s[1, pl.ds(rb, SUB), :] = hb
            a_s[1, pl.ds(rb, SUB), :] = pb
            return hf, pf, hb, pb

        zero = jnp.zeros((SUB, GB), F32)
        one = jnp.ones((SUB, GB), F32)
        hf, pf, hb, pb = lax.fori_loop(0, steps, step, (zero, one, zero, one), unroll=8)

        ef, eb = hf, hb
        for sh in (1, 2, 4):
            keep_f = sub >= sh
            ef = ef + pf * jnp.where(keep_f, pltpu.roll(ef, sh, 0), 0.0)
            pf = pf * jnp.where(keep_f, pltpu.roll(pf, sh, 0), 1.0)
            keep_b = sub < SUB - sh
            eb = eb + pb * jnp.where(keep_b, pltpu.roll(eb, SUB - sh, 0), 0.0)
            pb = pb * jnp.where(keep_b, pltpu.roll(pb, SUB - sh, 0), 1.0)
        ends_ref[0:SUB, cols] = ef
        ends_ref[SUB:2 * SUB, cols] = eb
        cf = jnp.where(sub >= 1, pltpu.roll(ef, 1, 0), 0.0)
        cbk = jnp.where(sub < SUB - 1, pltpu.roll(eb, SUB - 1, 0), 0.0)
        cf2 = jnp.concatenate([cf, cf], axis=0)
        cb2 = jnp.concatenate([cbk, cbk], axis=0)

        def fix(m, _):
            rows = pl.ds(pl.multiple_of(m * 2 * SUB, 2 * SUB), 2 * SUB)
            hft = u_s[0, rows, :] + a_s[0, rows, :] * cf2
            hbt = u_s[1, rows, :] + a_s[1, rows, :] * cb2
            out_ref[rows, cols] = ((hft + hbt) * gy_ref[rows, cols].astype(F32)).astype(BF16)
            return 0

        lax.fori_loop(0, r // (2 * SUB), fix, 0, unroll=4)


def _lru(xr, gy, h0, conv_w, conv_b, wg, bg, lam, *, seq, name):
    t = xr.shape[0]
    nseq = t // seq
    full = lambda shape: pl.BlockSpec(shape, lambda i: (0,) * len(shape))
    return pl.pallas_call(
        _lru_kernel,
        grid=(nseq,),
        in_specs=[pl.BlockSpec((seq, D), lambda i: (i, 0)), pl.BlockSpec((seq, D), lambda i: (i, 0)),
                  pl.BlockSpec((None, 2, LRU_W), lambda i: (i, 0, 0)),
                  full((4, LRU_W)), full((1, LRU_W)), full((2, LRU_W // GB, GB, 2 * GB)),
                  full((2, 2, LRU_W)), full((2, LRU_W))],
        out_specs=[pl.BlockSpec((seq, D), lambda i: (i, 0)),
                   pl.BlockSpec((None, 2 * SUB, LRU_W), lambda i: (i, 0, 0))],
        out_shape=[jax.ShapeDtypeStruct((t, D), BF16), jax.ShapeDtypeStruct((nseq, 2 * SUB, LRU_W), F32)],
        scratch_shapes=[pltpu.VMEM((2, seq, GB), F32), pltpu.VMEM((2, seq, GB), F32)],
        compiler_params=_cparams(1), name=name,
    )(xr, gy, h0, conv_w, conv_b, wg, bg, lam)


POOL_G = 256


def _pool_kernel(up_ref, wp_ref, scale_ref, out_ref):
    r = up_ref.shape[0]
    steps = r // SUB
    row = lax.broadcasted_iota(jnp.int32, (r, 1), 0)
    t = (row & (SUB - 1)) * steps + jnp.right_shift(row, 3)
    for g in range(4):
        cols = slice(g * POOL_G, (g + 1) * POOL_G)
        x = up_ref[:, cols].astype(F32)
        half = 1 << g
        back, fwd = x, x
        for lvl in range(g):
            n = 1 << lvl
            back = back + _shift_prev(back, r, n)
            fwd = fwd + _shift_next(fwd, r, n)
        win = _shift_prev(back, r) + fwd
        cnt = jnp.minimum(t + half, r) - jnp.maximum(t - half, 0)
        d = (win / cnt.astype(F32) - x).astype(BF16)
        out_ref[:, cols] = (_dot(d, wp_ref[g]) * scale_ref[:, cols]).astype(BF16)


def _pool(up, w_pool, scale, *, layer, seq, name):
    t = up.shape[0]
    return pl.pallas_call(
        _pool_kernel,
        grid=(t // seq,),
        in_specs=[pl.BlockSpec((seq, D), lambda i: (i, 0)),
                  pl.BlockSpec((None, 4, POOL_G, POOL_G), lambda i: (layer, 0, 0, 0)),
                  pl.BlockSpec((1, D), lambda i: (0, 0))],
        out_specs=pl.BlockSpec((seq, D), lambda i: (i, 0)),
        out_shape=jax.ShapeDtypeStruct((t, D), BF16),
        compiler_params=_cparams(1), name=name,
    )(up, w_pool, scale)


ONES_ROWS = 16


def _attn_kernel(*refs, cached):
    transposed = cached
    if cached:
        q_ref, k_ref, v_ref, ck_ref, cv_ref, o_ref = refs
        kall = jnp.concatenate([ck_ref[...], k_ref[...]], axis=0)
        vall = jnp.concatenate([cv_ref[...], v_ref[...]], axis=0)
    else:
        q_ref, k_ref, v_ref, o_ref = refs
        kall = k_ref[...]
        vall = v_ref[...]
    half = HEAD_DIM
    lane = lax.broadcasted_iota(jnp.int32, (1, LANE), 1)

    def both_halves(x, own_low):
        own = jnp.where((lane < half) if own_low else (lane >= half), x, 0.0)
        oth = pltpu.roll(own, half, 1)
        lo, hi = (own, oth) if own_low else (oth, own)
        return lo.astype(BF16), hi.astype(BF16)

    if transposed:
        heads = []
        for kvh in range(N_KV):
            blk = slice((kvh // 2) * LANE, (kvh // 2 + 1) * LANE)
            ks = both_halves(kall[:, blk], kvh % 2 == 0)
            r0 = (kvh % 2) * HEAD_DIM
            v_t = vall[:, blk].T[r0:r0 + HEAD_DIM]
            v_t = jnp.concatenate([v_t, jnp.ones((ONES_ROWS, v_t.shape[1]), F32)], axis=0).astype(BF16)
            for pair in range(2):
                heads += [(kk, v_t, kvh * 4 * HEAD_DIM + pair * LANE) for kk in ks]

        def scores(h):
            kk, _, c0 = heads[h]
            return _dot_nt(kk, q_ref[:, c0:c0 + LANE])

        s_next = scores(0)
        outs = []
        for h, (_, v_t, c0) in enumerate(heads):
            s = s_next
            if h + 1 < len(heads):
                s_next = scores(h + 1)
            e = jnp.exp2(s - jnp.max(s, axis=0, keepdims=True)).astype(BF16)
            pv = _dot(v_t, e)
            outs.append(pv[:HEAD_DIM] / pv[HEAD_DIM:HEAD_DIM + 1])
            if h % 2 == 1:
                o_ref[:, c0:c0 + LANE] = jnp.concatenate(outs, axis=0).T.astype(BF16)
                outs = []
        return

    for kvh in range(N_KV):
        blk = slice((kvh // 2) * LANE, (kvh // 2 + 1) * LANE)
        ks = both_halves(kall[:, blk], kvh % 2 == 0)
        vs = both_halves(vall[:, blk], kvh % 2 == 0)
        for pair in range(2):
            c0 = kvh * 4 * HEAD_DIM + pair * LANE
            qp = q_ref[:, c0:c0 + LANE]
            acc = jnp.zeros((qp.shape[0], LANE), F32)
            for kk, vv in zip(ks, vs):
                s = _dot_nt(qp, kk)
                e = jnp.exp2(s - jnp.max(s, axis=-1, keepdims=True))
                l = jnp.sum(e, axis=-1, keepdims=True)
                acc = acc + _dot(e.astype(BF16), vv) / l
            o_ref[:, c0:c0 + LANE] = acc.astype(BF16)


def _attention(q, k, v, cache_k, cache_v, layer, *, seq, name):
    t = q.shape[0]
    cached = cache_k is not None
    in_specs = [pl.BlockSpec((seq, D), lambda i: (i, 0)), pl.BlockSpec((seq, KV_W), lambda i: (i, 0)),
                pl.BlockSpec((seq, KV_W), lambda i: (i, 0))]
    args = [q, k, v]
    if cached:
        in_specs += [pl.BlockSpec((None, None, PAST, KV_W), lambda i: (i, layer, 0, 0))] * 2
        args += [cache_k, cache_v]
    return pl.pallas_call(
        functools.partial(_attn_kernel, cached=cached),
        grid=(t // seq,), in_specs=in_specs,
        out_specs=pl.BlockSpec((seq, D), lambda i: (i, 0)),
        out_shape=jax.ShapeDtypeStruct((t, D), BF16),
        compiler_params=_cparams(1), name=name,
    )(*args)


def _merge_kernel(a_ref, o_ref, c_ref, gate_ref, x_ref, mod_ref, n2g_ref, wa_ref, wb_ref, wc_ref, wo_ref,
                  x1_ref, h2_ref):
    mix = gate_ref[:, 0:D].astype(F32) * _dot(a_ref[...], wa_ref[...])
    mix = mix + gate_ref[:, D:2 * D].astype(F32) * _dot(o_ref[...], wb_ref[...])
    mix = mix + gate_ref[:, 2 * D:3 * D].astype(F32) * _dot(c_ref[...], wc_ref[...])
    out = _dot(mix.astype(BF16), wo_ref[...])
    mod = mod_ref[...]
    g1, sh2, sc2 = mod[:, 2 * D:3 * D], mod[:, 3 * D:4 * D], mod[:, 4 * D:5 * D]
    x1 = x_ref[...] + g1 * out
    x1_ref[...] = x1
    h2_ref[...] = ((_rms(x1) * n2g_ref[...]) * (1.0 + sc2) + sh2).astype(BF16)


def _merge(a, o, c, gates, x, mod_l, n2g, wa, wb, wc, wo, *, layer, tm, row_of_tile, name):
    t = x.shape[0]
    row = lambda w: pl.BlockSpec((tm, w), lambda i: (i, 0))
    wspec = _resident((D, D), layer)
    return pl.pallas_call(
        _merge_kernel,
        grid=(t // tm,),
        in_specs=[row(D), row(D), row(D), row(3 * D), row(D),
                  pl.BlockSpec((None, 1, 6 * D), lambda i: (row_of_tile(i), 0, 0)),
                  pl.BlockSpec((1, D), lambda i: (0, 0)), wspec, wspec, wspec, wspec],
        out_specs=[row(D), row(D)],
        out_shape=[jax.ShapeDtypeStruct((t, D), F32), jax.ShapeDtypeStruct((t, D), BF16)],
        compiler_params=_cparams(1), name=name,
    )(a, o, c, gates, x, mod_l, n2g, wa, wb, wc, wo)


FF_TN = 512
FF_ROWS = 1024


def _ffn_kernel(h_ref, wg_ref, wv_ref, cwg_ref, cwv_ref, cbg_ref, cbv_ref, wd_ref, x1_ref, mod_ref, fg_ref,
                out_ref, acc_ref, *, seq, final):
    j = pl.program_id(1)
    h = h_ref[...]

    def conv(z, cw_ref, cb_ref):
        y = cb_ref[...] + _shift_prev(z, seq) * cw_ref[0:1, :]
        y = y + z * cw_ref[1:2, :]
        return y + _shift_next(z, seq) * cw_ref[2:3, :]

    g = conv(_dot(h, wg_ref[...]), cwg_ref, cbg_ref)
    v = conv(_dot(h, wv_ref[...]), cwv_ref, cbv_ref)
    hg = 0.5 * g
    act = ((hg * jnp.tanh(hg) + hg) * v).astype(BF16)
    part = _dot(act, wd_ref[...])

    @pl.when(j == 0)
    def _():
        acc_ref[...] = part

    @pl.when(j > 0)
    def _():
        acc_ref[...] += part

    @pl.when(j == pl.num_programs(1) - 1)
    def _():
        x2 = x1_ref[...] + mod_ref[:, 5 * D:6 * D] * acc_ref[...]
        out_ref[...] = _rms(x2) * fg_ref[...] if final else x2


def _ffn(h2, w_up, cw, cb, w_down, x1, mod_l, fg, *, layer, seq, row_of_tile, final, name):
    t = h2.shape[0]
    nj = D_FF // FF_TN
    return pl.pallas_call(
        functools.partial(_ffn_kernel, seq=seq, final=final),
        grid=(t // FF_ROWS, nj),
        in_specs=[pl.BlockSpec((FF_ROWS, D), lambda i, j: (i, 0)),
                  pl.BlockSpec((None, D, FF_TN), lambda i, j: (layer, 0, j)),
                  pl.BlockSpec((None, D, FF_TN), lambda i, j: (layer, 0, nj + j)),
                  pl.BlockSpec((3, FF_TN), lambda i, j: (0, j)),
                  pl.BlockSpec((3, FF_TN), lambda i, j: (0, nj + j)),
                  pl.BlockSpec((1, FF_TN), lambda i, j: (0, j)),
                  pl.BlockSpec((1, FF_TN), lambda i, j: (0, nj + j)),
                  pl.BlockSpec((None, FF_TN, D), lambda i, j: (layer, j, 0)),
                  pl.BlockSpec((FF_ROWS, D), lambda i, j: (i, 0)),
                  pl.BlockSpec((None, 1, 6 * D), lambda i, j: (row_of_tile(i), 0, 0)),
                  pl.BlockSpec((1, D), lambda i, j: (0, 0))],
        out_specs=pl.BlockSpec((FF_ROWS, D), lambda i, j: (i, 0)),
        out_shape=jax.ShapeDtypeStruct((t, D), F32),
        scratch_shapes=[pltpu.VMEM((FF_ROWS, D), F32)],
        compiler_params=_cparams(2), name=name,
    )(h2, w_up, w_up, cw, cw, cb, cb, w_down, x1, mod_l, fg)


def _interleave(x):
    b, s, w = x.shape
    return x.reshape(b, SUB, s // SUB, w).transpose(0, 2, 1, 3).reshape(b * s, w)


def _deinterleave(y, b, s):
    return y.reshape(b, s // SUB, SUB, -1).transpose(0, 2, 1, 3).reshape(b, s, -1)


def _rope_tables():
    p = np.arange(S_LAT)
    t = (p % SUB) * (S_LAT // SUB) + p // SUB
    pos = np.stack([t // GRID_W, t % GRID_W], axis=1).astype(np.float32)
    quarter = HEAD_DIM // 4
    inv = (ROPE_BASE ** (-np.arange(quarter, dtype=np.float32) / quarter)).astype(np.float32)
    d = np.arange(LANE) % HEAD_DIM
    which = d // (HEAD_DIM // 2)
    e = d % (HEAD_DIM // 2)
    ang = pos[:, which] * inv[e % quarter][None, :]
    sign = np.where(e < quarter, -1.0, 1.0).astype(np.float32)
    return jnp.asarray(np.cos(ang).astype(np.float32)), jnp.asarray((np.sin(ang) * sign).astype(np.float32))


def _block_diag_gates(w_r, w_i):
    eye = jnp.eye(4, dtype=F32)

    def bd(w):
        w = w.reshape(2, 4, 4, LRU_BW, LRU_BW)
        w = w[:, :, :, :, None, :] * eye[None, None, :, None, :, None]
        return w.reshape(2, 4, GB, GB)

    return jnp.concatenate([bd(w_r), bd(w_i)], axis=-1).astype(BF16)


def _trunk(x_prompt, x_sample, cache_k, cache_v, state_lru, c, c_ctx, norm1_g, norm2_g, w_mod, b_mod, w_in,
           b_gate, conv_w, conv_b, w_rg, b_rg, w_ig, b_ig, lru_lambda, q_norm_g, k_norm_g, w_pool, pool_scale,
           w_br_a, w_br_b, w_br_c, w_o, w_up, ffn_conv_w, ffn_conv_b, w_down, final_norm_g,
           paths=("ctx", "lat")):
    cond = jnp.zeros((16, D), F32).at[0:N_LAT].set(c).at[N_LAT].set(c_ctx)
    mod = _modulation(cond, w_mod, b_mod)
    rope_tabs = _rope_tables()
    gmat = jnp.asarray(np.kron(np.eye(2), np.full((HEAD_DIM, HEAD_DIM), 1.0 / HEAD_DIM)), BF16)
    ck = cache_k.reshape(N_LAT, DEPTH, PAST, KV_W)
    cv = cache_v.reshape(N_LAT, DEPTH, PAST, KV_W)
    fg = final_norm_g.reshape(1, D)

    xs = {"ctx": _interleave(x_prompt), "lat": _interleave(x_sample)}
    cfg = {"ctx": dict(seq=S_CTX, tm=512, row=lambda i: N_LAT),
           "lat": dict(seq=S_LAT, tm=512, row=lambda i: i // (S_LAT // 512))}
    new_k, new_v, new_s = [], [], []
    w_in_b, wa, wb, wc, wo, w_up_b, w_down_b, wp = (
        w.astype(BF16) for w in (w_in, w_br_a, w_br_b, w_br_c, w_o, w_up, w_down, w_pool))
    for l in range(DEPTH):
        mod_l = mod[l].reshape(16, 1, 6 * D)
        wg = _block_diag_gates(0.5 * w_rg[l], 0.5 * w_ig[l])
        bg = 0.5 * jnp.stack([b_rg[l], b_ig[l]], axis=1)
        qg = jnp.tile(q_norm_g[l], 2).reshape(1, LANE)
        kg = jnp.tile(k_norm_g[l], 2).reshape(1, LANE)
        for path in paths:
            seq, tm, row = cfg[path]["seq"], cfg[path]["tm"], cfg[path]["row"]
            lat = path == "lat"
            x = xs[path]
            xr, gy, q, k, v, up, gates = _inproj(
                x, mod_l, norm1_g[l].reshape(1, D), w_in_b, b_gate[l].reshape(1, 3 * D), qg, kg, gmat,
                rope_tabs if lat else None, layer=l, tm=tm, row_of_tile=row, name=f"inproj_{path}{l}")
            h0 = state_lru[:, l] if lat else jnp.zeros((N_CTX, 2, LRU_W), F32)
            a_pre, ends = _lru(xr, gy, h0, conv_w[l], conv_b[l].reshape(1, LRU_W), wg, bg, lru_lambda[l],
                               seq=seq, name=f"lru_{path}{l}")
            c_pre = _pool(up, wp, pool_scale[l].reshape(1, D), layer=l, seq=seq, name=f"pool_{path}{l}")
            o = _attention(q, k, v, ck if lat else None, cv if lat else None, l, seq=seq,
                           name=f"attn_{path}{l}")
            x1, h2 = _merge(a_pre, o, c_pre, gates, x, mod_l, norm2_g[l].reshape(1, D), wa, wb, wc, wo,
                            layer=l, tm=tm, row_of_tile=row, name=f"merge_{path}{l}")
            ff_row = (lambda i: i) if lat else row
            xs[path] = _ffn(h2, w_up_b, ffn_conv_w[l], ffn_conv_b[l].reshape(1, 2 * D_FF), w_down_b, x1, mod_l,
                            fg, layer=l, seq=seq, row_of_tile=ff_row, final=(l == DEPTH - 1),
                            name=f"ffn_{path}{l}")
            if not lat:
                new_k.append(_deinterleave(k, N_CTX, S_CTX).reshape(N_CTX, S_CTX, N_KV, HEAD_DIM))
                new_v.append(_deinterleave(v, N_CTX, S_CTX).reshape(N_CTX, S_CTX, N_KV, HEAD_DIM))
                new_s.append(jnp.stack([ends[:, SUB - 1], ends[:, SUB]], axis=1))
    return xs, new_k, new_v, new_s


def kernel(x_prompt, x_sample, cache_k, cache_v, state_lru, c, c_ctx, norm1_g, norm2_g, w_mod, b_mod, w_in,
           b_gate, conv_w, conv_b, w_rg, b_rg, w_ig, b_ig, lru_lambda, q_norm_g, k_norm_g, w_pool, pool_scale,
           w_br_a, w_br_b, w_br_c, w_o, w_up, ffn_conv_w, ffn_conv_b, w_down, final_norm_g):
    xs, new_k, new_v, new_s = _trunk(
        x_prompt, x_sample, cache_k, cache_v, state_lru, c, c_ctx, norm1_g, norm2_g, w_mod, b_mod, w_in,
        b_gate, conv_w, conv_b, w_rg, b_rg, w_ig, b_ig, lru_lambda, q_norm_g, k_norm_g, w_pool, pool_scale,
        w_br_a, w_br_b, w_br_c, w_o, w_up, ffn_conv_w, ffn_conv_b, w_down, final_norm_g)
    y_prompt = _deinterleave(xs["ctx"], N_CTX, S_CTX)
    y_sample = _deinterleave(xs["lat"], N_LAT, S_LAT)
    return (y_prompt, y_sample, jnp.stack(new_k, axis=1), jnp.stack(new_v, axis=1), jnp.stack(new_s, axis=1))
```

```python
import functools

import numpy as np
import jax
import jax.numpy as jnp
from jax import lax
from jax.experimental import pallas as pl
from jax.experimental.pallas import tpu as pltpu

F32 = jnp.float32
BF16 = jnp.bfloat16

D = 1024
DEPTH = 2
N_CTX, S_CTX = 16, 256
N_LAT, S_LAT = 8, 1024
PAST = 256
GRID_W = 64
LRU_W = 1024
LRU_BW = 64
LRU_C = 8.0
N_HEADS, N_KV, HEAD_DIM = 16, 4, 64
KV_W = N_KV * HEAD_DIM
ROPE_BASE = 10000.0
D_FF = 3 * D
EPS = 1e-6
SUB = 8
LANE = 128
ROPE_SHIFT = HEAD_DIM // 4
C_XR, C_YR, C_Q, C_K, C_V, C_UP, C_GL, C_END = 0, 1024, 2048, 3072, 3328, 3584, 4608, 7680

LOG2E = float(np.log2(np.e))
Q_PRESCALE = LOG2E * HEAD_DIM ** -0.5

VMEM_LIMIT = 56 * 1024 * 1024


def _cparams(n_axes):
    return pltpu.CompilerParams(dimension_semantics=("arbitrary",) * n_axes,
                                vmem_limit_bytes=VMEM_LIMIT)


def _resident(shape, layer):
    return pl.BlockSpec((None,) + shape, lambda *_: (layer,) + (0,) * len(shape), pipeline_mode=pl.Buffered(1))


def _dot(a, b):
    return jnp.dot(a, b, preferred_element_type=F32)


def _dot_nt(a, b):
    return lax.dot_general(a, b, (((1,), (1,)), ((), ())), preferred_element_type=F32)


def _rms(x):
    return x * lax.rsqrt(jnp.mean(x * x, axis=-1, keepdims=True) + EPS)


def _shift_prev(x, rm, n=1):
    r, w = x.shape
    sub = lax.broadcasted_iota(jnp.int32, (SUB, w), 0)
    pieces = []
    for m in range(r // rm):
        base = m * rm
        for t in range(n):
            lo = base + rm - SUB * (n - t)
            pieces.append(jnp.where(sub == 0, 0.0, pltpu.roll(x[lo:lo + SUB], 1, 0)))
        pieces.append(x[base:base + rm - SUB * n])
    return jnp.concatenate(pieces, axis=0)


def _shift_next(x, rm, n=1):
    r, w = x.shape
    sub = lax.broadcasted_iota(jnp.int32, (SUB, w), 0)
    pieces = []
    for m in range(r // rm):
        base = m * rm
        pieces.append(x[base + SUB * n:base + rm])
        for t in range(n):
            lo = base + SUB * t
            pieces.append(jnp.where(sub == SUB - 1, 0.0, pltpu.roll(x[lo:lo + SUB], SUB - 1, 0)))
    return jnp.concatenate(pieces, axis=0)


def _mod_kernel(cond_ref, w_ref, b_ref, out_ref):
    c = cond_ref[...]
    s = (c * jax.nn.sigmoid(c)).astype(BF16)
    out_ref[...] = _dot(s, w_ref[...].astype(BF16)) + b_ref[...]


def _modulation(cond, w_mod, b_mod):
    tn = 1536
    return pl.pallas_call(
        _mod_kernel,
        grid=(DEPTH, 6 * D // tn),
        in_specs=[pl.BlockSpec((16, D), lambda l, j: (0, 0)),
                  pl.BlockSpec((None, D, tn), lambda l, j: (l, 0, j)),
                  pl.BlockSpec((None, 1, tn), lambda l, j: (l, 0, j))],
        out_specs=pl.BlockSpec((None, 16, tn), lambda l, j: (l, 0, j)),
        out_shape=jax.ShapeDtypeStruct((DEPTH, 16, 6 * D), F32),
        compiler_params=_cparams(2),
        name="modulation",
    )(cond, w_mod, b_mod.reshape(DEPTH, 1, 6 * D))


def _inproj_kernel(*refs, rope):
    if rope:
        (x_ref, mod_ref, n1g_ref, w_ref, bgate_ref, qg_ref, kg_ref, gmat_ref, cos_ref, sin_ref,
         xr_ref, gy_ref, q_ref, k_ref, v_ref, up_ref, gate_ref) = refs
    else:
        (x_ref, mod_ref, n1g_ref, w_ref, bgate_ref, qg_ref, kg_ref, gmat_ref,
         xr_ref, gy_ref, q_ref, k_ref, v_ref, up_ref, gate_ref) = refs
    mod = mod_ref[...]
    sh1, sc1 = mod[:, 0:D], mod[:, D:2 * D]
    h = ((_rms(x_ref[...]) * n1g_ref[...]) * (1.0 + sc1) + sh1).astype(BF16)

    def mm(c0, c1):
        return _dot(h, w_ref[:, c0:c1])

    xr_ref[...] = mm(C_XR, C_YR).astype(BF16)
    gy_ref[...] = jax.nn.gelu(mm(C_YR, C_Q)).astype(BF16)

    lane = lax.broadcasted_iota(jnp.int32, (1, LANE), 1)
    first = (lane & ROPE_SHIFT) == 0

    def head_norm(xb, g):
        ms = _dot((xb * xb).astype(BF16), gmat_ref[...])
        y = (xb * lax.rsqrt(ms + EPS)) * g
        if rope:
            partner = jnp.where(first, pltpu.roll(y, LANE - ROPE_SHIFT, 1), pltpu.roll(y, ROPE_SHIFT, 1))
            y = y * cos_ref[...] + partner * sin_ref[...]
        return y

    qa = mm(C_Q, C_K)
    for c in range(N_HEADS * HEAD_DIM // LANE):
        qn = head_norm(qa[:, c * LANE:(c + 1) * LANE], qg_ref[...])
        q_ref[:, c * LANE:(c + 1) * LANE] = (qn * Q_PRESCALE).astype(BF16)
    ka = mm(C_K, C_V)
    for c in range(KV_W // LANE):
        k_ref[:, c * LANE:(c + 1) * LANE] = head_norm(ka[:, c * LANE:(c + 1) * LANE], kg_ref[...])
    v_ref[...] = mm(C_V, C_UP)
    up_ref[...] = mm(C_UP, C_GL).astype(BF16)
    for g in range(3):
        z = mm(C_GL + g * D, C_GL + (g + 1) * D) + bgate_ref[:, g * D:(g + 1) * D]
        gate_ref[:, g * D:(g + 1) * D] = jax.nn.sigmoid(z).astype(BF16)


def _inproj(x, mod_l, n1g, w_in, b_gate, qg, kg, gmat, rope_tabs, *, layer, tm, row_of_tile, name):
    t = x.shape[0]
    rope = rope_tabs is not None
    full = lambda shape: pl.BlockSpec(shape, lambda i: (0,) * len(shape))
    in_specs = [pl.BlockSpec((tm, D), lambda i: (i, 0)),
                pl.BlockSpec((None, 1, 6 * D), lambda i: (row_of_tile(i), 0, 0)),
                full((1, D)), _resident((D, C_END), layer), full((1, 3 * D)), full((1, LANE)), full((1, LANE)),
                full((LANE, LANE))]
    args = [x, mod_l, n1g, w_in, b_gate, qg, kg, gmat]
    if rope:
        per_seq = S_LAT // tm
        in_specs += [pl.BlockSpec((tm, LANE), lambda i: (i % per_seq, 0))] * 2
        args += list(rope_tabs)
    row = lambda w: pl.BlockSpec((tm, w), lambda i: (i, 0))
    out_specs = [row(D), row(D), row(D), row(KV_W), row(KV_W), row(D), row(3 * D)]
    out_shape = [jax.ShapeDtypeStruct((t, D), BF16), jax.ShapeDtypeStruct((t, D), BF16),
                 jax.ShapeDtypeStruct((t, D), BF16), jax.ShapeDtypeStruct((t, KV_W), F32),
                 jax.ShapeDtypeStruct((t, KV_W), F32), jax.ShapeDtypeStruct((t, D), BF16),
                 jax.ShapeDtypeStruct((t, 3 * D), BF16)]
    return pl.pallas_call(
        functools.partial(_inproj_kernel, rope=rope),
        grid=(t // tm,), in_specs=in_specs, out_specs=out_specs, out_shape=out_shape,
        compiler_params=_cparams(1), name=name,
    )(*args)


GB = 256


def _lru_kernel(xr_ref, gy_ref, h0_ref, cw_ref, cb_ref, wg_ref, bg_ref, lam_ref,
                out_ref, ends_ref, a_s, u_s):
    r = xr_ref.shape[0]
    steps = r // SUB
    sub = lax.broadcasted_iota(jnp.int32, (SUB, GB), 0)
    for cb in range(LRU_W // GB):
        cols = slice(cb * GB, (cb + 1) * GB)
        x = xr_ref[:, cols].astype(F32)
        xn = _shift_next(x, r)
        xc = cb_ref[:, cols] + _shift_prev(x, r) * cw_ref[0:1, cols]
        xc = xc + x * cw_ref[1:2, cols]
        xc = xc + xn * cw_ref[2:3, cols]
        xc = xc + _shift_next(xn, r) * cw_ref[3:4, cols]
        lhs = xc.astype(BF16)
        xh = 0.5 * xc
        for d in range(2):
            g = _dot(lhs, wg_ref[d, cb])
            tr = jnp.tanh(g[:, :GB] + bg_ref[d, 0:1, cols])
            ti = jnp.tanh(g[:, GB:] + bg_ref[d, 1:2, cols])
            lam = lam_ref[d:d + 1, cols]
            log_sig = jnp.minimum(lam, 0.0) - jnp.log(1.0 + jnp.exp(-jnp.abs(lam)))
            ch = (0.5 * LRU_C) * log_sig
            a = jnp.exp2(tr * (ch * LOG2E) + ch * LOG2E)
            m2 = jnp.tanh(tr * (-ch) - ch) * (1.0 + a * a)
            mult = jnp.where(m2 > 0.0, m2 * lax.rsqrt(m2), 0.0)
            u = mult * (ti * xh + xh)
            a_s[d] = a
            u_s[d] = u
            e0 = 0 if d == 0 else r - SUB
            edge = sub == (0 if d == 0 else SUB - 1)
            ae = a_s[d, e0:e0 + SUB, :]
            u_s[d, e0:e0 + SUB, :] = u_s[d, e0:e0 + SUB, :] + jnp.where(edge, ae * h0_ref[d:d + 1, cols], 0.0)
            a_s[d, e0:e0 + SUB, :] = jnp.where(edge, 0.0, ae)

        def step(k, carry):
            hf, pf, hb, pb = carry
            rf = pl.multiple_of(k * SUB, SUB)
            rb = pl.multiple_of((steps - 1 - k) * SUB, SUB)
            af = a_s[0, pl.ds(rf, SUB), :]
            hf = af * hf + u_s[0, pl.ds(rf, SUB), :]
            pf = af * pf
            u_s[0, pl.ds(rf, SUB), :] = hf
            a_s[0, pl.ds(rf, SUB), :] = pf
            ab = a_s[1, pl.ds(rb, SUB), :]
            hb = ab * hb + u_s[1, pl.ds(rb, SUB), :]
            pb = ab * pb
            u_s[1, pl.ds(rb, SUB), :] = hb
            a_s[1, pl.ds(rb, SUB), :] = pb
            return hf, pf, hb, pb

        zero = jnp.zeros((SUB, GB), F32)
        one = jnp.ones((SUB, GB), F32)
        hf, pf, hb, pb = lax.fori_loop(0, steps, step, (zero, one, zero, one), unroll=8)

        ef, eb = hf, hb
        for sh in (1, 2, 4):
            keep_f = sub >= sh
            ef = ef + pf * jnp.where(keep_f, pltpu.roll(ef, sh, 0), 0.0)
            pf = pf * jnp.where(keep_f, pltpu.roll(pf, sh, 0), 1.0)
            keep_b = sub < SUB - sh
            eb = eb + pb * jnp.where(keep_b, pltpu.roll(eb, SUB - sh, 0), 0.0)
            pb = pb * jnp.where(keep_b, pltpu.roll(pb, SUB - sh, 0), 1.0)
        ends_ref[0:SUB, cols] = ef
        ends_ref[SUB:2 * SUB, cols] = eb
        cf = jnp.where(sub >= 1, pltpu.roll(ef, 1, 0), 0.0)
        cbk = jnp.where(sub < SUB - 1, pltpu.roll(eb, SUB - 1, 0), 0.0)
        cf2 = jnp.concatenate([cf, cf], axis=0)
        cb2 = jnp.concatenate([cbk, cbk], axis=0)

        def fix(m, _):
            rows = pl.ds(pl.multiple_of(m * 2 * SUB, 2 * SUB), 2 * SUB)
            hft = u_s[0, rows, :] + a_s[0, rows, :] * cf2
            hbt = u_s[1, rows, :] + a_s[1, rows, :] * cb2
            out_ref[rows, cols] = ((hft + hbt) * gy_ref[rows, cols].astype(F32)).astype(BF16)
            return 0

        lax.fori_loop(0, r // (2 * SUB), fix, 0, unroll=4)


def _lru(xr, gy, h0, conv_w, conv_b, wg, bg, lam, *, seq, name):
    t = xr.shape[0]
    nseq = t // seq
    full = lambda shape: pl.BlockSpec(shape, lambda i: (0,) * len(shape))
    return pl.pallas_call(
        _lru_kernel,
        grid=(nseq,),
        in_specs=[pl.BlockSpec((seq, D), lambda i: (i, 0)), pl.BlockSpec((seq, D), lambda i: (i, 0)),
                  pl.BlockSpec((None, 2, LRU_W), lambda i: (i, 0, 0)),
                  full((4, LRU_W)), full((1, LRU_W)), full((2, LRU_W // GB, GB, 2 * GB)),
                  full((2, 2, LRU_W)), full((2, LRU_W))],
        out_specs=[pl.BlockSpec((seq, D), lambda i: (i, 0)),
                   pl.BlockSpec((None, 2 * SUB, LRU_W), lambda i: (i, 0, 0))],
        out_shape=[jax.ShapeDtypeStruct((t, D), BF16), jax.ShapeDtypeStruct((nseq, 2 * SUB, LRU_W), F32)],
        scratch_shapes=[pltpu.VMEM((2, seq, GB), F32), pltpu.VMEM((2, seq, GB), F32)],
        compiler_params=_cparams(1), name=name,
    )(xr, gy, h0, conv_w, conv_b, wg, bg, lam)


POOL_G = 256


def _pool_kernel(up_ref, wp_ref, scale_ref, out_ref):
    r = up_ref.shape[0]
    steps = r // SUB
    row = lax.broadcasted_iota(jnp.int32, (r, 1), 0)
    t = (row & (SUB - 1)) * steps + jnp.right_shift(row, 3)
    for g in range(4):
        cols = slice(g * POOL_G, (g + 1) * POOL_G)
        x = up_ref[:, cols].astype(F32)
        half = 1 << g
        back, fwd = x, x
        for lvl in range(g):
            n = 1 << lvl
            back = back + _shift_prev(back, r, n)
            fwd = fwd + _shift_next(fwd, r, n)
        win = _shift_prev(back, r) + fwd
        cnt = jnp.minimum(t + half, r) - jnp.maximum(t - half, 0)
        d = (win / cnt.astype(F32) - x).astype(BF16)
        out_ref[:, cols] = (_dot(d, wp_ref[g]) * scale_ref[:, cols]).astype(BF16)


def _pool(up, w_pool, scale, *, layer, seq, name):
    t = up.shape[0]
    return pl.pallas_call(
        _pool_kernel,
        grid=(t // seq,),
        in_specs=[pl.BlockSpec((seq, D), lambda i: (i, 0)),
                  pl.BlockSpec((None, 4, POOL_G, POOL_G), lambda i: (layer, 0, 0, 0)),
                  pl.BlockSpec((1, D), lambda i: (0, 0))],
        out_specs=pl.BlockSpec((seq, D), lambda i: (i, 0)),
        out_shape=jax.ShapeDtypeStruct((t, D), BF16),
        compiler_params=_cparams(1), name=name,
    )(up, w_pool, scale)


ONES_ROWS = 16


def _attn_kernel(*refs, cached):
    transposed = cached
    if cached:
        q_ref, k_ref, v_ref, ck_ref, cv_ref, o_ref = refs
        kall = jnp.concatenate([ck_ref[...], k_ref[...]], axis=0)
        vall = jnp.concatenate([cv_ref[...], v_ref[...]], axis=0)
    else:
        q_ref, k_ref, v_ref, o_ref = refs
        kall = k_ref[...]
        vall = v_ref[...]
    half = HEAD_DIM
    lane = lax.broadcasted_iota(jnp.int32, (1, LANE), 1)

    def both_halves(x, own_low):
        own = jnp.where((lane < half) if own_low else (lane >= half), x, 0.0)
        oth = pltpu.roll(own, half, 1)
        lo, hi = (own, oth) if own_low else (oth, own)
        return lo.astype(BF16), hi.astype(BF16)

    if transposed:
        heads = []
        for kvh in range(N_KV):
            blk = slice((kvh // 2) * LANE, (kvh // 2 + 1) * LANE)
            ks = both_halves(kall[:, blk], kvh % 2 == 0)
            r0 = (kvh % 2) * HEAD_DIM
            v_t = vall[:, blk].T[r0:r0 + HEAD_DIM]
            v_t = jnp.concatenate([v_t, jnp.ones((ONES_ROWS, v_t.shape[1]), F32)], axis=0).astype(BF16)
            for pair in range(2):
                heads += [(kk, v_t, kvh * 4 * HEAD_DIM + pair * LANE) for kk in ks]

        def scores(h):
            kk, _, c0 = heads[h]
            return _dot_nt(kk, q_ref[:, c0:c0 + LANE])

        s_next = scores(0)
        outs = []
        for h, (_, v_t, c0) in enumerate(heads):
            s = s_next
            if h + 1 < len(heads):
                s_next = scores(h + 1)
            e = jnp.exp2(s - jnp.max(s, axis=0, keepdims=True)).astype(BF16)
            pv = _dot(v_t, e)
            outs.append(pv[:HEAD_DIM] / pv[HEAD_DIM:HEAD_DIM + 1])
            if h % 2 == 1:
                o_ref[:, c0:c0 + LANE] = jnp.concatenate(outs, axis=0).T.astype(BF16)
                outs = []
        return

    for kvh in range(N_KV):
        blk = slice((kvh // 2) * LANE, (kvh // 2 + 1) * LANE)
        ks = both_halves(kall[:, blk], kvh % 2 == 0)
        vs = both_halves(vall[:, blk], kvh % 2 == 0)
        for pair in range(2):
            c0 = kvh * 4 * HEAD_DIM + pair * LANE
            qp = q_ref[:, c0:c0 + LANE]
            acc = jnp.zeros((qp.shape[0], LANE), F32)
            for kk, vv in zip(ks, vs):
                s = _dot_nt(qp, kk)
                e = jnp.exp2(s - jnp.max(s, axis=-1, keepdims=True))
                l = jnp.sum(e, axis=-1, keepdims=True)
                acc = acc + _dot(e.astype(BF16), vv) / l
            o_ref[:, c0:c0 + LANE] = acc.astype(BF16)


def _attention(q, k, v, cache_k, cache_v, layer, *, seq, name):
    t = q.shape[0]
    cached = cache_k is not None
    in_specs = [pl.BlockSpec((seq, D), lambda i: (i, 0)), pl.BlockSpec((seq, KV_W), lambda i: (i, 0)),
                pl.BlockSpec((seq, KV_W), lambda i: (i, 0))]
    args = [q, k, v]
    if cached:
        in_specs += [pl.BlockSpec((None, None, PAST, KV_W), lambda i: (i, layer, 0, 0))] * 2
        args += [cache_k, cache_v]
    return pl.pallas_call(
        functools.partial(_attn_kernel, cached=cached),
        grid=(t // seq,), in_specs=in_specs,
        out_specs=pl.BlockSpec((seq, D), lambda i: (i, 0)),
        out_shape=jax.ShapeDtypeStruct((t, D), BF16),
        compiler_params=_cparams(1), name=name,
    )(*args)


def _merge_kernel(a_ref, o_ref, c_ref, gate_ref, x_ref, mod_ref, n2g_ref, wa_ref, wb_ref, wc_ref, wo_ref,
                  x1_ref, h2_ref):
    mix = gate_ref[:, 0:D].astype(F32) * _dot(a_ref[...], wa_ref[...])
    mix = mix + gate_ref[:, D:2 * D].astype(F32) * _dot(o_ref[...], wb_ref[...])
    mix = mix + gate_ref[:, 2 * D:3 * D].astype(F32) * _dot(c_ref[...], wc_ref[...])
    out = _dot(mix.astype(BF16), wo_ref[...])
    mod = mod_ref[...]
    g1, sh2, sc2 = mod[:, 2 * D:3 * D], mod[:, 3 * D:4 * D], mod[:, 4 * D:5 * D]
    x1 = x_ref[...] + g1 * out
    x1_ref[...] = x1
    h2_ref[...] = ((_rms(x1) * n2g_ref[...]) * (1.0 + sc2) + sh2).astype(BF16)


def _merge(a, o, c, gates, x, mod_l, n2g, wa, wb, wc, wo, *, layer, tm, row_of_tile, name):
    t = x.shape[0]
    row = lambda w: pl.BlockSpec((tm, w), lambda i: (i, 0))
    wspec = _resident((D, D), layer)
    return pl.pallas_call(
        _merge_kernel,
        grid=(t // tm,),
        in_specs=[row(D), row(D), row(D), row(3 * D), row(D),
                  pl.BlockSpec((None, 1, 6 * D), lambda i: (row_of_tile(i), 0, 0)),
                  pl.BlockSpec((1, D), lambda i: (0, 0)), wspec, wspec, wspec, wspec],
        out_specs=[row(D), row(D)],
        out_shape=[jax.ShapeDtypeStruct((t, D), F32), jax.ShapeDtypeStruct((t, D), BF16)],
        compiler_params=_cparams(1), name=name,
    )(a, o, c, gates, x, mod_l, n2g, wa, wb, wc, wo)


FF_TN = 1024
FF_ROWS = 1024
DOWN_TM = 1024


def _ffn_up_kernel(h_ref, wg_ref, wv_ref, cwg_ref, cwv_ref, cbg_ref, cbv_ref, out_ref, *, seq):
    h = h_ref[...]

    def conv(z, cw_ref, cb_ref):
        y = cb_ref[...] + _shift_prev(z, seq) * cw_ref[0:1, :]
        y = y + z * cw_ref[1:2, :]
        return y + _shift_next(z, seq) * cw_ref[2:3, :]

    g = conv(_dot(h, wg_ref[...]), cwg_ref, cbg_ref)
    v = conv(_dot(h, wv_ref[...]), cwv_ref, cbv_ref)
    hg = 0.5 * g
    out_ref[...] = ((hg * jnp.tanh(hg) + hg) * v).astype(BF16)


def _ffn_up(h2, w_up, cw, cb, *, layer, seq, name):
    t = h2.shape[0]
    nj = D_FF // FF_TN
    return pl.pallas_call(
        functools.partial(_ffn_up_kernel, seq=seq),
        grid=(t // FF_ROWS, nj),
        in_specs=[pl.BlockSpec((FF_ROWS, D), lambda i, j: (i, 0)),
                  pl.BlockSpec((None, D, FF_TN), lambda i, j: (layer, 0, j)),
                  pl.BlockSpec((None, D, FF_TN), lambda i, j: (layer, 0, nj + j)),
                  pl.BlockSpec((3, FF_TN), lambda i, j: (0, j)),
                  pl.BlockSpec((3, FF_TN), lambda i, j: (0, nj + j)),
                  pl.BlockSpec((1, FF_TN), lambda i, j: (0, j)),
                  pl.BlockSpec((1, FF_TN), lambda i, j: (0, nj + j))],
        out_specs=pl.BlockSpec((FF_ROWS, FF_TN), lambda i, j: (i, j)),
        out_shape=jax.ShapeDtypeStruct((t, D_FF), BF16),
        compiler_params=_cparams(2), name=name,
    )(h2, w_up, w_up, cw, cw, cb, cb)


def _ffn_down_kernel(act_ref, w_ref, x1_ref, mod_ref, fg_ref, out_ref, *, final):
    g2 = mod_ref[:, 5 * D:6 * D]
    x2 = x1_ref[...] + g2 * _dot(act_ref[...], w_ref[...])
    out_ref[...] = _rms(x2) * fg_ref[...] if final else x2


def _ffn_down(act, w_down, x1, mod_l, fg, *, layer, tm, row_of_tile, final, name):
    t = x1.shape[0]
    return pl.pallas_call(
        functools.partial(_ffn_down_kernel, final=final),
        grid=(t // tm,),
        in_specs=[pl.BlockSpec((tm, D_FF), lambda i: (i, 0)),
                  _resident((D_FF, D), layer),
                  pl.BlockSpec((tm, D), lambda i: (i, 0)),
                  pl.BlockSpec((None, 1, 6 * D), lambda i: (row_of_tile(i), 0, 0)),
                  pl.BlockSpec((1, D), lambda i: (0, 0))],
        out_specs=pl.BlockSpec((tm, D), lambda i: (i, 0)),
        out_shape=jax.ShapeDtypeStruct((t, D), F32),
        compiler_params=_cparams(1), name=name,
    )(act, w_down, x1, mod_l, fg)


def _interleave(x):
    b, s, w = x.shape
    return x.reshape(b, SUB, s // SUB, w).transpose(0, 2, 1, 3).reshape(b * s, w)


def _deinterleave(y, b, s):
    return y.reshape(b, s // SUB, SUB, -1).transpose(0, 2, 1, 3).reshape(b, s, -1)


def _rope_tables():
    p = np.arange(S_LAT)
    t = (p % SUB) * (S_LAT // SUB) + p // SUB
    pos = np.stack([t // GRID_W, t % GRID_W], axis=1).astype(np.float32)
    quarter = HEAD_DIM // 4
    inv = (ROPE_BASE ** (-np.arange(quarter, dtype=np.float32) / quarter)).astype(np.float32)
    d = np.arange(LANE) % HEAD_DIM
    which = d // (HEAD_DIM // 2)
    e = d % (HEAD_DIM // 2)
    ang = pos[:, which] * inv[e % quarter][None, :]
    sign = np.where(e < quarter, -1.0, 1.0).astype(np.float32)
    return jnp.asarray(np.cos(ang).astype(np.float32)), jnp.asarray((np.sin(ang) * sign).astype(np.float32))


def _block_diag_gates(w_r, w_i):
    eye = jnp.eye(4, dtype=F32)

    def bd(w):
        w = w.reshape(2, 4, 4, LRU_BW, LRU_BW)
        w = w[:, :, :, :, None, :] * eye[None, None, :, None, :, None]
        return w.reshape(2, 4, GB, GB)

    return jnp.concatenate([bd(w_r), bd(w_i)], axis=-1).astype(BF16)


def _trunk(x_prompt, x_sample, cache_k, cache_v, state_lru, c, c_ctx, norm1_g, norm2_g, w_mod, b_mod, w_in,
           b_gate, conv_w, conv_b, w_rg, b_rg, w_ig, b_ig, lru_lambda, q_norm_g, k_norm_g, w_pool, pool_scale,
           w_br_a, w_br_b, w_br_c, w_o, w_up, ffn_conv_w, ffn_conv_b, w_down, final_norm_g,
           paths=("ctx", "lat")):
    cond = jnp.zeros((16, D), F32).at[0:N_LAT].set(c).at[N_LAT].set(c_ctx)
    mod = _modulation(cond, w_mod, b_mod)
    rope_tabs = _rope_tables()
    gmat = jnp.asarray(np.kron(np.eye(2), np.full((HEAD_DIM, HEAD_DIM), 1.0 / HEAD_DIM)), BF16)
    ck = cache_k.reshape(N_LAT, DEPTH, PAST, KV_W)
    cv = cache_v.reshape(N_LAT, DEPTH, PAST, KV_W)
    fg = final_norm_g.reshape(1, D)

    xs = {"ctx": _interleave(x_prompt), "lat": _interleave(x_sample)}
    cfg = {"ctx": dict(seq=S_CTX, tm=512, row=lambda i: N_LAT),
           "lat": dict(seq=S_LAT, tm=512, row=lambda i: i // (S_LAT // 512))}
    new_k, new_v, new_s = [], [], []
    w_in_b, wa, wb, wc, wo, w_up_b, w_down_b, wp = (
        w.astype(BF16) for w in (w_in, w_br_a, w_br_b, w_br_c, w_o, w_up, w_down, w_pool))
    for l in range(DEPTH):
        mod_l = mod[l].reshape(16, 1, 6 * D)
        wg = _block_diag_gates(0.5 * w_rg[l], 0.5 * w_ig[l])
        bg = 0.5 * jnp.stack([b_rg[l], b_ig[l]], axis=1)
        qg = jnp.tile(q_norm_g[l], 2).reshape(1, LANE)
        kg = jnp.tile(k_norm_g[l], 2).reshape(1, LANE)
        for path in paths:
            seq, tm, row = cfg[path]["seq"], cfg[path]["tm"], cfg[path]["row"]
            lat = path == "lat"
            x = xs[path]
            xr, gy, q, k, v, up, gates = _inproj(
                x, mod_l, norm1_g[l].reshape(1, D), w_in_b, b_gate[l].reshape(1, 3 * D), qg, kg, gmat,
                rope_tabs if lat else None, layer=l, tm=tm, row_of_tile=row, name=f"inproj_{path}{l}")
            h0 = state_lru[:, l] if lat else jnp.zeros((N_CTX, 2, LRU_W), F32)
            a_pre, ends = _lru(xr, gy, h0, conv_w[l], conv_b[l].reshape(1, LRU_W), wg, bg, lru_lambda[l],
                               seq=seq, name=f"lru_{path}{l}")
            c_pre = _pool(up, wp, pool_scale[l].reshape(1, D), layer=l, seq=seq, name=f"pool_{path}{l}")
            o = _attention(q, k, v, ck if lat else None, cv if lat else None, l, seq=seq,
                           name=f"attn_{path}{l}")
            x1, h2 = _merge(a_pre, o, c_pre, gates, x, mod_l, norm2_g[l].reshape(1, D), wa, wb, wc, wo,
                            layer=l, tm=tm, row_of_tile=row, name=f"merge_{path}{l}")
            act = _ffn_up(h2, w_up_b, ffn_conv_w[l], ffn_conv_b[l].reshape(1, 2 * D_FF), layer=l, seq=seq,
                          name=f"ffn_up_{path}{l}")
            dn_row = (lambda i: i) if lat else row
            xs[path] = _ffn_down(act, w_down_b, x1, mod_l, fg, layer=l, tm=DOWN_TM, row_of_tile=dn_row,
                                 final=(l == DEPTH - 1), name=f"ffn_down_{path}{l}")
            if not lat:
                new_k.append(_deinterleave(k, N_CTX, S_CTX).reshape(N_CTX, S_CTX, N_KV, HEAD_DIM))
                new_v.append(_deinterleave(v, N_CTX, S_CTX).reshape(N_CTX, S_CTX, N_KV, HEAD_DIM))
                new_s.append(jnp.stack([ends[:, SUB - 1], ends[:, SUB]], axis=1))
    return xs, new_k, new_v, new_s


def kernel(x_prompt, x_sample, cache_k, cache_v, state_lru, c, c_ctx, norm1_g, norm2_g, w_mod, b_mod, w_in,
           b_gate, conv_w, conv_b, w_rg, b_rg, w_ig, b_ig, lru_lambda, q_norm_g, k_norm_g, w_pool, pool_scale,
           w_br_a, w_br_b, w_br_c, w_o, w_up, ffn_conv_w, ffn_conv_b, w_down, final_norm_g):
    xs, new_k, new_v, new_s = _trunk(
        x_prompt, x_sample, cache_k, cache_v, state_lru, c, c_ctx, norm1_g, norm2_g, w_mod, b_mod, w_in,
        b_gate, conv_w, conv_b, w_rg, b_rg, w_ig, b_ig, lru_lambda, q_norm_g, k_norm_g, w_pool, pool_scale,
        w_br_a, w_br_b, w_br_c, w_o, w_up, ffn_conv_w, ffn_conv_b, w_down, final_norm_g)
    y_prompt = _deinterleave(xs["ctx"], N_CTX, S_CTX)
    y_sample = _deinterleave(xs["lat"], N_LAT, S_LAT)
    return (y_prompt, y_sample, jnp.stack(new_k, axis=1), jnp.stack(new_v, axis=1), jnp.stack(new_s, axis=1))
```

```python
import functools

import numpy as np
import jax
import jax.numpy as jnp
from jax import lax
from jax.experimental import pallas as pl
from jax.experimental.pallas import tpu as pltpu

F32 = jnp.float32
BF16 = jnp.bfloat16

D = 1024
DEPTH = 2
N_CTX, S_CTX = 16, 256
N_LAT, S_LAT = 8, 1024
PAST = 256
GRID_W = 64
LRU_W = 1024
LRU_BW = 64
LRU_C = 8.0
N_HEADS, N_KV, HEAD_DIM = 16, 4, 64
KV_W = N_KV * HEAD_DIM
ROPE_BASE = 10000.0
D_FF = 3 * D
EPS = 1e-6
SUB = 8
LANE = 128
ROPE_SHIFT = HEAD_DIM // 4
C_XR, C_YR, C_Q, C_K, C_V, C_UP, C_GL, C_END = 0, 1024, 2048, 3072, 3328, 3584, 4608, 7680

LOG2E = float(np.log2(np.e))
Q_PRESCALE = LOG2E * HEAD_DIM ** -0.5

VMEM_LIMIT = 56 * 1024 * 1024


def _cparams(n_axes):
    return pltpu.CompilerParams(dimension_semantics=("arbitrary",) * n_axes,
                                vmem_limit_bytes=VMEM_LIMIT)


def _resident(shape, layer):
    return pl.BlockSpec((None,) + shape, lambda *_: (layer,) + (0,) * len(shape), pipeline_mode=pl.Buffered(1))


def _dot(a, b):
    return jnp.dot(a, b, preferred_element_type=F32)


def _dot_nt(a, b):
    return lax.dot_general(a, b, (((1,), (1,)), ((), ())), preferred_element_type=F32)


def _rms(x):
    return x * lax.rsqrt(jnp.mean(x * x, axis=-1, keepdims=True) + EPS)


def _shift_prev(x, rm, n=1):
    r, w = x.shape
    sub = lax.broadcasted_iota(jnp.int32, (SUB, w), 0)
    pieces = []
    for m in range(r // rm):
        base = m * rm
        for t in range(n):
            lo = base + rm - SUB * (n - t)
            pieces.append(jnp.where(sub == 0, 0.0, pltpu.roll(x[lo:lo + SUB], 1, 0)))
        pieces.append(x[base:base + rm - SUB * n])
    return jnp.concatenate(pieces, axis=0)


def _shift_next(x, rm, n=1):
    r, w = x.shape
    sub = lax.broadcasted_iota(jnp.int32, (SUB, w), 0)
    pieces = []
    for m in range(r // rm):
        base = m * rm
        pieces.append(x[base + SUB * n:base + rm])
        for t in range(n):
            lo = base + SUB * t
            pieces.append(jnp.where(sub == SUB - 1, 0.0, pltpu.roll(x[lo:lo + SUB], SUB - 1, 0)))
    return jnp.concatenate(pieces, axis=0)


def _mod_kernel(cond_ref, w_ref, b_ref, out_ref):
    c = cond_ref[...]
    s = (c * jax.nn.sigmoid(c)).astype(BF16)
    out_ref[...] = _dot(s, w_ref[...].astype(BF16)) + b_ref[...]


def _modulation(cond, w_mod, b_mod):
    tn = 1536
    return pl.pallas_call(
        _mod_kernel,
        grid=(DEPTH, 6 * D // tn),
        in_specs=[pl.BlockSpec((16, D), lambda l, j: (0, 0)),
                  pl.BlockSpec((None, D, tn), lambda l, j: (l, 0, j)),
                  pl.BlockSpec((None, 1, tn), lambda l, j: (l, 0, j))],
        out_specs=pl.BlockSpec((None, 16, tn), lambda l, j: (l, 0, j)),
        out_shape=jax.ShapeDtypeStruct((DEPTH, 16, 6 * D), F32),
        compiler_params=_cparams(2),
        name="modulation",
    )(cond, w_mod, b_mod.reshape(DEPTH, 1, 6 * D))


def _inproj_kernel(*refs, rope):
    if rope:
        (x_ref, mod_ref, n1g_ref, w_ref, bgate_ref, qg_ref, kg_ref, gmat_ref, cos_ref, sin_ref,
         xr_ref, gy_ref, q_ref, k_ref, v_ref, up_ref, gate_ref) = refs
    else:
        (x_ref, mod_ref, n1g_ref, w_ref, bgate_ref, qg_ref, kg_ref, gmat_ref,
         xr_ref, gy_ref, q_ref, k_ref, v_ref, up_ref, gate_ref) = refs
    mod = mod_ref[...]
    sh1, sc1 = mod[:, 0:D], mod[:, D:2 * D]
    h = ((_rms(x_ref[...]) * n1g_ref[...]) * (1.0 + sc1) + sh1).astype(BF16)

    def mm(c0, c1):
        return _dot(h, w_ref[:, c0:c1])

    for g in range(3):
        z = mm(C_GL + g * D, C_GL + (g + 1) * D) + bgate_ref[:, g * D:(g + 1) * D]
        gate_ref[:, g * D:(g + 1) * D] = jax.nn.sigmoid(z).astype(BF16)
    gy_ref[...] = jax.nn.gelu(mm(C_YR, C_Q)).astype(BF16)

    lane = lax.broadcasted_iota(jnp.int32, (1, LANE), 1)
    first = (lane & ROPE_SHIFT) == 0

    def head_norm(xb, g):
        ms = _dot((xb * xb).astype(BF16), gmat_ref[...])
        y = (xb * lax.rsqrt(ms + EPS)) * g
        if rope:
            partner = jnp.where(first, pltpu.roll(y, LANE - ROPE_SHIFT, 1), pltpu.roll(y, ROPE_SHIFT, 1))
            y = y * cos_ref[...] + partner * sin_ref[...]
        return y

    qa = mm(C_Q, C_K)
    for c in range(N_HEADS * HEAD_DIM // LANE):
        qn = head_norm(qa[:, c * LANE:(c + 1) * LANE], qg_ref[...])
        q_ref[:, c * LANE:(c + 1) * LANE] = (qn * Q_PRESCALE).astype(BF16)
    ka = mm(C_K, C_V)
    for c in range(KV_W // LANE):
        k_ref[:, c * LANE:(c + 1) * LANE] = head_norm(ka[:, c * LANE:(c + 1) * LANE], kg_ref[...])
    up_ref[...] = mm(C_UP, C_GL).astype(BF16)
    xr_ref[...] = mm(C_XR, C_YR).astype(BF16)
    v_ref[...] = mm(C_V, C_UP)


def _inproj(x, mod_l, n1g, w_in, b_gate, qg, kg, gmat, rope_tabs, *, layer, tm, row_of_tile, name):
    t = x.shape[0]
    rope = rope_tabs is not None
    full = lambda shape: pl.BlockSpec(shape, lambda i: (0,) * len(shape))
    in_specs = [pl.BlockSpec((tm, D), lambda i: (i, 0)),
                pl.BlockSpec((None, 1, 6 * D), lambda i: (row_of_tile(i), 0, 0)),
                full((1, D)), _resident((D, C_END), layer), full((1, 3 * D)), full((1, LANE)), full((1, LANE)),
                full((LANE, LANE))]
    args = [x, mod_l, n1g, w_in, b_gate, qg, kg, gmat]
    if rope:
        per_seq = S_LAT // tm
        in_specs += [pl.BlockSpec((tm, LANE), lambda i: (i % per_seq, 0))] * 2
        args += list(rope_tabs)
    row = lambda w: pl.BlockSpec((tm, w), lambda i: (i, 0))
    out_specs = [row(D), row(D), row(D), row(KV_W), row(KV_W), row(D), row(3 * D)]
    out_shape = [jax.ShapeDtypeStruct((t, D), BF16), jax.ShapeDtypeStruct((t, D), BF16),
                 jax.ShapeDtypeStruct((t, D), BF16), jax.ShapeDtypeStruct((t, KV_W), F32),
                 jax.ShapeDtypeStruct((t, KV_W), F32), jax.ShapeDtypeStruct((t, D), BF16),
                 jax.ShapeDtypeStruct((t, 3 * D), BF16)]
    return pl.pallas_call(
        functools.partial(_inproj_kernel, rope=rope),
        grid=(t // tm,), in_specs=in_specs, out_specs=out_specs, out_shape=out_shape,
        compiler_params=_cparams(1), name=name,
    )(*args)


GB = 256


def _lru_kernel(xr_ref, gy_ref, h0_ref, cw_ref, cb_ref, wg_ref, bg_ref, lam_ref,
                out_ref, ends_ref, a_s, u_s):
    r = xr_ref.shape[0]
    steps = r // SUB
    sub = lax.broadcasted_iota(jnp.int32, (SUB, GB), 0)
    for cb in range(LRU_W // GB):
        cols = slice(cb * GB, (cb + 1) * GB)
        x = xr_ref[:, cols].astype(F32)
        xn = _shift_next(x, r)
        xc = cb_ref[:, cols] + _shift_prev(x, r) * cw_ref[0:1, cols]
        xc = xc + x * cw_ref[1:2, cols]
        xc = xc + xn * cw_ref[2:3, cols]
        xc = xc + _shift_next(xn, r) * cw_ref[3:4, cols]
        lhs = xc.astype(BF16)
        xh = 0.5 * xc
        for d in range(2):
            g = _dot(lhs, wg_ref[d, cb])
            tr = jnp.tanh(g[:, :GB] + bg_ref[d, 0:1, cols])
            ti = jnp.tanh(g[:, GB:] + bg_ref[d, 1:2, cols])
            lam = lam_ref[d:d + 1, cols]
            log_sig = jnp.minimum(lam, 0.0) - jnp.log(1.0 + jnp.exp(-jnp.abs(lam)))
            ch = (0.5 * LRU_C) * log_sig
            a = jnp.exp2(tr * (ch * LOG2E) + ch * LOG2E)
            m2 = jnp.tanh(tr * (-ch) - ch) * (1.0 + a * a)
            mult = jnp.where(m2 > 0.0, m2 * lax.rsqrt(m2), 0.0)
            u = mult * (ti * xh + xh)
            a_s[d] = a
            u_s[d] = u
            e0 = 0 if d == 0 else r - SUB
            edge = sub == (0 if d == 0 else SUB - 1)
            ae = a_s[d, e0:e0 + SUB, :]
            u_s[d, e0:e0 + SUB, :] = u_s[d, e0:e0 + SUB, :] + jnp.where(edge, ae * h0_ref[d:d + 1, cols], 0.0)
            a_s[d, e0:e0 + SUB, :] = jnp.where(edge, 0.0, ae)

        def step(k, carry):
            hf, pf, hb, pb = carry
            rf = pl.multiple_of(k * SUB, SUB)
            rb = pl.multiple_of((steps - 1 - k) * SUB, SUB)
            af = a_s[0, pl.ds(rf, SUB), :]
            hf = af * hf + u_s[0, pl.ds(rf, SUB), :]
            pf = af * pf
            u_s[0, pl.ds(rf, SUB), :] = hf
            a_s[0, pl.ds(rf, SUB), :] = pf
            ab = a_s[1, pl.ds(rb, SUB), :]
            hb = ab * hb + u_s[1, pl.ds(rb, SUB), :]
            pb = ab * pb
            u_s[1, pl.ds(rb, SUB), :] = hb
            a_s[1, pl.ds(rb, SUB), :] = pb
            return hf, pf, hb, pb

        zero = jnp.zeros((SUB, GB), F32)
        one = jnp.ones((SUB, GB), F32)
        hf, pf, hb, pb = lax.fori_loop(0, steps, step, (zero, one, zero, one), unroll=8)

        ef, eb = hf, hb
        for sh in (1, 2, 4):
            keep_f = sub >= sh
            ef = ef + pf * jnp.where(keep_f, pltpu.roll(ef, sh, 0), 0.0)
            pf = pf * jnp.where(keep_f, pltpu.roll(pf, sh, 0), 1.0)
            keep_b = sub < SUB - sh
            eb = eb + pb * jnp.where(keep_b, pltpu.roll(eb, SUB - sh, 0), 0.0)
            pb = pb * jnp.where(keep_b, pltpu.roll(pb, SUB - sh, 0), 1.0)
        ends_ref[0:SUB, cols] = ef
        ends_ref[SUB:2 * SUB, cols] = eb
        cf = jnp.where(sub >= 1, pltpu.roll(ef, 1, 0), 0.0)
        cbk = jnp.where(sub < SUB - 1, pltpu.roll(eb, SUB - 1, 0), 0.0)
        cf2 = jnp.concatenate([cf, cf], axis=0)
        cb2 = jnp.concatenate([cbk, cbk], axis=0)

        def fix(m, _):
            rows = pl.ds(pl.multiple_of(m * 2 * SUB, 2 * SUB), 2 * SUB)
            hft = u_s[0, rows, :] + a_s[0, rows, :] * cf2
            hbt = u_s[1, rows, :] + a_s[1, rows, :] * cb2
            out_ref[rows, cols] = ((hft + hbt) * gy_ref[rows, cols].astype(F32)).astype(BF16)
            return 0

        lax.fori_loop(0, r // (2 * SUB), fix, 0, unroll=4)


def _lru(xr, gy, h0, conv_w, conv_b, wg, bg, lam, *, seq, name):
    t = xr.shape[0]
    nseq = t // seq
    full = lambda shape: pl.BlockSpec(shape, lambda i: (0,) * len(shape))
    return pl.pallas_call(
        _lru_kernel,
        grid=(nseq,),
        in_specs=[pl.BlockSpec((seq, D), lambda i: (i, 0)), pl.BlockSpec((seq, D), lambda i: (i, 0)),
                  pl.BlockSpec((None, 2, LRU_W), lambda i: (i, 0, 0)),
                  full((4, LRU_W)), full((1, LRU_W)), full((2, LRU_W // GB, GB, 2 * GB)),
                  full((2, 2, LRU_W)), full((2, LRU_W))],
        out_specs=[pl.BlockSpec((seq, D), lambda i: (i, 0)),
                   pl.BlockSpec((None, 2 * SUB, LRU_W), lambda i: (i, 0, 0))],
        out_shape=[jax.ShapeDtypeStruct((t, D), BF16), jax.ShapeDtypeStruct((nseq, 2 * SUB, LRU_W), F32)],
        scratch_shapes=[pltpu.VMEM((2, seq, GB), F32), pltpu.VMEM((2, seq, GB), F32)],
        compiler_params=_cparams(1), name=name,
    )(xr, gy, h0, conv_w, conv_b, wg, bg, lam)


POOL_G = 256


def _pool_kernel(up_ref, wp_ref, scale_ref, out_ref):
    r = up_ref.shape[0]
    steps = r // SUB
    row = lax.broadcasted_iota(jnp.int32, (r, 1), 0)
    t = (row & (SUB - 1)) * steps + jnp.right_shift(row, 3)
    for g in range(4):
        cols = slice(g * POOL_G, (g + 1) * POOL_G)
        x = up_ref[:, cols].astype(F32)
        half = 1 << g
        back, fwd = x, x
        for lvl in range(g):
            n = 1 << lvl
            back = back + _shift_prev(back, r, n)
            fwd = fwd + _shift_next(fwd, r, n)
        win = _shift_prev(back, r) + fwd
        cnt = jnp.minimum(t + half, r) - jnp.maximum(t - half, 0)
        d = (win / cnt.astype(F32) - x).astype(BF16)
        out_ref[:, cols] = (_dot(d, wp_ref[g]) * scale_ref[:, cols]).astype(BF16)


def _pool(up, w_pool, scale, *, layer, seq, name):
    t = up.shape[0]
    return pl.pallas_call(
        _pool_kernel,
        grid=(t // seq,),
        in_specs=[pl.BlockSpec((seq, D), lambda i: (i, 0)),
                  pl.BlockSpec((None, 4, POOL_G, POOL_G), lambda i: (layer, 0, 0, 0)),
                  pl.BlockSpec((1, D), lambda i: (0, 0))],
        out_specs=pl.BlockSpec((seq, D), lambda i: (i, 0)),
        out_shape=jax.ShapeDtypeStruct((t, D), BF16),
        compiler_params=_cparams(1), name=name,
    )(up, w_pool, scale)


ONES_ROWS = 16


def _attn_kernel(*refs, cached):
    transposed = cached
    if cached:
        q_ref, k_ref, v_ref, ck_ref, cv_ref, o_ref = refs
        kall = jnp.concatenate([ck_ref[...], k_ref[...]], axis=0)
        vall = jnp.concatenate([cv_ref[...], v_ref[...]], axis=0)
    else:
        q_ref, k_ref, v_ref, o_ref = refs
        kall = k_ref[...]
        vall = v_ref[...]
    half = HEAD_DIM
    lane = lax.broadcasted_iota(jnp.int32, (1, LANE), 1)

    def both_halves(x, own_low):
        own = jnp.where((lane < half) if own_low else (lane >= half), x, 0.0)
        oth = pltpu.roll(own, half, 1)
        lo, hi = (own, oth) if own_low else (oth, own)
        return lo.astype(BF16), hi.astype(BF16)

    if transposed:
        heads = []
        for kvh in range(N_KV):
            blk = slice((kvh // 2) * LANE, (kvh // 2 + 1) * LANE)
            ks = both_halves(kall[:, blk], kvh % 2 == 0)
            r0 = (kvh % 2) * HEAD_DIM
            v_t = vall[:, blk].T[r0:r0 + HEAD_DIM]
            v_t = jnp.concatenate([v_t, jnp.ones((ONES_ROWS, v_t.shape[1]), F32)], axis=0).astype(BF16)
            for pair in range(2):
                heads += [(kk, v_t, kvh * 4 * HEAD_DIM + pair * LANE) for kk in ks]

        def scores(h):
            kk, _, c0 = heads[h]
            return _dot_nt(kk, q_ref[:, c0:c0 + LANE])

        s_next = scores(0)
        outs = []
        for h, (_, v_t, c0) in enumerate(heads):
            s = s_next
            if h + 1 < len(heads):
                s_next = scores(h + 1)
            e = jnp.exp2(s - jnp.max(s, axis=0, keepdims=True)).astype(BF16)
            pv = _dot(v_t, e)
            outs.append(pv[:HEAD_DIM] / pv[HEAD_DIM:HEAD_DIM + 1])
            if h % 2 == 1:
                o_ref[:, c0:c0 + LANE] = jnp.concatenate(outs, axis=0).T.astype(BF16)
                outs = []
        return

    for kvh in range(N_KV):
        blk = slice((kvh // 2) * LANE, (kvh // 2 + 1) * LANE)
        ks = both_halves(kall[:, blk], kvh % 2 == 0)
        vs = both_halves(vall[:, blk], kvh % 2 == 0)
        for pair in range(2):
            c0 = kvh * 4 * HEAD_DIM + pair * LANE
            qp = q_ref[:, c0:c0 + LANE]
            acc = jnp.zeros((qp.shape[0], LANE), F32)
            for kk, vv in zip(ks, vs):
                s = _dot_nt(qp, kk)
                e = jnp.exp2(s - jnp.max(s, axis=-1, keepdims=True))
                l = jnp.sum(e, axis=-1, keepdims=True)
                acc = acc + _dot(e.astype(BF16), vv) / l
            o_ref[:, c0:c0 + LANE] = acc.astype(BF16)


def _attention(q, k, v, cache_k, cache_v, layer, *, seq, name):
    t = q.shape[0]
    cached = cache_k is not None
    in_specs = [pl.BlockSpec((seq, D), lambda i: (i, 0)), pl.BlockSpec((seq, KV_W), lambda i: (i, 0)),
                pl.BlockSpec((seq, KV_W), lambda i: (i, 0))]
    args = [q, k, v]
    if cached:
        in_specs += [pl.BlockSpec((None, None, PAST, KV_W), lambda i: (i, layer, 0, 0))] * 2
        args += [cache_k, cache_v]
    return pl.pallas_call(
        functools.partial(_attn_kernel, cached=cached),
        grid=(t // seq,), in_specs=in_specs,
        out_specs=pl.BlockSpec((seq, D), lambda i: (i, 0)),
        out_shape=jax.ShapeDtypeStruct((t, D), BF16),
        compiler_params=_cparams(1), name=name,
    )(*args)


def _merge_kernel(a_ref, o_ref, c_ref, gate_ref, x_ref, mod_ref, n2g_ref, wa_ref, wb_ref, wc_ref, wo_ref,
                  x1_ref, h2_ref):
    mix = gate_ref[:, 0:D].astype(F32) * _dot(a_ref[...], wa_ref[...])
    mix = mix + gate_ref[:, D:2 * D].astype(F32) * _dot(o_ref[...], wb_ref[...])
    mix = mix + gate_ref[:, 2 * D:3 * D].astype(F32) * _dot(c_ref[...], wc_ref[...])
    out = _dot(mix.astype(BF16), wo_ref[...])
    mod = mod_ref[...]
    g1, sh2, sc2 = mod[:, 2 * D:3 * D], mod[:, 3 * D:4 * D], mod[:, 4 * D:5 * D]
    x1 = x_ref[...] + g1 * out
    x1_ref[...] = x1
    h2_ref[...] = ((_rms(x1) * n2g_ref[...]) * (1.0 + sc2) + sh2).astype(BF16)


def _merge(a, o, c, gates, x, mod_l, n2g, wa, wb, wc, wo, *, layer, tm, row_of_tile, name):
    t = x.shape[0]
    row = lambda w: pl.BlockSpec((tm, w), lambda i: (i, 0))
    wspec = _resident((D, D), layer)
    return pl.pallas_call(
        _merge_kernel,
        grid=(t // tm,),
        in_specs=[row(D), row(D), row(D), row(3 * D), row(D),
                  pl.BlockSpec((None, 1, 6 * D), lambda i: (row_of_tile(i), 0, 0)),
                  pl.BlockSpec((1, D), lambda i: (0, 0)), wspec, wspec, wspec, wspec],
        out_specs=[row(D), row(D)],
        out_shape=[jax.ShapeDtypeStruct((t, D), F32), jax.ShapeDtypeStruct((t, D), BF16)],
        compiler_params=_cparams(1), name=name,
    )(a, o, c, gates, x, mod_l, n2g, wa, wb, wc, wo)


FF_TN = 1024
FF_SUB = 512
FF_ROWS = 1024
DOWN_TM = 1024


def _ffn_up_kernel(h_ref, wg_ref, wv_ref, cwg_ref, cwv_ref, cbg_ref, cbv_ref, out_ref, *, seq):
    h = h_ref[...]

    def conv(z, cw_ref, cb_ref, cols):
        y = cb_ref[:, cols] + _shift_prev(z, seq) * cw_ref[0:1, cols]
        y = y + z * cw_ref[1:2, cols]
        return y + _shift_next(z, seq) * cw_ref[2:3, cols]

    for c in range(FF_TN // FF_SUB):
        cols = slice(c * FF_SUB, (c + 1) * FF_SUB)
        g = conv(_dot(h, wg_ref[:, cols]), cwg_ref, cbg_ref, cols)
        v = conv(_dot(h, wv_ref[:, cols]), cwv_ref, cbv_ref, cols)
        hg = 0.5 * g
        out_ref[:, cols] = ((hg * jnp.tanh(hg) + hg) * v).astype(BF16)


def _ffn_up(h2, w_up, cw, cb, *, layer, seq, name):
    t = h2.shape[0]
    nj = D_FF // FF_TN
    return pl.pallas_call(
        functools.partial(_ffn_up_kernel, seq=seq),
        grid=(t // FF_ROWS, nj),
        in_specs=[pl.BlockSpec((FF_ROWS, D), lambda i, j: (i, 0)),
                  pl.BlockSpec((None, D, FF_TN), lambda i, j: (layer, 0, j)),
                  pl.BlockSpec((None, D, FF_TN), lambda i, j: (layer, 0, nj + j)),
                  pl.BlockSpec((3, FF_TN), lambda i, j: (0, j)),
                  pl.BlockSpec((3, FF_TN), lambda i, j: (0, nj + j)),
                  pl.BlockSpec((1, FF_TN), lambda i, j: (0, j)),
                  pl.BlockSpec((1, FF_TN), lambda i, j: (0, nj + j))],
        out_specs=pl.BlockSpec((FF_ROWS, FF_TN), lambda i, j: (i, j)),
        out_shape=jax.ShapeDtypeStruct((t, D_FF), BF16),
        compiler_params=_cparams(2), name=name,
    )(h2, w_up, w_up, cw, cw, cb, cb)


def _ffn_down_kernel(act_ref, w_ref, x1_ref, mod_ref, fg_ref, out_ref, *, final):
    g2 = mod_ref[:, 5 * D:6 * D]
    x2 = x1_ref[...] + g2 * _dot(act_ref[...], w_ref[...])
    out_ref[...] = _rms(x2) * fg_ref[...] if final else x2


def _ffn_down(act, w_down, x1, mod_l, fg, *, layer, tm, row_of_tile, final, name):
    t = x1.shape[0]
    return pl.pallas_call(
        functools.partial(_ffn_down_kernel, final=final),
        grid=(t // tm,),
        in_specs=[pl.BlockSpec((tm, D_FF), lambda i: (i, 0)),
                  _resident((D_FF, D), layer),
                  pl.BlockSpec((tm, D), lambda i: (i, 0)),
                  pl.BlockSpec((None, 1, 6 * D), lambda i: (row_of_tile(i), 0, 0)),
                  pl.BlockSpec((1, D), lambda i: (0, 0))],
        out_specs=pl.BlockSpec((tm, D), lambda i: (i, 0)),
        out_shape=jax.ShapeDtypeStruct((t, D), F32),
        compiler_params=_cparams(1), name=name,
    )(act, w_down, x1, mod_l, fg)


def _interleave(x):
    b, s, w = x.shape
    return x.reshape(b, SUB, s // SUB, w).transpose(0, 2, 1, 3).reshape(b * s, w)


def _deinterleave(y, b, s):
    return y.reshape(b, s // SUB, SUB, -1).transpose(0, 2, 1, 3).reshape(b, s, -1)


def _rope_tables():
    p = np.arange(S_LAT)
    t = (p % SUB) * (S_LAT // SUB) + p // SUB
    pos = np.stack([t // GRID_W, t % GRID_W], axis=1).astype(np.float32)
    quarter = HEAD_DIM // 4
    inv = (ROPE_BASE ** (-np.arange(quarter, dtype=np.float32) / quarter)).astype(np.float32)
    d = np.arange(LANE) % HEAD_DIM
    which = d // (HEAD_DIM // 2)
    e = d % (HEAD_DIM // 2)
    ang = pos[:, which] * inv[e % quarter][None, :]
    sign = np.where(e < quarter, -1.0, 1.0).astype(np.float32)
    return jnp.asarray(np.cos(ang).astype(np.float32)), jnp.asarray((np.sin(ang) * sign).astype(np.float32))


def _block_diag_gates(w_r, w_i):
    eye = jnp.eye(4, dtype=F32)

    def bd(w):
        w = w.reshape(2, 4, 4, LRU_BW, LRU_BW)
        w = w[:, :, :, :, None, :] * eye[None, None, :, None, :, None]
        return w.reshape(2, 4, GB, GB)

    return jnp.concatenate([bd(w_r), bd(w_i)], axis=-1).astype(BF16)


def _trunk(x_prompt, x_sample, cache_k, cache_v, state_lru, c, c_ctx, norm1_g, norm2_g, w_mod, b_mod, w_in,
           b_gate, conv_w, conv_b, w_rg, b_rg, w_ig, b_ig, lru_lambda, q_norm_g, k_norm_g, w_pool, pool_scale,
           w_br_a, w_br_b, w_br_c, w_o, w_up, ffn_conv_w, ffn_conv_b, w_down, final_norm_g,
           paths=("ctx", "lat")):
    cond = jnp.zeros((16, D), F32).at[0:N_LAT].set(c).at[N_LAT].set(c_ctx)
    mod = _modulation(cond, w_mod, b_mod)
    rope_tabs = _rope_tables()
    gmat = jnp.asarray(np.kron(np.eye(2), np.full((HEAD_DIM, HEAD_DIM), 1.0 / HEAD_DIM)), BF16)
    ck = cache_k.reshape(N_LAT, DEPTH, PAST, KV_W)
    cv = cache_v.reshape(N_LAT, DEPTH, PAST, KV_W)
    fg = final_norm_g.reshape(1, D)

    xs = {"ctx": _interleave(x_prompt), "lat": _interleave(x_sample)}
    cfg = {"ctx": dict(seq=S_CTX, tm=512, row=lambda i: N_LAT),
           "lat": dict(seq=S_LAT, tm=512, row=lambda i: i // (S_LAT // 512))}
    new_k, new_v, new_s = [], [], []
    w_in_b, wa, wb, wc, wo, w_up_b, w_down_b, wp = (
        w.astype(BF16) for w in (w_in, w_br_a, w_br_b, w_br_c, w_o, w_up, w_down, w_pool))
    for l in range(DEPTH):
        mod_l = mod[l].reshape(16, 1, 6 * D)
        wg = _block_diag_gates(0.5 * w_rg[l], 0.5 * w_ig[l])
        bg = 0.5 * jnp.stack([b_rg[l], b_ig[l]], axis=1)
        qg = jnp.tile(q_norm_g[l], 2).reshape(1, LANE)
        kg = jnp.tile(k_norm_g[l], 2).reshape(1, LANE)
        for path in paths:
            seq, tm, row = cfg[path]["seq"], cfg[path]["tm"], cfg[path]["row"]
            lat = path == "lat"
            x = xs[path]
            xr, gy, q, k, v, up, gates = _inproj(
                x, mod_l, norm1_g[l].reshape(1, D), w_in_b, b_gate[l].reshape(1, 3 * D), qg, kg, gmat,
                rope_tabs if lat else None, layer=l, tm=tm, row_of_tile=row, name=f"inproj_{path}{l}")
            h0 = state_lru[:, l] if lat else jnp.zeros((N_CTX, 2, LRU_W), F32)
            a_pre, ends = _lru(xr, gy, h0, conv_w[l], conv_b[l].reshape(1, LRU_W), wg, bg, lru_lambda[l],
                               seq=seq, name=f"lru_{path}{l}")
            c_pre = _pool(up, wp, pool_scale[l].reshape(1, D), layer=l, seq=seq, name=f"pool_{path}{l}")
            o = _attention(q, k, v, ck if lat else None, cv if lat else None, l, seq=seq,
                           name=f"attn_{path}{l}")
            x1, h2 = _merge(a_pre, o, c_pre, gates, x, mod_l, norm2_g[l].reshape(1, D), wa, wb, wc, wo,
                            layer=l, tm=tm, row_of_tile=row, name=f"merge_{path}{l}")
            act = _ffn_up(h2, w_up_b, ffn_conv_w[l], ffn_conv_b[l].reshape(1, 2 * D_FF), layer=l, seq=seq,
                          name=f"ffn_up_{path}{l}")
            dn_row = (lambda i: i) if lat else row
            xs[path] = _ffn_down(act, w_down_b, x1, mod_l, fg, layer=l, tm=DOWN_TM, row_of_tile=dn_row,
                                 final=(l == DEPTH - 1), name=f"ffn_down_{path}{l}")
            if not lat:
                new_k.append(_deinterleave(k, N_CTX, S_CTX).reshape(N_CTX, S_CTX, N_KV, HEAD_DIM))
                new_v.append(_deinterleave(v, N_CTX, S_CTX).reshape(N_CTX, S_CTX, N_KV, HEAD_DIM))
                new_s.append(jnp.stack([ends[:, SUB - 1], ends[:, SUB]], axis=1))
    return xs, new_k, new_v, new_s


def kernel(x_prompt, x_sample, cache_k, cache_v, state_lru, c, c_ctx, norm1_g, norm2_g, w_mod, b_mod, w_in,
           b_gate, conv_w, conv_b, w_rg, b_rg, w_ig, b_ig, lru_lambda, q_norm_g, k_norm_g, w_pool, pool_scale,
           w_br_a, w_br_b, w_br_c, w_o, w_up, ffn_conv_w, ffn_conv_b, w_down, final_norm_g):
    xs, new_k, new_v, new_s = _trunk(
        x_prompt, x_sample, cache_k, cache_v, state_lru, c, c_ctx, norm1_g, norm2_g, w_mod, b_mod, w_in,
        b_gate, conv_w, conv_b, w_rg, b_rg, w_ig, b_ig, lru_lambda, q_norm_g, k_norm_g, w_pool, pool_scale,
        w_br_a, w_br_b, w_br_c, w_o, w_up, ffn_conv_w, ffn_conv_b, w_down, final_norm_g)
    y_prompt = _deinterleave(xs["ctx"], N_CTX, S_CTX)
    y_sample = _deinterleave(xs["lat"], N_LAT, S_LAT)
    return (y_prompt, y_sample, jnp.stack(new_k, axis=1), jnp.stack(new_v, axis=1), jnp.stack(new_s, axis=1))
```

```python
import functools

import numpy as np
import jax
import jax.numpy as jnp
from jax import lax
from jax.experimental import pallas as pl
from jax.experimental.pallas import tpu as pltpu

F32 = jnp.float32
BF16 = jnp.bfloat16

D = 1024
DEPTH = 2
N_CTX, S_CTX = 16, 256
N_LAT, S_LAT = 8, 1024
PAST = 256
GRID_W = 64
LRU_W = 1024
LRU_BW = 64
LRU_C = 8.0
N_HEADS, N_KV, HEAD_DIM = 16, 4, 64
KV_W = N_KV * HEAD_DIM
ROPE_BASE = 10000.0
D_FF = 3 * D
EPS = 1e-6
SUB = 8
LANE = 128
ROPE_SHIFT = HEAD_DIM // 4
C_XR, C_YR, C_Q, C_K, C_V, C_UP, C_GL, C_END = 0, 1024, 2048, 3072, 3328, 3584, 4608, 7680

LOG2E = float(np.log2(np.e))
Q_PRESCALE = LOG2E * HEAD_DIM ** -0.5

VMEM_LIMIT = 56 * 1024 * 1024


def _cparams(n_axes):
    return pltpu.CompilerParams(dimension_semantics=("arbitrary",) * n_axes,
                                vmem_limit_bytes=VMEM_LIMIT)


def _resident(shape, layer):
    return pl.BlockSpec((None,) + shape, lambda *_: (layer,) + (0,) * len(shape), pipeline_mode=pl.Buffered(1))


def _dot(a, b):
    return jnp.dot(a, b, preferred_element_type=F32)


def _dot_nt(a, b):
    return lax.dot_general(a, b, (((1,), (1,)), ((), ())), preferred_element_type=F32)


def _rms(x):
    return x * lax.rsqrt(jnp.mean(x * x, axis=-1, keepdims=True) + EPS)


def _shift_prev(x, rm, n=1):
    r, w = x.shape
    sub = lax.broadcasted_iota(jnp.int32, (SUB, w), 0)
    pieces = []
    for m in range(r // rm):
        base = m * rm
        for t in range(n):
            lo = base + rm - SUB * (n - t)
            pieces.append(jnp.where(sub == 0, 0.0, pltpu.roll(x[lo:lo + SUB], 1, 0)))
        pieces.append(x[base:base + rm - SUB * n])
    return jnp.concatenate(pieces, axis=0)


def _shift_next(x, rm, n=1):
    r, w = x.shape
    sub = lax.broadcasted_iota(jnp.int32, (SUB, w), 0)
    pieces = []
    for m in range(r // rm):
        base = m * rm
        pieces.append(x[base + SUB * n:base + rm])
        for t in range(n):
            lo = base + SUB * t
            pieces.append(jnp.where(sub == SUB - 1, 0.0, pltpu.roll(x[lo:lo + SUB], SUB - 1, 0)))
    return jnp.concatenate(pieces, axis=0)


def _mod_kernel(cond_ref, w_ref, b_ref, out_ref):
    c = cond_ref[...]
    s = (c * jax.nn.sigmoid(c)).astype(BF16)
    out_ref[...] = _dot(s, w_ref[...].astype(BF16)) + b_ref[...]


def _modulation(cond, w_mod, b_mod):
    tn = 1536
    return pl.pallas_call(
        _mod_kernel,
        grid=(DEPTH, 6 * D // tn),
        in_specs=[pl.BlockSpec((16, D), lambda l, j: (0, 0)),
                  pl.BlockSpec((None, D, tn), lambda l, j: (l, 0, j)),
                  pl.BlockSpec((None, 1, tn), lambda l, j: (l, 0, j))],
        out_specs=pl.BlockSpec((None, 16, tn), lambda l, j: (l, 0, j)),
        out_shape=jax.ShapeDtypeStruct((DEPTH, 16, 6 * D), F32),
        compiler_params=_cparams(2),
        name="modulation",
    )(cond, w_mod, b_mod.reshape(DEPTH, 1, 6 * D))


def _inproj_kernel(*refs, rope):
    if rope:
        (x_ref, mod_ref, n1g_ref, w_ref, bgate_ref, qg_ref, kg_ref, gmat_ref, cos_ref, sin_ref,
         xr_ref, gy_ref, q_ref, k_ref, v_ref, up_ref, gate_ref) = refs
    else:
        (x_ref, mod_ref, n1g_ref, w_ref, bgate_ref, qg_ref, kg_ref, gmat_ref, _, _,
         xr_ref, gy_ref, q_ref, kc_ref, vc_ref, up_ref, gate_ref, stage_ref) = refs
    mod = mod_ref[...]
    sh1, sc1 = mod[:, 0:D], mod[:, D:2 * D]
    h = ((_rms(x_ref[...]) * n1g_ref[...]) * (1.0 + sc1) + sh1).astype(BF16)

    def mm(c0, c1):
        return _dot(h, w_ref[:, c0:c1])

    for g in range(3):
        z = mm(C_GL + g * D, C_GL + (g + 1) * D) + bgate_ref[:, g * D:(g + 1) * D]
        gate_ref[:, g * D:(g + 1) * D] = jax.nn.sigmoid(z).astype(BF16)
    gy_ref[...] = jax.nn.gelu(mm(C_YR, C_Q)).astype(BF16)

    lane = lax.broadcasted_iota(jnp.int32, (1, LANE), 1)
    first = (lane & ROPE_SHIFT) == 0

    def head_norm(xb, g):
        ms = _dot((xb * xb).astype(BF16), gmat_ref[...])
        y = (xb * lax.rsqrt(ms + EPS)) * g
        if rope:
            partner = jnp.where(first, pltpu.roll(y, LANE - ROPE_SHIFT, 1), pltpu.roll(y, ROPE_SHIFT, 1))
            y = y * cos_ref[...] + partner * sin_ref[...]
        return y

    qa = mm(C_Q, C_K)
    for c in range(N_HEADS * HEAD_DIM // LANE):
        qn = head_norm(qa[:, c * LANE:(c + 1) * LANE], qg_ref[...])
        q_ref[:, c * LANE:(c + 1) * LANE] = (qn * Q_PRESCALE).astype(BF16)
    kv_slabs = KV_W // LANE
    ka = mm(C_K, C_V)
    for c in range(kv_slabs):
        kn = head_norm(ka[:, c * LANE:(c + 1) * LANE], kg_ref[...])
        if rope:
            k_ref[:, c * LANE:(c + 1) * LANE] = kn
        else:
            stage_ref[c] = kn
    up_ref[...] = mm(C_UP, C_GL).astype(BF16)
    xr_ref[...] = mm(C_XR, C_YR).astype(BF16)
    va = mm(C_V, C_UP)
    if rope:
        v_ref[...] = va
    else:
        steps = S_CTX // SUB
        for c in range(kv_slabs):
            stage_ref[kv_slabs + c] = va[:, c * LANE:(c + 1) * LANE]
        for slab, dst in enumerate([kc_ref] * kv_slabs + [vc_ref] * kv_slabs):
            cols = slice((slab % kv_slabs) * LANE, (slab % kv_slabs + 1) * LANE)
            for s in range(x_ref.shape[0] // S_CTX):
                for j in range(SUB):
                    dst[s, j * steps:(j + 1) * steps, cols] = (
                        stage_ref[slab, pl.ds(s * S_CTX + j, steps, stride=SUB), :])


def _inproj(x, mod_l, n1g, w_in, b_gate, qg, kg, gmat, rope_tabs, kv_bufs, *, layer, tm, row_of_tile, name):
    t = x.shape[0]
    rope = rope_tabs is not None
    full = lambda shape: pl.BlockSpec(shape, lambda i: (0,) * len(shape))
    in_specs = [pl.BlockSpec((tm, D), lambda i: (i, 0)),
                pl.BlockSpec((None, 1, 6 * D), lambda i: (row_of_tile(i), 0, 0)),
                full((1, D)), _resident((D, C_END), layer), full((1, 3 * D)), full((1, LANE)), full((1, LANE)),
                full((LANE, LANE))]
    args = [x, mod_l, n1g, w_in, b_gate, qg, kg, gmat]
    if rope:
        per_seq = S_LAT // tm
        in_specs += [pl.BlockSpec((tm, LANE), lambda i: (i % per_seq, 0))] * 2
        args += list(rope_tabs)
    row = lambda w: pl.BlockSpec((tm, w), lambda i: (i, 0))
    act = lambda w: jax.ShapeDtypeStruct((t, w), BF16)
    if rope:
        kv_specs = [row(KV_W), row(KV_W)]
        kv_shapes = [jax.ShapeDtypeStruct((t, KV_W), F32)] * 2
        aliases, scratch = {}, []
    else:
        in_specs += [pl.BlockSpec(memory_space=pl.ANY)] * 2
        args += list(kv_bufs)
        kv_specs = [pl.BlockSpec((tm // S_CTX, None, S_CTX, KV_W), lambda i: (i, layer, 0, 0))] * 2
        kv_shapes = [jax.ShapeDtypeStruct(b.shape, b.dtype) for b in kv_bufs]
        aliases = {len(args) - 2: 3, len(args) - 1: 4}
        scratch = [pltpu.VMEM((2 * KV_W // LANE, tm, LANE), F32)]
    out_specs = [row(D), row(D), row(D)] + kv_specs + [row(D), row(3 * D)]
    out_shape = [act(D), act(D), act(D)] + kv_shapes + [act(D), act(3 * D)]
    return pl.pallas_call(
        functools.partial(_inproj_kernel, rope=rope),
        grid=(t // tm,), in_specs=in_specs, out_specs=out_specs, out_shape=out_shape,
        input_output_aliases=aliases, scratch_shapes=scratch,
        compiler_params=_cparams(1), name=name,
    )(*args)


GB = 256


def _lru_kernel(xr_ref, gy_ref, h0_ref, cw_ref, cb_ref, wg_ref, bg_ref, lam_ref,
                out_ref, ends_ref, a_s, u_s):
    r = xr_ref.shape[0]
    steps = r // SUB
    sub = lax.broadcasted_iota(jnp.int32, (SUB, GB), 0)
    for cb in range(LRU_W // GB):
        cols = slice(cb * GB, (cb + 1) * GB)
        x = xr_ref[:, cols].astype(F32)
        xn = _shift_next(x, r)
        xc = cb_ref[:, cols] + _shift_prev(x, r) * cw_ref[0:1, cols]
        xc = xc + x * cw_ref[1:2, cols]
        xc = xc + xn * cw_ref[2:3, cols]
        xc = xc + _shift_next(xn, r) * cw_ref[3:4, cols]
        lhs = xc.astype(BF16)
        xh = 0.5 * xc
        for d in range(2):
            g = _dot(lhs, wg_ref[d, cb])
            tr = jnp.tanh(g[:, :GB] + bg_ref[d, 0:1, cols])
            ti = jnp.tanh(g[:, GB:] + bg_ref[d, 1:2, cols])
            lam = lam_ref[d:d + 1, cols]
            log_sig = jnp.minimum(lam, 0.0) - jnp.log(1.0 + jnp.exp(-jnp.abs(lam)))
            ch = (0.5 * LRU_C) * log_sig
            a = jnp.exp2(tr * (ch * LOG2E) + ch * LOG2E)
            m2 = jnp.tanh(tr * (-ch) - ch) * (1.0 + a * a)
            mult = jnp.where(m2 > 0.0, m2 * lax.rsqrt(m2), 0.0)
            u = mult * (ti * xh + xh)
            a_s[d] = a
            u_s[d] = u
            e0 = 0 if d == 0 else r - SUB
            edge = sub == (0 if d == 0 else SUB - 1)
            ae = a_s[d, e0:e0 + SUB, :]
            u_s[d, e0:e0 + SUB, :] = u_s[d, e0:e0 + SUB, :] + jnp.where(edge, ae * h0_ref[d:d + 1, cols], 0.0)
            a_s[d, e0:e0 + SUB, :] = jnp.where(edge, 0.0, ae)

        def step(k, carry):
            hf, pf, hb, pb = carry
            rf = pl.multiple_of(k * SUB, SUB)
            rb = pl.multiple_of((steps - 1 - k) * SUB, SUB)
            af = a_s[0, pl.ds(rf, SUB), :]
            hf = af * hf + u_s[0, pl.ds(rf, SUB), :]
            pf = af * pf
            u_s[0, pl.ds(rf, SUB), :] = hf
            a_s[0, pl.ds(rf, SUB), :] = pf
            ab = a_s[1, pl.ds(rb, SUB), :]
            hb = ab * hb + u_s[1, pl.ds(rb, SUB), :]
            pb = ab * pb
            u_s[1, pl.ds(rb, SUB), :] = hb
            a_s[1, pl.ds(rb, SUB), :] = pb
            return hf, pf, hb, pb

        zero = jnp.zeros((SUB, GB), F32)
        one = jnp.ones((SUB, GB), F32)
        hf, pf, hb, pb = lax.fori_loop(0, steps, step, (zero, one, zero, one), unroll=8)

        ef, eb = hf, hb
        for sh in (1, 2, 4):
            keep_f = sub >= sh
            ef = ef + pf * jnp.where(keep_f, pltpu.roll(ef, sh, 0), 0.0)
            pf = pf * jnp.where(keep_f, pltpu.roll(pf, sh, 0), 1.0)
            keep_b = sub < SUB - sh
            eb = eb + pb * jnp.where(keep_b, pltpu.roll(eb, SUB - sh, 0), 0.0)
            pb = pb * jnp.where(keep_b, pltpu.roll(pb, SUB - sh, 0), 1.0)
        ends_ref[0:SUB, cols] = ef
        ends_ref[SUB:2 * SUB, cols] = eb
        cf = jnp.where(sub >= 1, pltpu.roll(ef, 1, 0), 0.0)
        cbk = jnp.where(sub < SUB - 1, pltpu.roll(eb, SUB - 1, 0), 0.0)
        cf2 = jnp.concatenate([cf, cf], axis=0)
        cb2 = jnp.concatenate([cbk, cbk], axis=0)

        def fix(m, _):
            rows = pl.ds(pl.multiple_of(m * 2 * SUB, 2 * SUB), 2 * SUB)
            hft = u_s[0, rows, :] + a_s[0, rows, :] * cf2
            hbt = u_s[1, rows, :] + a_s[1, rows, :] * cb2
            out_ref[rows, cols] = ((hft + hbt) * gy_ref[rows, cols].astype(F32)).astype(BF16)
            return 0

        lax.fori_loop(0, r // (2 * SUB), fix, 0, unroll=4)


def _lru(xr, gy, h0, conv_w, conv_b, wg, bg, lam, *, seq, name):
    t = xr.shape[0]
    nseq = t // seq
    full = lambda shape: pl.BlockSpec(shape, lambda i: (0,) * len(shape))
    return pl.pallas_call(
        _lru_kernel,
        grid=(nseq,),
        in_specs=[pl.BlockSpec((seq, D), lambda i: (i, 0)), pl.BlockSpec((seq, D), lambda i: (i, 0)),
                  pl.BlockSpec((None, 2, LRU_W), lambda i: (i, 0, 0)),
                  full((4, LRU_W)), full((1, LRU_W)), full((2, LRU_W // GB, GB, 2 * GB)),
                  full((2, 2, LRU_W)), full((2, LRU_W))],
        out_specs=[pl.BlockSpec((seq, D), lambda i: (i, 0)),
                   pl.BlockSpec((None, 2 * SUB, LRU_W), lambda i: (i, 0, 0))],
        out_shape=[jax.ShapeDtypeStruct((t, D), BF16), jax.ShapeDtypeStruct((nseq, 2 * SUB, LRU_W), F32)],
        scratch_shapes=[pltpu.VMEM((2, seq, GB), F32), pltpu.VMEM((2, seq, GB), F32)],
        compiler_params=_cparams(1), name=name,
    )(xr, gy, h0, conv_w, conv_b, wg, bg, lam)


POOL_G = 256


def _pool_kernel(up_ref, wp_ref, scale_ref, out_ref):
    r = up_ref.shape[0]
    steps = r // SUB
    row = lax.broadcasted_iota(jnp.int32, (r, 1), 0)
    t = (row & (SUB - 1)) * steps + jnp.right_shift(row, 3)
    for g in range(4):
        cols = slice(g * POOL_G, (g + 1) * POOL_G)
        x = up_ref[:, cols].astype(F32)
        half = 1 << g
        back, fwd = x, x
        for lvl in range(g):
            n = 1 << lvl
            back = back + _shift_prev(back, r, n)
            fwd = fwd + _shift_next(fwd, r, n)
        win = _shift_prev(back, r) + fwd
        cnt = jnp.minimum(t + half, r) - jnp.maximum(t - half, 0)
        d = (win / cnt.astype(F32) - x).astype(BF16)
        out_ref[:, cols] = (_dot(d, wp_ref[g]) * scale_ref[:, cols]).astype(BF16)


def _pool(up, w_pool, scale, *, layer, seq, name):
    t = up.shape[0]
    return pl.pallas_call(
        _pool_kernel,
        grid=(t // seq,),
        in_specs=[pl.BlockSpec((seq, D), lambda i: (i, 0)),
                  pl.BlockSpec((None, 4, POOL_G, POOL_G), lambda i: (layer, 0, 0, 0)),
                  pl.BlockSpec((1, D), lambda i: (0, 0))],
        out_specs=pl.BlockSpec((seq, D), lambda i: (i, 0)),
        out_shape=jax.ShapeDtypeStruct((t, D), BF16),
        compiler_params=_cparams(1), name=name,
    )(up, w_pool, scale)


ONES_ROWS = 16


def _attn_kernel(*refs, cached):
    transposed = cached
    if cached:
        q_ref, k_ref, v_ref, ck_ref, cv_ref, o_ref = refs
        kall = jnp.concatenate([ck_ref[...], k_ref[...]], axis=0)
        vall = jnp.concatenate([cv_ref[...], v_ref[...]], axis=0)
    else:
        q_ref, k_ref, v_ref, o_ref = refs
        kall = k_ref[...]
        vall = v_ref[...]
    half = HEAD_DIM
    lane = lax.broadcasted_iota(jnp.int32, (1, LANE), 1)

    def both_halves(x, own_low):
        own = jnp.where((lane < half) if own_low else (lane >= half), x, 0.0)
        oth = pltpu.roll(own, half, 1)
        lo, hi = (own, oth) if own_low else (oth, own)
        return lo.astype(BF16), hi.astype(BF16)

    if transposed:
        heads = []
        for kvh in range(N_KV):
            blk = slice((kvh // 2) * LANE, (kvh // 2 + 1) * LANE)
            ks = both_halves(kall[:, blk], kvh % 2 == 0)
            r0 = (kvh % 2) * HEAD_DIM
            v_t = vall[:, blk].T[r0:r0 + HEAD_DIM]
            v_t = jnp.concatenate([v_t, jnp.ones((ONES_ROWS, v_t.shape[1]), F32)], axis=0).astype(BF16)
            for pair in range(2):
                heads += [(kk, v_t, kvh * 4 * HEAD_DIM + pair * LANE) for kk in ks]

        def scores(h):
            kk, _, c0 = heads[h]
            return _dot_nt(kk, q_ref[:, c0:c0 + LANE])

        s_next = scores(0)
        outs = []
        for h, (_, v_t, c0) in enumerate(heads):
            s = s_next
            if h + 1 < len(heads):
                s_next = scores(h + 1)
            e = jnp.exp2(s - jnp.max(s, axis=0, keepdims=True)).astype(BF16)
            pv = _dot(v_t, e)
            outs.append(pv[:HEAD_DIM] / pv[HEAD_DIM:HEAD_DIM + 1])
            if h % 2 == 1:
                o_ref[:, c0:c0 + LANE] = jnp.concatenate(outs, axis=0).T.astype(BF16)
                outs = []
        return

    for kvh in range(N_KV):
        blk = slice((kvh // 2) * LANE, (kvh // 2 + 1) * LANE)
        ks = both_halves(kall[:, blk], kvh % 2 == 0)
        vs = both_halves(vall[:, blk], kvh % 2 == 0)
        for pair in range(2):
            c0 = kvh * 4 * HEAD_DIM + pair * LANE
            qp = q_ref[:, c0:c0 + LANE]
            acc = jnp.zeros((qp.shape[0], LANE), F32)
            for kk, vv in zip(ks, vs):
                s = _dot_nt(qp, kk)
                e = jnp.exp2(s - jnp.max(s, axis=-1, keepdims=True))
                l = jnp.sum(e, axis=-1, keepdims=True)
                acc = acc + _dot(e.astype(BF16), vv) / l
            o_ref[:, c0:c0 + LANE] = acc.astype(BF16)


def _attention(q, k, v, cache_k, cache_v, layer, *, seq, name):
    t = q.shape[0]
    cached = cache_k is not None
    if cached:
        kv_spec = pl.BlockSpec((seq, KV_W), lambda i: (i, 0))
    else:
        kv_spec = pl.BlockSpec((None, None, seq, KV_W), lambda i: (i, layer, 0, 0))
    in_specs = [pl.BlockSpec((seq, D), lambda i: (i, 0)), kv_spec, kv_spec]
    args = [q, k, v]
    if cached:
        in_specs += [pl.BlockSpec((None, None, PAST, KV_W), lambda i: (i, layer, 0, 0))] * 2
        args += [cache_k, cache_v]
    return pl.pallas_call(
        functools.partial(_attn_kernel, cached=cached),
        grid=(t // seq,), in_specs=in_specs,
        out_specs=pl.BlockSpec((seq, D), lambda i: (i, 0)),
        out_shape=jax.ShapeDtypeStruct((t, D), BF16),
        compiler_params=_cparams(1), name=name,
    )(*args)


def _merge_kernel(a_ref, o_ref, c_ref, gate_ref, x_ref, mod_ref, n2g_ref, wa_ref, wb_ref, wc_ref, wo_ref,
                  x1_ref, h2_ref):
    mix = gate_ref[:, 0:D].astype(F32) * _dot(a_ref[...], wa_ref[...])
    mix = mix + gate_ref[:, D:2 * D].astype(F32) * _dot(o_ref[...], wb_ref[...])
    mix = mix + gate_ref[:, 2 * D:3 * D].astype(F32) * _dot(c_ref[...], wc_ref[...])
    out = _dot(mix.astype(BF16), wo_ref[...])
    mod = mod_ref[...]
    g1, sh2, sc2 = mod[:, 2 * D:3 * D], mod[:, 3 * D:4 * D], mod[:, 4 * D:5 * D]
    x1 = x_ref[...] + g1 * out
    x1_ref[...] = x1
    h2_ref[...] = ((_rms(x1) * n2g_ref[...]) * (1.0 + sc2) + sh2).astype(BF16)


def _merge(a, o, c, gates, x, mod_l, n2g, wa, wb, wc, wo, *, layer, tm, row_of_tile, name):
    t = x.shape[0]
    row = lambda w: pl.BlockSpec((tm, w), lambda i: (i, 0))
    wspec = _resident((D, D), layer)
    return pl.pallas_call(
        _merge_kernel,
        grid=(t // tm,),
        in_specs=[row(D), row(D), row(D), row(3 * D), row(D),
                  pl.BlockSpec((None, 1, 6 * D), lambda i: (row_of_tile(i), 0, 0)),
                  pl.BlockSpec((1, D), lambda i: (0, 0)), wspec, wspec, wspec, wspec],
        out_specs=[row(D), row(D)],
        out_shape=[jax.ShapeDtypeStruct((t, D), F32), jax.ShapeDtypeStruct((t, D), BF16)],
        compiler_params=_cparams(1), name=name,
    )(a, o, c, gates, x, mod_l, n2g, wa, wb, wc, wo)


FF_TN = 1024
FF_SUB = 512
FF_ROWS = 1024
DOWN_TM = 1024


def _ffn_up_kernel(h_ref, wg_ref, wv_ref, cwg_ref, cwv_ref, cbg_ref, cbv_ref, out_ref, *, seq):
    h = h_ref[...]

    def conv(z, cw_ref, cb_ref, cols):
        y = cb_ref[:, cols] + _shift_prev(z, seq) * cw_ref[0:1, cols]
        y = y + z * cw_ref[1:2, cols]
        return y + _shift_next(z, seq) * cw_ref[2:3, cols]

    for c in range(FF_TN // FF_SUB):
        cols = slice(c * FF_SUB, (c + 1) * FF_SUB)
        g = conv(_dot(h, wg_ref[:, cols]), cwg_ref, cbg_ref, cols)
        v = conv(_dot(h, wv_ref[:, cols]), cwv_ref, cbv_ref, cols)
        hg = 0.5 * g
        out_ref[:, cols] = ((hg * jnp.tanh(hg) + hg) * v).astype(BF16)


def _ffn_up(h2, w_up, cw, cb, *, layer, seq, name):
    t = h2.shape[0]
    nj = D_FF // FF_TN
    return pl.pallas_call(
        functools.partial(_ffn_up_kernel, seq=seq),
        grid=(t // FF_ROWS, nj),
        in_specs=[pl.BlockSpec((FF_ROWS, D), lambda i, j: (i, 0)),
                  pl.BlockSpec((None, D, FF_TN), lambda i, j: (layer, 0, j)),
                  pl.BlockSpec((None, D, FF_TN), lambda i, j: (layer, 0, nj + j)),
                  pl.BlockSpec((3, FF_TN), lambda i, j: (0, j)),
                  pl.BlockSpec((3, FF_TN), lambda i, j: (0, nj + j)),
                  pl.BlockSpec((1, FF_TN), lambda i, j: (0, j)),
                  pl.BlockSpec((1, FF_TN), lambda i, j: (0, nj + j))],
        out_specs=pl.BlockSpec((FF_ROWS, FF_TN), lambda i, j: (i, j)),
        out_shape=jax.ShapeDtypeStruct((t, D_FF), BF16),
        compiler_params=_cparams(2), name=name,
    )(h2, w_up, w_up, cw, cw, cb, cb)


def _ffn_down_kernel(act_ref, w_ref, x1_ref, mod_ref, fg_ref, out_ref, *, final):
    g2 = mod_ref[:, 5 * D:6 * D]
    x2 = x1_ref[...] + g2 * _dot(act_ref[...], w_ref[...])
    out_ref[...] = _rms(x2) * fg_ref[...] if final else x2


def _ffn_down(act, w_down, x1, mod_l, fg, *, layer, tm, row_of_tile, final, name):
    t = x1.shape[0]
    return pl.pallas_call(
        functools.partial(_ffn_down_kernel, final=final),
        grid=(t // tm,),
        in_specs=[pl.BlockSpec((tm, D_FF), lambda i: (i, 0)),
                  _resident((D_FF, D), layer),
                  pl.BlockSpec((tm, D), lambda i: (i, 0)),
                  pl.BlockSpec((None, 1, 6 * D), lambda i: (row_of_tile(i), 0, 0)),
                  pl.BlockSpec((1, D), lambda i: (0, 0))],
        out_specs=pl.BlockSpec((tm, D), lambda i: (i, 0)),
        out_shape=jax.ShapeDtypeStruct((t, D), F32),
        compiler_params=_cparams(1), name=name,
    )(act, w_down, x1, mod_l, fg)


def _interleave(x):
    b, s, w = x.shape
    return x.reshape(b, SUB, s // SUB, w).transpose(0, 2, 1, 3).reshape(b * s, w)


def _deinterleave(y, b, s):
    return y.reshape(b, s // SUB, SUB, -1).transpose(0, 2, 1, 3).reshape(b, s, -1)


def _rope_tables():
    p = np.arange(S_LAT)
    t = (p % SUB) * (S_LAT // SUB) + p // SUB
    pos = np.stack([t // GRID_W, t % GRID_W], axis=1).astype(np.float32)
    quarter = HEAD_DIM // 4
    inv = (ROPE_BASE ** (-np.arange(quarter, dtype=np.float32) / quarter)).astype(np.float32)
    d = np.arange(LANE) % HEAD_DIM
    which = d // (HEAD_DIM // 2)
    e = d % (HEAD_DIM // 2)
    ang = pos[:, which] * inv[e % quarter][None, :]
    sign = np.where(e < quarter, -1.0, 1.0).astype(np.float32)
    return jnp.asarray(np.cos(ang).astype(np.float32)), jnp.asarray((np.sin(ang) * sign).astype(np.float32))


def _block_diag_gates(w_r, w_i):
    eye = jnp.eye(4, dtype=F32)

    def bd(w):
        w = w.reshape(2, 4, 4, LRU_BW, LRU_BW)
        w = w[:, :, :, :, None, :] * eye[None, None, :, None, :, None]
        return w.reshape(2, 4, GB, GB)

    return jnp.concatenate([bd(w_r), bd(w_i)], axis=-1).astype(BF16)


def _trunk(x_prompt, x_sample, cache_k, cache_v, state_lru, c, c_ctx, norm1_g, norm2_g, w_mod, b_mod, w_in,
           b_gate, conv_w, conv_b, w_rg, b_rg, w_ig, b_ig, lru_lambda, q_norm_g, k_norm_g, w_pool, pool_scale,
           w_br_a, w_br_b, w_br_c, w_o, w_up, ffn_conv_w, ffn_conv_b, w_down, final_norm_g,
           paths=("ctx", "lat")):
    cond = jnp.zeros((16, D), F32).at[0:N_LAT].set(c).at[N_LAT].set(c_ctx)
    mod = _modulation(cond, w_mod, b_mod)
    rope_tabs = _rope_tables()
    gmat = jnp.asarray(np.kron(np.eye(2), np.full((HEAD_DIM, HEAD_DIM), 1.0 / HEAD_DIM)), BF16)
    ck = cache_k.reshape(N_LAT, DEPTH, PAST, KV_W)
    cv = cache_v.reshape(N_LAT, DEPTH, PAST, KV_W)
    fg = final_norm_g.reshape(1, D)

    xs = {"ctx": _interleave(x_prompt), "lat": _interleave(x_sample)}
    cfg = {"ctx": dict(seq=S_CTX, tm=512, row=lambda i: N_LAT),
           "lat": dict(seq=S_LAT, tm=512, row=lambda i: i // (S_LAT // 512))}
    new_s = []
    kv_bufs = (jnp.zeros((N_CTX, DEPTH, S_CTX, KV_W), F32), jnp.zeros((N_CTX, DEPTH, S_CTX, KV_W), F32))
    w_in_b, wa, wb, wc, wo, w_up_b, w_down_b, wp = (
        w.astype(BF16) for w in (w_in, w_br_a, w_br_b, w_br_c, w_o, w_up, w_down, w_pool))
    for l in range(DEPTH):
        mod_l = mod[l].reshape(16, 1, 6 * D)
        wg = _block_diag_gates(0.5 * w_rg[l], 0.5 * w_ig[l])
        bg = 0.5 * jnp.stack([b_rg[l], b_ig[l]], axis=1)
        qg = jnp.tile(q_norm_g[l], 2).reshape(1, LANE)
        kg = jnp.tile(k_norm_g[l], 2).reshape(1, LANE)
        for path in paths:
            seq, tm, row = cfg[path]["seq"], cfg[path]["tm"], cfg[path]["row"]
            lat = path == "lat"
            x = xs[path]
            xr, gy, q, k, v, up, gates = _inproj(
                x, mod_l, norm1_g[l].reshape(1, D), w_in_b, b_gate[l].reshape(1, 3 * D), qg, kg, gmat,
                rope_tabs if lat else None, None if lat else kv_bufs, layer=l, tm=tm, row_of_tile=row,
                name=f"inproj_{path}{l}")
            if not lat:
                kv_bufs = (k, v)
            h0 = state_lru[:, l] if lat else jnp.zeros((N_CTX, 2, LRU_W), F32)
            a_pre, ends = _lru(xr, gy, h0, conv_w[l], conv_b[l].reshape(1, LRU_W), wg, bg, lru_lambda[l],
                               seq=seq, name=f"lru_{path}{l}")
            c_pre = _pool(up, wp, pool_scale[l].reshape(1, D), layer=l, seq=seq, name=f"pool_{path}{l}")
            o = _attention(q, k, v, ck if lat else None, cv if lat else None, l, seq=seq,
                           name=f"attn_{path}{l}")
            x1, h2 = _merge(a_pre, o, c_pre, gates, x, mod_l, norm2_g[l].reshape(1, D), wa, wb, wc, wo,
                            layer=l, tm=tm, row_of_tile=row, name=f"merge_{path}{l}")
            act = _ffn_up(h2, w_up_b, ffn_conv_w[l], ffn_conv_b[l].reshape(1, 2 * D_FF), layer=l, seq=seq,
                          name=f"ffn_up_{path}{l}")
            dn_row = (lambda i: i) if lat else row
            xs[path] = _ffn_down(act, w_down_b, x1, mod_l, fg, layer=l, tm=DOWN_TM, row_of_tile=dn_row,
                                 final=(l == DEPTH - 1), name=f"ffn_down_{path}{l}")
            if not lat:
                new_s.append(jnp.stack([ends[:, SUB - 1], ends[:, SUB]], axis=1))
    new_k, new_v = (b.reshape(N_CTX, DEPTH, S_CTX, N_KV, HEAD_DIM) for b in kv_bufs)
    return xs, new_k, new_v, new_s


def kernel(x_prompt, x_sample, cache_k, cache_v, state_lru, c, c_ctx, norm1_g, norm2_g, w_mod, b_mod, w_in,
           b_gate, conv_w, conv_b, w_rg, b_rg, w_ig, b_ig, lru_lambda, q_norm_g, k_norm_g, w_pool, pool_scale,
           w_br_a, w_br_b, w_br_c, w_o, w_up, ffn_conv_w, ffn_conv_b, w_down, final_norm_g):
    xs, new_k, new_v, new_s = _trunk(
        x_prompt, x_sample, cache_k, cache_v, state_lru, c, c_ctx, norm1_g, norm2_g, w_mod, b_mod, w_in,
        b_gate, conv_w, conv_b, w_rg, b_rg, w_ig, b_ig, lru_lambda, q_norm_g, k_norm_g, w_pool, pool_scale,
        w_br_a, w_br_b, w_br_c, w_o, w_up, ffn_conv_w, ffn_conv_b, w_down, final_norm_g)
    y_prompt = _deinterleave(xs["ctx"], N_CTX, S_CTX)
    y_sample = _deinterleave(xs["lat"], N_LAT, S_LAT)
    return (y_prompt, y_sample, new_k, new_v, jnp.stack(new_s, axis=1))
```

```python
import functools

import numpy as np
import jax
import jax.numpy as jnp
from jax import lax
from jax.experimental import pallas as pl
from jax.experimental.pallas import tpu as pltpu

F32 = jnp.float32
BF16 = jnp.bfloat16

D = 1024
DEPTH = 2
N_CTX, S_CTX = 16, 256
N_LAT, S_LAT = 8, 1024
PAST = 256
GRID_W = 64
LRU_W = 1024
LRU_BW = 64
LRU_C = 8.0
N_HEADS, N_KV, HEAD_DIM = 16, 4, 64
KV_W = N_KV * HEAD_DIM
ROPE_BASE = 10000.0
D_FF = 3 * D
EPS = 1e-6
SUB = 8
LANE = 128
ROPE_SHIFT = HEAD_DIM // 4
C_XR, C_YR, C_Q, C_K, C_V, C_UP, C_GL, C_END = 0, 1024, 2048, 3072, 3328, 3584, 4608, 7680

LOG2E = float(np.log2(np.e))
Q_PRESCALE = LOG2E * HEAD_DIM ** -0.5

VMEM_LIMIT = 56 * 1024 * 1024


def _cparams(n_axes):
    return pltpu.CompilerParams(dimension_semantics=("arbitrary",) * n_axes,
                                vmem_limit_bytes=VMEM_LIMIT)


def _resident(shape, layer):
    return pl.BlockSpec((None,) + shape, lambda *_: (layer,) + (0,) * len(shape), pipeline_mode=pl.Buffered(1))


def _dot(a, b):
    return jnp.dot(a, b, preferred_element_type=F32)


def _dot_nt(a, b):
    return lax.dot_general(a, b, (((1,), (1,)), ((), ())), preferred_element_type=F32)


def _rms(x):
    return x * lax.rsqrt(jnp.mean(x * x, axis=-1, keepdims=True) + EPS)


def _shift_prev(x, rm, n=1):
    r, w = x.shape
    sub = lax.broadcasted_iota(jnp.int32, (SUB, w), 0)
    pieces = []
    for m in range(r // rm):
        base = m * rm
        for t in range(n):
            lo = base + rm - SUB * (n - t)
            pieces.append(jnp.where(sub == 0, 0.0, pltpu.roll(x[lo:lo + SUB], 1, 0)))
        pieces.append(x[base:base + rm - SUB * n])
    return jnp.concatenate(pieces, axis=0)


def _shift_next(x, rm, n=1):
    r, w = x.shape
    sub = lax.broadcasted_iota(jnp.int32, (SUB, w), 0)
    pieces = []
    for m in range(r // rm):
        base = m * rm
        pieces.append(x[base + SUB * n:base + rm])
        for t in range(n):
            lo = base + SUB * t
            pieces.append(jnp.where(sub == SUB - 1, 0.0, pltpu.roll(x[lo:lo + SUB], SUB - 1, 0)))
    return jnp.concatenate(pieces, axis=0)


def _mod_kernel(cond_ref, w_ref, b_ref, out_ref):
    c = cond_ref[...]
    s = (c * jax.nn.sigmoid(c)).astype(BF16)
    out_ref[...] = _dot(s, w_ref[...].astype(BF16)) + b_ref[...]


def _modulation(cond, w_mod, b_mod):
    tn = 1536
    return pl.pallas_call(
        _mod_kernel,
        grid=(DEPTH, 6 * D // tn),
        in_specs=[pl.BlockSpec((16, D), lambda l, j: (0, 0)),
                  pl.BlockSpec((None, D, tn), lambda l, j: (l, 0, j)),
                  pl.BlockSpec((None, 1, tn), lambda l, j: (l, 0, j))],
        out_specs=pl.BlockSpec((None, 16, tn), lambda l, j: (l, 0, j)),
        out_shape=jax.ShapeDtypeStruct((DEPTH, 16, 6 * D), F32),
        compiler_params=_cparams(2),
        name="modulation",
    )(cond, w_mod, b_mod.reshape(DEPTH, 1, 6 * D))


def _inproj_kernel(*refs, rope, layer):
    if rope:
        (x_ref, mod_ref, n1g_ref, w_ref, bgate_ref, qg_ref, kg_ref, gmat_ref, cos_ref, sin_ref,
         xr_ref, gy_ref, q_ref, k_ref, v_ref, up_ref, gate_ref) = refs
    else:
        if layer == 0:
            (x_ref, mod_ref, n1g_ref, w_ref, bgate_ref, qg_ref, kg_ref, gmat_ref,
             xr_ref, gy_ref, q_ref, kc_ref, vc_ref, up_ref, gate_ref, stage_ref) = refs
            for dst in (kc_ref, vc_ref):
                for other in range(1, DEPTH):
                    dst[:, other] = jnp.zeros((dst.shape[0],) + dst.shape[2:], F32)
            kc_ref, vc_ref = kc_ref.at[:, 0], vc_ref.at[:, 0]
        else:
            (x_ref, mod_ref, n1g_ref, w_ref, bgate_ref, qg_ref, kg_ref, gmat_ref, _, _,
             xr_ref, gy_ref, q_ref, kc_ref, vc_ref, up_ref, gate_ref, stage_ref) = refs
    mod = mod_ref[...]
    sh1, sc1 = mod[:, 0:D], mod[:, D:2 * D]
    h = ((_rms(x_ref[...]) * n1g_ref[...]) * (1.0 + sc1) + sh1).astype(BF16)

    def mm(c0, c1):
        return _dot(h, w_ref[:, c0:c1])

    for g in range(3):
        z = mm(C_GL + g * D, C_GL + (g + 1) * D) + bgate_ref[:, g * D:(g + 1) * D]
        gate_ref[:, g * D:(g + 1) * D] = jax.nn.sigmoid(z).astype(BF16)
    gy_ref[...] = jax.nn.gelu(mm(C_YR, C_Q)).astype(BF16)

    lane = lax.broadcasted_iota(jnp.int32, (1, LANE), 1)
    first = (lane & ROPE_SHIFT) == 0

    def head_norm(xb, g):
        ms = _dot((xb * xb).astype(BF16), gmat_ref[...])
        y = (xb * lax.rsqrt(ms + EPS)) * g
        if rope:
            partner = jnp.where(first, pltpu.roll(y, LANE - ROPE_SHIFT, 1), pltpu.roll(y, ROPE_SHIFT, 1))
            y = y * cos_ref[...] + partner * sin_ref[...]
        return y

    qa = mm(C_Q, C_K)
    for c in range(N_HEADS * HEAD_DIM // LANE):
        qn = head_norm(qa[:, c * LANE:(c + 1) * LANE], qg_ref[...])
        q_ref[:, c * LANE:(c + 1) * LANE] = (qn * Q_PRESCALE).astype(BF16)
    kv_slabs = KV_W // LANE
    ka = mm(C_K, C_V)
    for c in range(kv_slabs):
        kn = head_norm(ka[:, c * LANE:(c + 1) * LANE], kg_ref[...])
        if rope:
            k_ref[:, c * LANE:(c + 1) * LANE] = kn
        else:
            stage_ref[c] = kn
    up_ref[...] = mm(C_UP, C_GL).astype(BF16)
    xr_ref[...] = mm(C_XR, C_YR).astype(BF16)
    va = mm(C_V, C_UP)
    if rope:
        v_ref[...] = va
    else:
        steps = S_CTX // SUB
        for c in range(kv_slabs):
            stage_ref[kv_slabs + c] = va[:, c * LANE:(c + 1) * LANE]
        for slab, dst in enumerate([kc_ref] * kv_slabs + [vc_ref] * kv_slabs):
            cols = slice((slab % kv_slabs) * LANE, (slab % kv_slabs + 1) * LANE)
            for s in range(x_ref.shape[0] // S_CTX):
                for j in range(SUB):
                    dst[s, j * steps:(j + 1) * steps, cols] = (
                        stage_ref[slab, pl.ds(s * S_CTX + j, steps, stride=SUB), :])


def _inproj(x, mod_l, n1g, w_in, b_gate, qg, kg, gmat, rope_tabs, kv_bufs, *, layer, tm, row_of_tile, name):
    t = x.shape[0]
    rope = rope_tabs is not None
    full = lambda shape: pl.BlockSpec(shape, lambda i: (0,) * len(shape))
    in_specs = [pl.BlockSpec((tm, D), lambda i: (i, 0)),
                pl.BlockSpec((None, 1, 6 * D), lambda i: (row_of_tile(i), 0, 0)),
                full((1, D)), _resident((D, C_END), layer), full((1, 3 * D)), full((1, LANE)), full((1, LANE)),
                full((LANE, LANE))]
    args = [x, mod_l, n1g, w_in, b_gate, qg, kg, gmat]
    if rope:
        per_seq = S_LAT // tm
        in_specs += [pl.BlockSpec((tm, LANE), lambda i: (i % per_seq, 0))] * 2
        args += list(rope_tabs)
    row = lambda w: pl.BlockSpec((tm, w), lambda i: (i, 0))
    act = lambda w: jax.ShapeDtypeStruct((t, w), BF16)
    if rope:
        kv_specs = [row(KV_W), row(KV_W)]
        kv_shapes = [jax.ShapeDtypeStruct((t, KV_W), F32)] * 2
        aliases, scratch = {}, []
    else:
        kv_shapes = [jax.ShapeDtypeStruct((N_CTX, DEPTH, S_CTX, KV_W), F32)] * 2
        if layer == 0:
            kv_specs = [pl.BlockSpec((tm // S_CTX, DEPTH, S_CTX, KV_W), lambda i: (i, 0, 0, 0))] * 2
            aliases = {}
        else:
            in_specs += [pl.BlockSpec(memory_space=pl.ANY)] * 2
            args += list(kv_bufs)
            kv_specs = [pl.BlockSpec((tm // S_CTX, None, S_CTX, KV_W), lambda i: (i, layer, 0, 0))] * 2
            aliases = {len(args) - 2: 3, len(args) - 1: 4}
        scratch = [pltpu.VMEM((2 * KV_W // LANE, tm, LANE), F32)]
    out_specs = [row(D), row(D), row(D)] + kv_specs + [row(D), row(3 * D)]
    out_shape = [act(D), act(D), act(D)] + kv_shapes + [act(D), act(3 * D)]
    return pl.pallas_call(
        functools.partial(_inproj_kernel, rope=rope, layer=layer),
        grid=(t // tm,), in_specs=in_specs, out_specs=out_specs, out_shape=out_shape,
        input_output_aliases=aliases, scratch_shapes=scratch,
        compiler_params=_cparams(1), name=name,
    )(*args)


GB = 256


def _lru_kernel(xr_ref, gy_ref, h0_ref, cw_ref, cb_ref, wg_ref, bg_ref, lam_ref,
                out_ref, ends_ref, a_s, u_s, xs_ref, xc_s):
    r = xr_ref.shape[0]
    steps = r // SUB
    sub = lax.broadcasted_iota(jnp.int32, (SUB, GB), 0)
    for cb in range(LRU_W // GB):
        cols = slice(cb * GB, (cb + 1) * GB)
        x = xr_ref[:, cols].astype(F32)
        xs_ref[SUB:r + SUB, :] = x
        xs_ref[0:SUB, :] = jnp.where(sub == 0, 0.0, pltpu.roll(x[r - SUB:r], 1, 0))
        for t in range(2):
            xs_ref[r + (1 + t) * SUB:r + (2 + t) * SUB, :] = jnp.where(
                sub == SUB - 1, 0.0, pltpu.roll(x[t * SUB:(t + 1) * SUB], SUB - 1, 0))
        xc = cb_ref[:, cols] + xs_ref[0:r, :] * cw_ref[0:1, cols]
        for t in range(1, 4):
            xc = xc + xs_ref[t * SUB:r + t * SUB, :] * cw_ref[t:t + 1, cols]
        xc_s[...] = xc
        lhs = xc_s[...].astype(BF16)
        xh = 0.5 * xc_s[...]
        for d in range(2):
            g = _dot(lhs, wg_ref[d, cb])
            tr = jnp.tanh(g[:, :GB] + bg_ref[d, 0:1, cols])
            ti = jnp.tanh(g[:, GB:] + bg_ref[d, 1:2, cols])
            lam = lam_ref[d:d + 1, cols]
            log_sig = jnp.minimum(lam, 0.0) - jnp.log(1.0 + jnp.exp(-jnp.abs(lam)))
            ch = (0.5 * LRU_C) * log_sig
            a = jnp.exp2(tr * (ch * LOG2E) + ch * LOG2E)
            m2 = jnp.tanh(tr * (-ch) - ch) * (1.0 + a * a)
            mult = jnp.where(m2 > 0.0, m2 * lax.rsqrt(m2), 0.0)
            u = mult * (ti * xh + xh)
            a_s[d] = a
            u_s[d] = u
            e0 = 0 if d == 0 else r - SUB
            edge = sub == (0 if d == 0 else SUB - 1)
            ae = a_s[d, e0:e0 + SUB, :]
            u_s[d, e0:e0 + SUB, :] = u_s[d, e0:e0 + SUB, :] + jnp.where(edge, ae * h0_ref[d:d + 1, cols], 0.0)
            a_s[d, e0:e0 + SUB, :] = jnp.where(edge, 0.0, ae)

        def step(k, carry):
            hf, pf, hb, pb = carry
            rf = pl.multiple_of(k * SUB, SUB)
            rb = pl.multiple_of((steps - 1 - k) * SUB, SUB)
            af = a_s[0, pl.ds(rf, SUB), :]
            hf = af * hf + u_s[0, pl.ds(rf, SUB), :]
            pf = af * pf
            u_s[0, pl.ds(rf, SUB), :] = hf
            a_s[0, pl.ds(rf, SUB), :] = pf
            ab = a_s[1, pl.ds(rb, SUB), :]
            hb = ab * hb + u_s[1, pl.ds(rb, SUB), :]
            pb = ab * pb
            u_s[1, pl.ds(rb, SUB), :] = hb
            a_s[1, pl.ds(rb, SUB), :] = pb
            return hf, pf, hb, pb

        zero = jnp.zeros((SUB, GB), F32)
        one = jnp.ones((SUB, GB), F32)
        hf, pf, hb, pb = lax.fori_loop(0, steps, step, (zero, one, zero, one), unroll=8)

        ef, eb = hf, hb
        for sh in (1, 2, 4):
            keep_f = sub >= sh
            ef = ef + pf * jnp.where(keep_f, pltpu.roll(ef, sh, 0), 0.0)
            pf = pf * jnp.where(keep_f, pltpu.roll(pf, sh, 0), 1.0)
            keep_b = sub < SUB - sh
            eb = eb + pb * jnp.where(keep_b, pltpu.roll(eb, SUB - sh, 0), 0.0)
            pb = pb * jnp.where(keep_b, pltpu.roll(pb, SUB - sh, 0), 1.0)
        ends_ref[0:SUB, cols] = ef
        ends_ref[SUB:2 * SUB, cols] = eb
        cf = jnp.where(sub >= 1, pltpu.roll(ef, 1, 0), 0.0)
        cbk = jnp.where(sub < SUB - 1, pltpu.roll(eb, SUB - 1, 0), 0.0)
        cf2 = jnp.concatenate([cf, cf], axis=0)
        cb2 = jnp.concatenate([cbk, cbk], axis=0)

        def fix(m, _):
            rows = pl.ds(pl.multiple_of(m * 2 * SUB, 2 * SUB), 2 * SUB)
            hft = u_s[0, rows, :] + a_s[0, rows, :] * cf2
            hbt = u_s[1, rows, :] + a_s[1, rows, :] * cb2
            out_ref[rows, cols] = ((hft + hbt) * gy_ref[rows, cols].astype(F32)).astype(BF16)
            return 0

        lax.fori_loop(0, r // (2 * SUB), fix, 0, unroll=4)


def _lru(xr, gy, h0, conv_w, conv_b, wg, bg, lam, *, seq, name):
    t = xr.shape[0]
    nseq = t // seq
    full = lambda shape: pl.BlockSpec(shape, lambda i: (0,) * len(shape))
    return pl.pallas_call(
        _lru_kernel,
        grid=(nseq,),
        in_specs=[pl.BlockSpec((seq, D), lambda i: (i, 0)), pl.BlockSpec((seq, D), lambda i: (i, 0)),
                  pl.BlockSpec((None, 2, LRU_W), lambda i: (i, 0, 0)),
                  full((4, LRU_W)), full((1, LRU_W)), full((2, LRU_W // GB, GB, 2 * GB)),
                  full((2, 2, LRU_W)), full((2, LRU_W))],
        out_specs=[pl.BlockSpec((seq, D), lambda i: (i, 0)),
                   pl.BlockSpec((None, 2 * SUB, LRU_W), lambda i: (i, 0, 0))],
        out_shape=[jax.ShapeDtypeStruct((t, D), BF16), jax.ShapeDtypeStruct((nseq, 2 * SUB, LRU_W), F32)],
        scratch_shapes=[pltpu.VMEM((2, seq, GB), F32), pltpu.VMEM((2, seq, GB), F32),
                        pltpu.VMEM((seq + 3 * SUB, GB), F32), pltpu.VMEM((seq, GB), F32)],
        compiler_params=_cparams(1), name=name,
    )(xr, gy, h0, conv_w, conv_b, wg, bg, lam)


POOL_G = 256


def _pool_kernel(up_ref, wp_ref, scale_ref, out_ref):
    r = up_ref.shape[0]
    steps = r // SUB
    row = lax.broadcasted_iota(jnp.int32, (r, 1), 0)
    t = (row & (SUB - 1)) * steps + jnp.right_shift(row, 3)
    for g in range(4):
        cols = slice(g * POOL_G, (g + 1) * POOL_G)
        x = up_ref[:, cols].astype(F32)
        half = 1 << g
        back, fwd = x, x
        for lvl in range(g):
            n = 1 << lvl
            back = back + _shift_prev(back, r, n)
            fwd = fwd + _shift_next(fwd, r, n)
        win = _shift_prev(back, r) + fwd
        cnt = jnp.minimum(t + half, r) - jnp.maximum(t - half, 0)
        d = (win / cnt.astype(F32) - x).astype(BF16)
        out_ref[:, cols] = (_dot(d, wp_ref[g]) * scale_ref[:, cols]).astype(BF16)


def _pool(up, w_pool, scale, *, layer, seq, name):
    t = up.shape[0]
    return pl.pallas_call(
        _pool_kernel,
        grid=(t // seq,),
        in_specs=[pl.BlockSpec((seq, D), lambda i: (i, 0)),
                  pl.BlockSpec((None, 4, POOL_G, POOL_G), lambda i: (layer, 0, 0, 0)),
                  pl.BlockSpec((1, D), lambda i: (0, 0))],
        out_specs=pl.BlockSpec((seq, D), lambda i: (i, 0)),
        out_shape=jax.ShapeDtypeStruct((t, D), BF16),
        compiler_params=_cparams(1), name=name,
    )(up, w_pool, scale)


ONES_ROWS = 16


def _attn_kernel(*refs, cached):
    transposed = cached
    if cached:
        q_ref, k_ref, v_ref, ck_ref, cv_ref, o_ref = refs
        kall = jnp.concatenate([ck_ref[...], k_ref[...]], axis=0)
        vall = jnp.concatenate([cv_ref[...], v_ref[...]], axis=0)
    else:
        q_ref, k_ref, v_ref, o_ref = refs
        kall = k_ref[...]
        vall = v_ref[...]
    half = HEAD_DIM
    lane = lax.broadcasted_iota(jnp.int32, (1, LANE), 1)

    def both_halves(x, own_low):
        own = jnp.where((lane < half) if own_low else (lane >= half), x, 0.0)
        oth = pltpu.roll(own, half, 1)
        lo, hi = (own, oth) if own_low else (oth, own)
        return lo.astype(BF16), hi.astype(BF16)

    if transposed:
        heads = []
        for kvh in range(N_KV):
            blk = slice((kvh // 2) * LANE, (kvh // 2 + 1) * LANE)
            ks = both_halves(kall[:, blk], kvh % 2 == 0)
            r0 = (kvh % 2) * HEAD_DIM
            v_t = vall[:, blk].T[r0:r0 + HEAD_DIM]
            v_t = jnp.concatenate([v_t, jnp.ones((ONES_ROWS, v_t.shape[1]), F32)], axis=0).astype(BF16)
            for pair in range(2):
                heads += [(kk, v_t, kvh * 4 * HEAD_DIM + pair * LANE) for kk in ks]

        def scores(h):
            kk, _, c0 = heads[h]
            return _dot_nt(kk, q_ref[:, c0:c0 + LANE])

        s_next = scores(0)
        outs = []
        for h, (_, v_t, c0) in enumerate(heads):
            s = s_next
            if h + 1 < len(heads):
                s_next = scores(h + 1)
            e = jnp.exp2(s - jnp.max(s, axis=0, keepdims=True)).astype(BF16)
            pv = _dot(v_t, e)
            outs.append(pv[:HEAD_DIM] / pv[HEAD_DIM:HEAD_DIM + 1])
            if h % 2 == 1:
                o_ref[:, c0:c0 + LANE] = jnp.concatenate(outs, axis=0).T.astype(BF16)
                outs = []
        return

    for kvh in range(N_KV):
        blk = slice((kvh // 2) * LANE, (kvh // 2 + 1) * LANE)
        ks = both_halves(kall[:, blk], kvh % 2 == 0)
        vs = both_halves(vall[:, blk], kvh % 2 == 0)
        for pair in range(2):
            c0 = kvh * 4 * HEAD_DIM + pair * LANE
            qp = q_ref[:, c0:c0 + LANE]
            acc = jnp.zeros((qp.shape[0], LANE), F32)
            for kk, vv in zip(ks, vs):
                s = _dot_nt(qp, kk)
                e = jnp.exp2(s - jnp.max(s, axis=-1, keepdims=True))
                l = jnp.sum(e, axis=-1, keepdims=True)
                acc = acc + _dot(e.astype(BF16), vv) / l
            o_ref[:, c0:c0 + LANE] = acc.astype(BF16)


def _attention(q, k, v, cache_k, cache_v, layer, *, seq, name):
    t = q.shape[0]
    cached = cache_k is not None
    if cached:
        kv_spec = pl.BlockSpec((seq, KV_W), lambda i: (i, 0))
    else:
        kv_spec = pl.BlockSpec((None, None, seq, KV_W), lambda i: (i, layer, 0, 0))
    in_specs = [pl.BlockSpec((seq, D), lambda i: (i, 0)), kv_spec, kv_spec]
    args = [q, k, v]
    if cached:
        in_specs += [pl.BlockSpec((None, None, PAST, KV_W), lambda i: (i, layer, 0, 0))] * 2
        args += [cache_k, cache_v]
    return pl.pallas_call(
        functools.partial(_attn_kernel, cached=cached),
        grid=(t // seq,), in_specs=in_specs,
        out_specs=pl.BlockSpec((seq, D), lambda i: (i, 0)),
        out_shape=jax.ShapeDtypeStruct((t, D), BF16),
        compiler_params=_cparams(1), name=name,
    )(*args)


def _merge_kernel(a_ref, o_ref, c_ref, gate_ref, x_ref, mod_ref, n2g_ref, wa_ref, wb_ref, wc_ref, wo_ref,
                  x1_ref, h2_ref):
    mix = gate_ref[:, 0:D].astype(F32) * _dot(a_ref[...], wa_ref[...])
    mix = mix + gate_ref[:, D:2 * D].astype(F32) * _dot(o_ref[...], wb_ref[...])
    mix = mix + gate_ref[:, 2 * D:3 * D].astype(F32) * _dot(c_ref[...], wc_ref[...])
    out = _dot(mix.astype(BF16), wo_ref[...])
    mod = mod_ref[...]
    g1, sh2, sc2 = mod[:, 2 * D:3 * D], mod[:, 3 * D:4 * D], mod[:, 4 * D:5 * D]
    x1 = x_ref[...] + g1 * out
    x1_ref[...] = x1
    h2_ref[...] = ((_rms(x1) * n2g_ref[...]) * (1.0 + sc2) + sh2).astype(BF16)


def _merge(a, o, c, gates, x, mod_l, n2g, wa, wb, wc, wo, *, layer, tm, row_of_tile, name):
    t = x.shape[0]
    row = lambda w: pl.BlockSpec((tm, w), lambda i: (i, 0))
    wspec = _resident((D, D), layer)
    return pl.pallas_call(
        _merge_kernel,
        grid=(t // tm,),
        in_specs=[row(D), row(D), row(D), row(3 * D), row(D),
                  pl.BlockSpec((None, 1, 6 * D), lambda i: (row_of_tile(i), 0, 0)),
                  pl.BlockSpec((1, D), lambda i: (0, 0)), wspec, wspec, wspec, wspec],
        out_specs=[row(D), row(D)],
        out_shape=[jax.ShapeDtypeStruct((t, D), F32), jax.ShapeDtypeStruct((t, D), BF16)],
        compiler_params=_cparams(1), name=name,
    )(a, o, c, gates, x, mod_l, n2g, wa, wb, wc, wo)


FF_TN = 1024
FF_SUB = 512
FF_ROWS = 1024
DOWN_TM = 1024


def _ffn_up_kernel(h_ref, wg_ref, wv_ref, cwg_ref, cwv_ref, cbg_ref, cbv_ref, out_ref, *, seq):
    h = h_ref[...]

    def conv(z, cw_ref, cb_ref, cols):
        y = cb_ref[:, cols] + _shift_prev(z, seq) * cw_ref[0:1, cols]
        y = y + z * cw_ref[1:2, cols]
        return y + _shift_next(z, seq) * cw_ref[2:3, cols]

    for c in range(FF_TN // FF_SUB):
        cols = slice(c * FF_SUB, (c + 1) * FF_SUB)
        g = conv(_dot(h, wg_ref[:, cols]), cwg_ref, cbg_ref, cols)
        v = conv(_dot(h, wv_ref[:, cols]), cwv_ref, cbv_ref, cols)
        hg = 0.5 * g
        out_ref[:, cols] = ((hg * jnp.tanh(hg) + hg) * v).astype(BF16)


def _ffn_up(h2, w_up, cw, cb, *, layer, seq, name):
    t = h2.shape[0]
    nj = D_FF // FF_TN
    return pl.pallas_call(
        functools.partial(_ffn_up_kernel, seq=seq),
        grid=(t // FF_ROWS, nj),
        in_specs=[pl.BlockSpec((FF_ROWS, D), lambda i, j: (i, 0)),
                  pl.BlockSpec((None, D, FF_TN), lambda i, j: (layer, 0, j)),
                  pl.BlockSpec((None, D, FF_TN), lambda i, j: (layer, 0, nj + j)),
                  pl.BlockSpec((3, FF_TN), lambda i, j: (0, j)),
                  pl.BlockSpec((3, FF_TN), lambda i, j: (0, nj + j)),
                  pl.BlockSpec((1, FF_TN), lambda i, j: (0, j)),
                  pl.BlockSpec((1, FF_TN), lambda i, j: (0, nj + j))],
        out_specs=pl.BlockSpec((FF_ROWS, FF_TN), lambda i, j: (i, j)),
        out_shape=jax.ShapeDtypeStruct((t, D_FF), BF16),
        compiler_params=_cparams(2), name=name,
    )(h2, w_up, w_up, cw, cw, cb, cb)


def _ffn_down_kernel(act_ref, w_ref, x1_ref, mod_ref, fg_ref, out_ref, *, final):
    g2 = mod_ref[:, 5 * D:6 * D]
    x2 = x1_ref[...] + g2 * _dot(act_ref[...], w_ref[...])
    out_ref[...] = _rms(x2) * fg_ref[...] if final else x2


def _ffn_down(act, w_down, x1, mod_l, fg, *, layer, tm, row_of_tile, final, name):
    t = x1.shape[0]
    return pl.pallas_call(
        functools.partial(_ffn_down_kernel, final=final),
        grid=(t // tm,),
        in_specs=[pl.BlockSpec((tm, D_FF), lambda i: (i, 0)),
                  _resident((D_FF, D), layer),
                  pl.BlockSpec((tm, D), lambda i: (i, 0)),
                  pl.BlockSpec((None, 1, 6 * D), lambda i: (row_of_tile(i), 0, 0)),
                  pl.BlockSpec((1, D), lambda i: (0, 0))],
        out_specs=pl.BlockSpec((tm, D), lambda i: (i, 0)),
        out_shape=jax.ShapeDtypeStruct((t, D), F32),
        compiler_params=_cparams(1), name=name,
    )(act, w_down, x1, mod_l, fg)


def _interleave(x):
    b, s, w = x.shape
    return x.reshape(b, SUB, s // SUB, w).transpose(0, 2, 1, 3).reshape(b * s, w)


def _deinterleave(y, b, s):
    return y.reshape(b, s // SUB, SUB, -1).transpose(0, 2, 1, 3).reshape(b, s, -1)


def _rope_tables():
    p = np.arange(S_LAT)
    t = (p % SUB) * (S_LAT // SUB) + p // SUB
    pos = np.stack([t // GRID_W, t % GRID_W], axis=1).astype(np.float32)
    quarter = HEAD_DIM // 4
    inv = (ROPE_BASE ** (-np.arange(quarter, dtype=np.float32) / quarter)).astype(np.float32)
    d = np.arange(LANE) % HEAD_DIM
    which = d // (HEAD_DIM // 2)
    e = d % (HEAD_DIM // 2)
    ang = pos[:, which] * inv[e % quarter][None, :]
    sign = np.where(e < quarter, -1.0, 1.0).astype(np.float32)
    return jnp.asarray(np.cos(ang).astype(np.float32)), jnp.asarray((np.sin(ang) * sign).astype(np.float32))


def _block_diag_gates(w_r, w_i):
    eye = jnp.eye(4, dtype=F32)

    def bd(w):
        w = w.reshape(2, 4, 4, LRU_BW, LRU_BW)
        w = w[:, :, :, :, None, :] * eye[None, None, :, None, :, None]
        return w.reshape(2, 4, GB, GB)

    return jnp.concatenate([bd(w_r), bd(w_i)], axis=-1).astype(BF16)


def _trunk(x_prompt, x_sample, cache_k, cache_v, state_lru, c, c_ctx, norm1_g, norm2_g, w_mod, b_mod, w_in,
           b_gate, conv_w, conv_b, w_rg, b_rg, w_ig, b_ig, lru_lambda, q_norm_g, k_norm_g, w_pool, pool_scale,
           w_br_a, w_br_b, w_br_c, w_o, w_up, ffn_conv_w, ffn_conv_b, w_down, final_norm_g,
           paths=("ctx", "lat")):
    cond = jnp.zeros((16, D), F32).at[0:N_LAT].set(c).at[N_LAT].set(c_ctx)
    mod = _modulation(cond, w_mod, b_mod)
    rope_tabs = _rope_tables()
    gmat = jnp.asarray(np.kron(np.eye(2), np.full((HEAD_DIM, HEAD_DIM), 1.0 / HEAD_DIM)), BF16)
    ck = cache_k.reshape(N_LAT, DEPTH, PAST, KV_W)
    cv = cache_v.reshape(N_LAT, DEPTH, PAST, KV_W)
    fg = final_norm_g.reshape(1, D)

    xs = {"ctx": _interleave(x_prompt), "lat": _interleave(x_sample)}
    cfg = {"ctx": dict(seq=S_CTX, tm=512, row=lambda i: N_LAT),
           "lat": dict(seq=S_LAT, tm=512, row=lambda i: i // (S_LAT // 512))}
    new_s = []
    kv_bufs = None
    w_in_b, wa, wb, wc, wo, w_up_b, w_down_b, wp = (
        w.astype(BF16) for w in (w_in, w_br_a, w_br_b, w_br_c, w_o, w_up, w_down, w_pool))
    for l in range(DEPTH):
        mod_l = mod[l].reshape(16, 1, 6 * D)
        wg = _block_diag_gates(0.5 * w_rg[l], 0.5 * w_ig[l])
        bg = 0.5 * jnp.stack([b_rg[l], b_ig[l]], axis=1)
        qg = jnp.tile(q_norm_g[l], 2).reshape(1, LANE)
        kg = jnp.tile(k_norm_g[l], 2).reshape(1, LANE)
        for path in paths:
            seq, tm, row = cfg[path]["seq"], cfg[path]["tm"], cfg[path]["row"]
            lat = path == "lat"
            x = xs[path]
            xr, gy, q, k, v, up, gates = _inproj(
                x, mod_l, norm1_g[l].reshape(1, D), w_in_b, b_gate[l].reshape(1, 3 * D), qg, kg, gmat,
                rope_tabs if lat else None, None if lat else kv_bufs, layer=l, tm=tm, row_of_tile=row,
                name=f"inproj_{path}{l}")
            if not lat:
                kv_bufs = (k, v)
            h0 = state_lru[:, l] if lat else jnp.zeros((N_CTX, 2, LRU_W), F32)
            a_pre, ends = _lru(xr, gy, h0, conv_w[l], conv_b[l].reshape(1, LRU_W), wg, bg, lru_lambda[l],
                               seq=seq, name=f"lru_{path}{l}")
            c_pre = _pool(up, wp, pool_scale[l].reshape(1, D), layer=l, seq=seq, name=f"pool_{path}{l}")
            o = _attention(q, k, v, ck if lat else None, cv if lat else None, l, seq=seq,
                           name=f"attn_{path}{l}")
            x1, h2 = _merge(a_pre, o, c_pre, gates, x, mod_l, norm2_g[l].reshape(1, D), wa, wb, wc, wo,
                            layer=l, tm=tm, row_of_tile=row, name=f"merge_{path}{l}")
            act = _ffn_up(h2, w_up_b, ffn_conv_w[l], ffn_conv_b[l].reshape(1, 2 * D_FF), layer=l, seq=seq,
                          name=f"ffn_up_{path}{l}")
            dn_row = (lambda i: i) if lat else row
            xs[path] = _ffn_down(act, w_down_b, x1, mod_l, fg, layer=l, tm=DOWN_TM, row_of_tile=dn_row,
                                 final=(l == DEPTH - 1), name=f"ffn_down_{path}{l}")
            if not lat:
                new_s.append(jnp.stack([ends[:, SUB - 1], ends[:, SUB]], axis=1))
    new_k, new_v = (b.reshape(N_CTX, DEPTH, S_CTX, N_KV, HEAD_DIM) for b in kv_bufs)
    return xs, new_k, new_v, new_s


def kernel(x_prompt, x_sample, cache_k, cache_v, state_lru, c, c_ctx, norm1_g, norm2_g, w_mod, b_mod, w_in,
           b_gate, conv_w, conv_b, w_rg, b_rg, w_ig, b_ig, lru_lambda, q_norm_g, k_norm_g, w_pool, pool_scale,
           w_br_a, w_br_b, w_br_c, w_o, w_up, ffn_conv_w, ffn_conv_b, w_down, final_norm_g):
    xs, new_k, new_v, new_s = _trunk(
        x_prompt, x_sample, cache_k, cache_v, state_lru, c, c_ctx, norm1_g, norm2_g, w_mod, b_mod, w_in,
        b_gate, conv_w, conv_b, w_rg, b_rg, w_ig, b_ig, lru_lambda, q_norm_g, k_norm_g, w_pool, pool_scale,
        w_br_a, w_br_b, w_br_c, w_o, w_up, ffn_conv_w, ffn_conv_b, w_down, final_norm_g)
    y_prompt = _deinterleave(xs["ctx"], N_CTX, S_CTX)
    y_sample = _deinterleave(xs["lat"], N_LAT, S_LAT)
    return (y_prompt, y_sample, new_k, new_v, jnp.stack(new_s, axis=1))
```

```python
import functools

import numpy as np
import jax
import jax.numpy as jnp
from jax import lax
from jax.experimental import pallas as pl
from jax.experimental.pallas import tpu as pltpu

F32 = jnp.float32
BF16 = jnp.bfloat16

D = 1024
DEPTH = 2
N_CTX, S_CTX = 16, 256
N_LAT, S_LAT = 8, 1024
PAST = 256
GRID_W = 64
LRU_W = 1024
LRU_BW = 64
LRU_C = 8.0
N_HEADS, N_KV, HEAD_DIM = 16, 4, 64
KV_W = N_KV * HEAD_DIM
ROPE_BASE = 10000.0
D_FF = 3 * D
EPS = 1e-6
SUB = 8
LANE = 128
ROPE_SHIFT = HEAD_DIM // 4
C_XR, C_YR, C_Q, C_K, C_V, C_UP, C_GL, C_END = 0, 1024, 2048, 3072, 3328, 3584, 4608, 7680

LOG2E = float(np.log2(np.e))
Q_PRESCALE = LOG2E * HEAD_DIM ** -0.5

VMEM_LIMIT = 56 * 1024 * 1024


def _cparams(n_axes):
    return pltpu.CompilerParams(dimension_semantics=("arbitrary",) * n_axes,
                                vmem_limit_bytes=VMEM_LIMIT)


def _resident(shape, layer):
    return pl.BlockSpec((None,) + shape, lambda *_: (layer,) + (0,) * len(shape), pipeline_mode=pl.Buffered(1))


def _dot(a, b):
    return jnp.dot(a, b, preferred_element_type=F32)


def _dot_nt(a, b):
    return lax.dot_general(a, b, (((1,), (1,)), ((), ())), preferred_element_type=F32)


def _rms(x):
    return x * lax.rsqrt(jnp.mean(x * x, axis=-1, keepdims=True) + EPS)


def _shift_prev(x, rm, n=1):
    r, w = x.shape
    sub = lax.broadcasted_iota(jnp.int32, (SUB, w), 0)
    pieces = []
    for m in range(r // rm):
        base = m * rm
        for t in range(n):
            lo = base + rm - SUB * (n - t)
            pieces.append(jnp.where(sub == 0, 0.0, pltpu.roll(x[lo:lo + SUB], 1, 0)))
        pieces.append(x[base:base + rm - SUB * n])
    return jnp.concatenate(pieces, axis=0)


def _shift_next(x, rm, n=1):
    r, w = x.shape
    sub = lax.broadcasted_iota(jnp.int32, (SUB, w), 0)
    pieces = []
    for m in range(r // rm):
        base = m * rm
        pieces.append(x[base + SUB * n:base + rm])
        for t in range(n):
            lo = base + SUB * t
            pieces.append(jnp.where(sub == SUB - 1, 0.0, pltpu.roll(x[lo:lo + SUB], SUB - 1, 0)))
    return jnp.concatenate(pieces, axis=0)


def _mod_kernel(cond_ref, w_ref, b_ref, out_ref):
    c = cond_ref[...]
    s = (c * jax.nn.sigmoid(c)).astype(BF16)
    out_ref[...] = _dot(s, w_ref[...].astype(BF16)) + b_ref[...]


def _modulation(cond, w_mod, b_mod):
    tn = 1536
    return pl.pallas_call(
        _mod_kernel,
        grid=(DEPTH, 6 * D // tn),
        in_specs=[pl.BlockSpec((16, D), lambda l, j: (0, 0)),
                  pl.BlockSpec((None, D, tn), lambda l, j: (l, 0, j)),
                  pl.BlockSpec((None, 1, tn), lambda l, j: (l, 0, j))],
        out_specs=pl.BlockSpec((None, 16, tn), lambda l, j: (l, 0, j)),
        out_shape=jax.ShapeDtypeStruct((DEPTH, 16, 6 * D), F32),
        compiler_params=_cparams(2),
        name="modulation",
    )(cond, w_mod, b_mod.reshape(DEPTH, 1, 6 * D))


def _inproj_kernel(*refs, rope, layer):
    if rope:
        (x_ref, mod_ref, n1g_ref, w_ref, bgate_ref, qg_ref, kg_ref, gmat_ref, cos_ref, sin_ref,
         xr_ref, gy_ref, q_ref, k_ref, v_ref, up_ref, gate_ref) = refs
    else:
        if layer == 0:
            (x_ref, mod_ref, n1g_ref, w_ref, bgate_ref, qg_ref, kg_ref, gmat_ref,
             xr_ref, gy_ref, q_ref, kc_ref, vc_ref, up_ref, gate_ref, stage_ref) = refs
            for dst in (kc_ref, vc_ref):
                for other in range(1, DEPTH):
                    dst[:, other] = jnp.zeros((dst.shape[0],) + dst.shape[2:], F32)
            kc_ref, vc_ref = kc_ref.at[:, 0], vc_ref.at[:, 0]
        else:
            (x_ref, mod_ref, n1g_ref, w_ref, bgate_ref, qg_ref, kg_ref, gmat_ref, _, _,
             xr_ref, gy_ref, q_ref, kc_ref, vc_ref, up_ref, gate_ref, stage_ref) = refs
    mod = mod_ref[...]
    sh1, sc1 = mod[:, 0:D], mod[:, D:2 * D]
    h = ((_rms(x_ref[...]) * n1g_ref[...]) * (1.0 + sc1) + sh1).astype(BF16)

    def mm(c0, c1):
        return _dot(h, w_ref[:, c0:c1])

    for g in range(3):
        z = mm(C_GL + g * D, C_GL + (g + 1) * D) + bgate_ref[:, g * D:(g + 1) * D]
        gate_ref[:, g * D:(g + 1) * D] = jax.nn.sigmoid(z).astype(BF16)
    gy_ref[...] = jax.nn.gelu(mm(C_YR, C_Q)).astype(BF16)

    lane = lax.broadcasted_iota(jnp.int32, (1, LANE), 1)
    first = (lane & ROPE_SHIFT) == 0

    def head_norm(xb, g):
        ms = _dot((xb * xb).astype(BF16), gmat_ref[...])
        y = (xb * lax.rsqrt(ms + EPS)) * g
        if rope:
            partner = jnp.where(first, pltpu.roll(y, LANE - ROPE_SHIFT, 1), pltpu.roll(y, ROPE_SHIFT, 1))
            y = y * cos_ref[...] + partner * sin_ref[...]
        return y

    qa = mm(C_Q, C_K)
    for c in range(N_HEADS * HEAD_DIM // LANE):
        qn = head_norm(qa[:, c * LANE:(c + 1) * LANE], qg_ref[...])
        q_ref[:, c * LANE:(c + 1) * LANE] = (qn * Q_PRESCALE).astype(BF16)
    kv_slabs = KV_W // LANE
    ka = mm(C_K, C_V)
    for c in range(kv_slabs):
        kn = head_norm(ka[:, c * LANE:(c + 1) * LANE], kg_ref[...])
        if rope:
            k_ref[:, c * LANE:(c + 1) * LANE] = kn
        else:
            stage_ref[c] = kn
    up_ref[...] = mm(C_UP, C_GL).astype(BF16)
    xr_ref[...] = mm(C_XR, C_YR).astype(BF16)
    va = mm(C_V, C_UP)
    if rope:
        v_ref[...] = va
    else:
        steps = S_CTX // SUB
        for c in range(kv_slabs):
            stage_ref[kv_slabs + c] = va[:, c * LANE:(c + 1) * LANE]
        for slab, dst in enumerate([kc_ref] * kv_slabs + [vc_ref] * kv_slabs):
            cols = slice((slab % kv_slabs) * LANE, (slab % kv_slabs + 1) * LANE)
            for s in range(x_ref.shape[0] // S_CTX):
                for j in range(SUB):
                    dst[s, j * steps:(j + 1) * steps, cols] = (
                        stage_ref[slab, pl.ds(s * S_CTX + j, steps, stride=SUB), :])


def _inproj(x, mod_l, n1g, w_in, b_gate, qg, kg, gmat, rope_tabs, kv_bufs, *, layer, tm, row_of_tile, name):
    t = x.shape[0]
    rope = rope_tabs is not None
    full = lambda shape: pl.BlockSpec(shape, lambda i: (0,) * len(shape))
    in_specs = [pl.BlockSpec((tm, D), lambda i: (i, 0)),
                pl.BlockSpec((None, 1, 6 * D), lambda i: (row_of_tile(i), 0, 0)),
                full((1, D)), _resident((D, C_END), layer), full((1, 3 * D)), full((1, LANE)), full((1, LANE)),
                full((LANE, LANE))]
    args = [x, mod_l, n1g, w_in, b_gate, qg, kg, gmat]
    if rope:
        per_seq = S_LAT // tm
        in_specs += [pl.BlockSpec((tm, LANE), lambda i: (i % per_seq, 0))] * 2
        args += list(rope_tabs)
    row = lambda w: pl.BlockSpec((tm, w), lambda i: (i, 0))
    act = lambda w: jax.ShapeDtypeStruct((t, w), BF16)
    if rope:
        kv_specs = [row(KV_W), row(KV_W)]
        kv_shapes = [jax.ShapeDtypeStruct((t, KV_W), F32)] * 2
        aliases, scratch = {}, []
    else:
        kv_shapes = [jax.ShapeDtypeStruct((N_CTX, DEPTH, S_CTX, KV_W), F32)] * 2
        if layer == 0:
            kv_specs = [pl.BlockSpec((tm // S_CTX, DEPTH, S_CTX, KV_W), lambda i: (i, 0, 0, 0))] * 2
            aliases = {}
        else:
            in_specs += [pl.BlockSpec(memory_space=pl.ANY)] * 2
            args += list(kv_bufs)
            kv_specs = [pl.BlockSpec((tm // S_CTX, None, S_CTX, KV_W), lambda i: (i, layer, 0, 0))] * 2
            aliases = {len(args) - 2: 3, len(args) - 1: 4}
        scratch = [pltpu.VMEM((2 * KV_W // LANE, tm, LANE), F32)]
    out_specs = [row(D), row(D), row(D)] + kv_specs + [row(D), row(3 * D)]
    out_shape = [act(D), act(D), act(D)] + kv_shapes + [act(D), act(3 * D)]
    return pl.pallas_call(
        functools.partial(_inproj_kernel, rope=rope, layer=layer),
        grid=(t // tm,), in_specs=in_specs, out_specs=out_specs, out_shape=out_shape,
        input_output_aliases=aliases, scratch_shapes=scratch,
        compiler_params=_cparams(1), name=name,
    )(*args)


GB = 256


def _lru_kernel(xr_ref, gy_ref, h0_ref, cw_ref, cb_ref, wg_ref, bg_ref, lam_ref,
                out_ref, ends_ref, a_s, u_s, h_s, p_s, xs_ref, xc_s):
    r = xr_ref.shape[0]
    steps = r // SUB
    sub = lax.broadcasted_iota(jnp.int32, (SUB, GB), 0)
    for cb in range(LRU_W // GB):
        cols = slice(cb * GB, (cb + 1) * GB)
        x = xr_ref[:, cols].astype(F32)
        xs_ref[SUB:r + SUB, :] = x
        xs_ref[0:SUB, :] = jnp.where(sub == 0, 0.0, pltpu.roll(x[r - SUB:r], 1, 0))
        for t in range(2):
            xs_ref[r + (1 + t) * SUB:r + (2 + t) * SUB, :] = jnp.where(
                sub == SUB - 1, 0.0, pltpu.roll(x[t * SUB:(t + 1) * SUB], SUB - 1, 0))
        xc = cb_ref[:, cols] + xs_ref[0:r, :] * cw_ref[0:1, cols]
        for t in range(1, 4):
            xc = xc + xs_ref[t * SUB:r + t * SUB, :] * cw_ref[t:t + 1, cols]
        xc_s[...] = xc
        lhs = xc_s[...].astype(BF16)
        xh = 0.5 * xc_s[...]
        for d in range(2):
            g = _dot(lhs, wg_ref[d, cb])
            tr = jnp.tanh(g[:, :GB] + bg_ref[d, 0:1, cols])
            ti = jnp.tanh(g[:, GB:] + bg_ref[d, 1:2, cols])
            lam = lam_ref[d:d + 1, cols]
            log_sig = jnp.minimum(lam, 0.0) - jnp.log(1.0 + jnp.exp(-jnp.abs(lam)))
            ch = (0.5 * LRU_C) * log_sig
            a = jnp.exp2(tr * (ch * LOG2E) + ch * LOG2E)
            m2 = jnp.tanh(tr * (-ch) - ch) * (1.0 + a * a)
            mult = jnp.where(m2 > 0.0, m2 * lax.rsqrt(m2), 0.0)
            u = mult * (ti * xh + xh)
            a_s[d] = a
            u_s[d] = u
            e0 = 0 if d == 0 else r - SUB
            edge = sub == (0 if d == 0 else SUB - 1)
            ae = a_s[d, e0:e0 + SUB, :]
            u_s[d, e0:e0 + SUB, :] = u_s[d, e0:e0 + SUB, :] + jnp.where(edge, ae * h0_ref[d:d + 1, cols], 0.0)
            a_s[d, e0:e0 + SUB, :] = jnp.where(edge, 0.0, ae)

        def step(k, carry):
            hf, pf, hb, pb = carry
            rf = pl.multiple_of(k * SUB, SUB)
            rb = pl.multiple_of((steps - 1 - k) * SUB, SUB)
            af = a_s[0, pl.ds(rf, SUB), :]
            hf = af * hf + u_s[0, pl.ds(rf, SUB), :]
            pf = af * pf
            h_s[0, pl.ds(rf, SUB), :] = hf
            p_s[0, pl.ds(rf, SUB), :] = pf
            ab = a_s[1, pl.ds(rb, SUB), :]
            hb = ab * hb + u_s[1, pl.ds(rb, SUB), :]
            pb = ab * pb
            h_s[1, pl.ds(rb, SUB), :] = hb
            p_s[1, pl.ds(rb, SUB), :] = pb
            return hf, pf, hb, pb

        zero = jnp.zeros((SUB, GB), F32)
        one = jnp.ones((SUB, GB), F32)
        hf, pf, hb, pb = lax.fori_loop(0, steps, step, (zero, one, zero, one), unroll=8)

        ef, eb = hf, hb
        for sh in (1, 2, 4):
            keep_f = sub >= sh
            ef = ef + pf * jnp.where(keep_f, pltpu.roll(ef, sh, 0), 0.0)
            pf = pf * jnp.where(keep_f, pltpu.roll(pf, sh, 0), 1.0)
            keep_b = sub < SUB - sh
            eb = eb + pb * jnp.where(keep_b, pltpu.roll(eb, SUB - sh, 0), 0.0)
            pb = pb * jnp.where(keep_b, pltpu.roll(pb, SUB - sh, 0), 1.0)
        ends_ref[0:SUB, cols] = ef
        ends_ref[SUB:2 * SUB, cols] = eb
        cf = jnp.where(sub >= 1, pltpu.roll(ef, 1, 0), 0.0)
        cbk = jnp.where(sub < SUB - 1, pltpu.roll(eb, SUB - 1, 0), 0.0)
        cf2 = jnp.concatenate([cf, cf], axis=0)
        cb2 = jnp.concatenate([cbk, cbk], axis=0)

        def fix(m, _):
            rows = pl.ds(pl.multiple_of(m * 2 * SUB, 2 * SUB), 2 * SUB)
            hft = h_s[0, rows, :] + p_s[0, rows, :] * cf2
            hbt = h_s[1, rows, :] + p_s[1, rows, :] * cb2
            out_ref[rows, cols] = ((hft + hbt) * gy_ref[rows, cols].astype(F32)).astype(BF16)
            return 0

        lax.fori_loop(0, r // (2 * SUB), fix, 0, unroll=4)


def _lru(xr, gy, h0, conv_w, conv_b, wg, bg, lam, *, seq, name):
    t = xr.shape[0]
    nseq = t // seq
    full = lambda shape: pl.BlockSpec(shape, lambda i: (0,) * len(shape))
    return pl.pallas_call(
        _lru_kernel,
        grid=(nseq,),
        in_specs=[pl.BlockSpec((seq, D), lambda i: (i, 0)), pl.BlockSpec((seq, D), lambda i: (i, 0)),
                  pl.BlockSpec((None, 2, LRU_W), lambda i: (i, 0, 0)),
                  full((4, LRU_W)), full((1, LRU_W)), full((2, LRU_W // GB, GB, 2 * GB)),
                  full((2, 2, LRU_W)), full((2, LRU_W))],
        out_specs=[pl.BlockSpec((seq, D), lambda i: (i, 0)),
                   pl.BlockSpec((None, 2 * SUB, LRU_W), lambda i: (i, 0, 0))],
        out_shape=[jax.ShapeDtypeStruct((t, D), BF16), jax.ShapeDtypeStruct((nseq, 2 * SUB, LRU_W), F32)],
        scratch_shapes=[pltpu.VMEM((2, seq, GB), F32)] * 4 +
                       [pltpu.VMEM((seq + 3 * SUB, GB), F32), pltpu.VMEM((seq, GB), F32)],
        compiler_params=_cparams(1), name=name,
    )(xr, gy, h0, conv_w, conv_b, wg, bg, lam)


POOL_G = 256


def _pool_kernel(up_ref, wp_ref, scale_ref, out_ref):
    r = up_ref.shape[0]
    steps = r // SUB
    row = lax.broadcasted_iota(jnp.int32, (r, 1), 0)
    t = (row & (SUB - 1)) * steps + jnp.right_shift(row, 3)
    for g in range(4):
        cols = slice(g * POOL_G, (g + 1) * POOL_G)
        x = up_ref[:, cols].astype(F32)
        half = 1 << g
        back, fwd = x, x
        for lvl in range(g):
            n = 1 << lvl
            back = back + _shift_prev(back, r, n)
            fwd = fwd + _shift_next(fwd, r, n)
        win = _shift_prev(back, r) + fwd
        cnt = jnp.minimum(t + half, r) - jnp.maximum(t - half, 0)
        d = (win / cnt.astype(F32) - x).astype(BF16)
        out_ref[:, cols] = (_dot(d, wp_ref[g]) * scale_ref[:, cols]).astype(BF16)


def _pool(up, w_pool, scale, *, layer, seq, name):
    t = up.shape[0]
    return pl.pallas_call(
        _pool_kernel,
        grid=(t // seq,),
        in_specs=[pl.BlockSpec((seq, D), lambda i: (i, 0)),
                  pl.BlockSpec((None, 4, POOL_G, POOL_G), lambda i: (layer, 0, 0, 0)),
                  pl.BlockSpec((1, D), lambda i: (0, 0))],
        out_specs=pl.BlockSpec((seq, D), lambda i: (i, 0)),
        out_shape=jax.ShapeDtypeStruct((t, D), BF16),
        compiler_params=_cparams(1), name=name,
    )(up, w_pool, scale)


ONES_ROWS = 16


def _attn_kernel(*refs, cached):
    transposed = cached
    if cached:
        q_ref, k_ref, v_ref, ck_ref, cv_ref, o_ref = refs
        kall = jnp.concatenate([ck_ref[...], k_ref[...]], axis=0)
        vall = jnp.concatenate([cv_ref[...], v_ref[...]], axis=0)
    else:
        q_ref, k_ref, v_ref, o_ref = refs
        kall = k_ref[...]
        vall = v_ref[...]
    half = HEAD_DIM
    lane = lax.broadcasted_iota(jnp.int32, (1, LANE), 1)

    def both_halves(x, own_low):
        own = jnp.where((lane < half) if own_low else (lane >= half), x, 0.0)
        oth = pltpu.roll(own, half, 1)
        lo, hi = (own, oth) if own_low else (oth, own)
        return lo.astype(BF16), hi.astype(BF16)

    if transposed:
        heads = []
        for kvh in range(N_KV):
            blk = slice((kvh // 2) * LANE, (kvh // 2 + 1) * LANE)
            ks = both_halves(kall[:, blk], kvh % 2 == 0)
            r0 = (kvh % 2) * HEAD_DIM
            v_t = vall[:, blk].T[r0:r0 + HEAD_DIM]
            v_t = jnp.concatenate([v_t, jnp.ones((ONES_ROWS, v_t.shape[1]), F32)], axis=0).astype(BF16)
            for pair in range(2):
                heads += [(kk, v_t, kvh * 4 * HEAD_DIM + pair * LANE) for kk in ks]

        def scores(h):
            kk, _, c0 = heads[h]
            return _dot_nt(kk, q_ref[:, c0:c0 + LANE])

        s_next = scores(0)
        outs = []
        for h, (_, v_t, c0) in enumerate(heads):
            s = s_next
            if h + 1 < len(heads):
                s_next = scores(h + 1)
            e = jnp.exp2(s - jnp.max(s, axis=0, keepdims=True)).astype(BF16)
            pv = _dot(v_t, e)
            outs.append(pv[:HEAD_DIM] / pv[HEAD_DIM:HEAD_DIM + 1])
            if h % 2 == 1:
                o_ref[:, c0:c0 + LANE] = jnp.concatenate(outs, axis=0).T.astype(BF16)
                outs = []
        return

    for kvh in range(N_KV):
        blk = slice((kvh // 2) * LANE, (kvh // 2 + 1) * LANE)
        ks = both_halves(kall[:, blk], kvh % 2 == 0)
        vs = both_halves(vall[:, blk], kvh % 2 == 0)
        for pair in range(2):
            c0 = kvh * 4 * HEAD_DIM + pair * LANE
            qp = q_ref[:, c0:c0 + LANE]
            acc = jnp.zeros((qp.shape[0], LANE), F32)
            for kk, vv in zip(ks, vs):
                s = _dot_nt(qp, kk)
                e = jnp.exp2(s - jnp.max(s, axis=-1, keepdims=True))
                l = jnp.sum(e, axis=-1, keepdims=True)
                acc = acc + _dot(e.astype(BF16), vv) / l
            o_ref[:, c0:c0 + LANE] = acc.astype(BF16)


def _attention(q, k, v, cache_k, cache_v, layer, *, seq, name):
    t = q.shape[0]
    cached = cache_k is not None
    if cached:
        kv_spec = pl.BlockSpec((seq, KV_W), lambda i: (i, 0))
    else:
        kv_spec = pl.BlockSpec((None, None, seq, KV_W), lambda i: (i, layer, 0, 0))
    in_specs = [pl.BlockSpec((seq, D), lambda i: (i, 0)), kv_spec, kv_spec]
    args = [q, k, v]
    if cached:
        in_specs += [pl.BlockSpec((None, None, PAST, KV_W), lambda i: (i, layer, 0, 0))] * 2
        args += [cache_k, cache_v]
    return pl.pallas_call(
        functools.partial(_attn_kernel, cached=cached),
        grid=(t // seq,), in_specs=in_specs,
        out_specs=pl.BlockSpec((seq, D), lambda i: (i, 0)),
        out_shape=jax.ShapeDtypeStruct((t, D), BF16),
        compiler_params=_cparams(1), name=name,
    )(*args)


def _merge_kernel(a_ref, o_ref, c_ref, gate_ref, x_ref, mod_ref, n2g_ref, wa_ref, wb_ref, wc_ref, wo_ref,
                  x1_ref, h2_ref):
    mix = gate_ref[:, 0:D].astype(F32) * _dot(a_ref[...], wa_ref[...])
    mix = mix + gate_ref[:, D:2 * D].astype(F32) * _dot(o_ref[...], wb_ref[...])
    mix = mix + gate_ref[:, 2 * D:3 * D].astype(F32) * _dot(c_ref[...], wc_ref[...])
    out = _dot(mix.astype(BF16), wo_ref[...])
    mod = mod_ref[...]
    g1, sh2, sc2 = mod[:, 2 * D:3 * D], mod[:, 3 * D:4 * D], mod[:, 4 * D:5 * D]
    x1 = x_ref[...] + g1 * out
    x1_ref[...] = x1
    h2_ref[...] = ((_rms(x1) * n2g_ref[...]) * (1.0 + sc2) + sh2).astype(BF16)


def _merge(a, o, c, gates, x, mod_l, n2g, wa, wb, wc, wo, *, layer, tm, row_of_tile, name):
    t = x.shape[0]
    row = lambda w: pl.BlockSpec((tm, w), lambda i: (i, 0))
    wspec = _resident((D, D), layer)
    return pl.pallas_call(
        _merge_kernel,
        grid=(t // tm,),
        in_specs=[row(D), row(D), row(D), row(3 * D), row(D),
                  pl.BlockSpec((None, 1, 6 * D), lambda i: (row_of_tile(i), 0, 0)),
                  pl.BlockSpec((1, D), lambda i: (0, 0)), wspec, wspec, wspec, wspec],
        out_specs=[row(D), row(D)],
        out_shape=[jax.ShapeDtypeStruct((t, D), F32), jax.ShapeDtypeStruct((t, D), BF16)],
        compiler_params=_cparams(1), name=name,
    )(a, o, c, gates, x, mod_l, n2g, wa, wb, wc, wo)


FF_TN = 1024
FF_SUB = 512
FF_ROWS = 1024
DOWN_TM = 1024


def _ffn_up_kernel(h_ref, wg_ref, wv_ref, cwg_ref, cwv_ref, cbg_ref, cbv_ref, out_ref, *, seq):
    h = h_ref[...]

    def conv(z, cw_ref, cb_ref, cols):
        y = cb_ref[:, cols] + _shift_prev(z, seq) * cw_ref[0:1, cols]
        y = y + z * cw_ref[1:2, cols]
        return y + _shift_next(z, seq) * cw_ref[2:3, cols]

    for c in range(FF_TN // FF_SUB):
        cols = slice(c * FF_SUB, (c + 1) * FF_SUB)
        g = conv(_dot(h, wg_ref[:, cols]), cwg_ref, cbg_ref, cols)
        v = conv(_dot(h, wv_ref[:, cols]), cwv_ref, cbv_ref, cols)
        hg = 0.5 * g
        out_ref[:, cols] = ((hg * jnp.tanh(hg) + hg) * v).astype(BF16)


def _ffn_up(h2, w_up, cw, cb, *, layer, seq, name):
    t = h2.shape[0]
    nj = D_FF // FF_TN
    return pl.pallas_call(
        functools.partial(_ffn_up_kernel, seq=seq),
        grid=(t // FF_ROWS, nj),
        in_specs=[pl.BlockSpec((FF_ROWS, D), lambda i, j: (i, 0)),
                  pl.BlockSpec((None, D, FF_TN), lambda i, j: (layer, 0, j)),
                  pl.BlockSpec((None, D, FF_TN), lambda i, j: (layer, 0, nj + j)),
                  pl.BlockSpec((3, FF_TN), lambda i, j: (0, j)),
                  pl.BlockSpec((3, FF_TN), lambda i, j: (0, nj + j)),
                  pl.BlockSpec((1, FF_TN), lambda i, j: (0, j)),
                  pl.BlockSpec((1, FF_TN), lambda i, j: (0, nj + j))],
        out_specs=pl.BlockSpec((FF_ROWS, FF_TN), lambda i, j: (i, j)),
        out_shape=jax.ShapeDtypeStruct((t, D_FF), BF16),
        compiler_params=_cparams(2), name=name,
    )(h2, w_up, w_up, cw, cw, cb, cb)


def _ffn_down_kernel(act_ref, w_ref, x1_ref, mod_ref, fg_ref, out_ref, *, final):
    g2 = mod_ref[:, 5 * D:6 * D]
    x2 = x1_ref[...] + g2 * _dot(act_ref[...], w_ref[...])
    out_ref[...] = _rms(x2) * fg_ref[...] if final else x2


def _ffn_down(act, w_down, x1, mod_l, fg, *, layer, tm, row_of_tile, final, name):
    t = x1.shape[0]
    return pl.pallas_call(
        functools.partial(_ffn_down_kernel, final=final),
        grid=(t // tm,),
        in_specs=[pl.BlockSpec((tm, D_FF), lambda i: (i, 0)),
                  _resident((D_FF, D), layer),
                  pl.BlockSpec((tm, D), lambda i: (i, 0)),
                  pl.BlockSpec((None, 1, 6 * D), lambda i: (row_of_tile(i), 0, 0)),
                  pl.BlockSpec((1, D), lambda i: (0, 0))],
        out_specs=pl.BlockSpec((tm, D), lambda i: (i, 0)),
        out_shape=jax.ShapeDtypeStruct((t, D), F32),
        compiler_params=_cparams(1), name=name,
    )(act, w_down, x1, mod_l, fg)


def _interleave(x):
    b, s, w = x.shape
    return x.reshape(b, SUB, s // SUB, w).transpose(0, 2, 1, 3).reshape(b * s, w)


def _deinterleave(y, b, s):
    return y.reshape(b, s // SUB, SUB, -1).transpose(0, 2, 1, 3).reshape(b, s, -1)


def _rope_tables():
    p = np.arange(S_LAT)
    t = (p % SUB) * (S_LAT // SUB) + p // SUB
    pos = np.stack([t // GRID_W, t % GRID_W], axis=1).astype(np.float32)
    quarter = HEAD_DIM // 4
    inv = (ROPE_BASE ** (-np.arange(quarter, dtype=np.float32) / quarter)).astype(np.float32)
    d = np.arange(LANE) % HEAD_DIM
    which = d // (HEAD_DIM // 2)
    e = d % (HEAD_DIM // 2)
    ang = pos[:, which] * inv[e % quarter][None, :]
    sign = np.where(e < quarter, -1.0, 1.0).astype(np.float32)
    return jnp.asarray(np.cos(ang).astype(np.float32)), jnp.asarray((np.sin(ang) * sign).astype(np.float32))


def _block_diag_gates(w_r, w_i):
    eye = jnp.eye(4, dtype=F32)

    def bd(w):
        w = w.reshape(2, 4, 4, LRU_BW, LRU_BW)
        w = w[:, :, :, :, None, :] * eye[None, None, :, None, :, None]
        return w.reshape(2, 4, GB, GB)

    return jnp.concatenate([bd(w_r), bd(w_i)], axis=-1).astype(BF16)


def _trunk(x_prompt, x_sample, cache_k, cache_v, state_lru, c, c_ctx, norm1_g, norm2_g, w_mod, b_mod, w_in,
           b_gate, conv_w, conv_b, w_rg, b_rg, w_ig, b_ig, lru_lambda, q_norm_g, k_norm_g, w_pool, pool_scale,
           w_br_a, w_br_b, w_br_c, w_o, w_up, ffn_conv_w, ffn_conv_b, w_down, final_norm_g,
           paths=("ctx", "lat")):
    cond = jnp.zeros((16, D), F32).at[0:N_LAT].set(c).at[N_LAT].set(c_ctx)
    mod = _modulation(cond, w_mod, b_mod)
    rope_tabs = _rope_tables()
    gmat = jnp.asarray(np.kron(np.eye(2), np.full((HEAD_DIM, HEAD_DIM), 1.0 / HEAD_DIM)), BF16)
    ck = cache_k.reshape(N_LAT, DEPTH, PAST, KV_W)
    cv = cache_v.reshape(N_LAT, DEPTH, PAST, KV_W)
    fg = final_norm_g.reshape(1, D)

    xs = {"ctx": _interleave(x_prompt), "lat": _interleave(x_sample)}
    cfg = {"ctx": dict(seq=S_CTX, tm=512, row=lambda i: N_LAT),
           "lat": dict(seq=S_LAT, tm=512, row=lambda i: i // (S_LAT // 512))}
    new_s = []
    kv_bufs = None
    w_in_b, wa, wb, wc, wo, w_up_b, w_down_b, wp = (
        w.astype(BF16) for w in (w_in, w_br_a, w_br_b, w_br_c, w_o, w_up, w_down, w_pool))
    for l in range(DEPTH):
        mod_l = mod[l].reshape(16, 1, 6 * D)
        wg = _block_diag_gates(0.5 * w_rg[l], 0.5 * w_ig[l])
        bg = 0.5 * jnp.stack([b_rg[l], b_ig[l]], axis=1)
        qg = jnp.tile(q_norm_g[l], 2).reshape(1, LANE)
        kg = jnp.tile(k_norm_g[l], 2).reshape(1, LANE)
        for path in paths:
            seq, tm, row = cfg[path]["seq"], cfg[path]["tm"], cfg[path]["row"]
            lat = path == "lat"
            x = xs[path]
            xr, gy, q, k, v, up, gates = _inproj(
                x, mod_l, norm1_g[l].reshape(1, D), w_in_b, b_gate[l].reshape(1, 3 * D), qg, kg, gmat,
                rope_tabs if lat else None, None if lat else kv_bufs, layer=l, tm=tm, row_of_tile=row,
                name=f"inproj_{path}{l}")
            if not lat:
                kv_bufs = (k, v)
            h0 = state_lru[:, l] if lat else jnp.zeros((N_CTX, 2, LRU_W), F32)
            a_pre, ends = _lru(xr, gy, h0, conv_w[l], conv_b[l].reshape(1, LRU_W), wg, bg, lru_lambda[l],
                               seq=seq, name=f"lru_{path}{l}")
            c_pre = _pool(up, wp, pool_scale[l].reshape(1, D), layer=l, seq=seq, name=f"pool_{path}{l}")
            o = _attention(q, k, v, ck if lat else None, cv if lat else None, l, seq=seq,
                           name=f"attn_{path}{l}")
            x1, h2 = _merge(a_pre, o, c_pre, gates, x, mod_l, norm2_g[l].reshape(1, D), wa, wb, wc, wo,
                            layer=l, tm=tm, row_of_tile=row, name=f"merge_{path}{l}")
            act = _ffn_up(h2, w_up_b, ffn_conv_w[l], ffn_conv_b[l].reshape(1, 2 * D_FF), layer=l, seq=seq,
                          name=f"ffn_up_{path}{l}")
            dn_row = (lambda i: i) if lat else row
            xs[path] = _ffn_down(act, w_down_b, x1, mod_l, fg, layer=l, tm=DOWN_TM, row_of_tile=dn_row,
                                 final=(l == DEPTH - 1), name=f"ffn_down_{path}{l}")
            if not lat:
                new_s.append(jnp.stack([ends[:, SUB - 1], ends[:, SUB]], axis=1))
    new_k, new_v = (b.reshape(N_CTX, DEPTH, S_CTX, N_KV, HEAD_DIM) for b in kv_bufs)
    return xs, new_k, new_v, new_s


def kernel(x_prompt, x_sample, cache_k, cache_v, state_lru, c, c_ctx, norm1_g, norm2_g, w_mod, b_mod, w_in,
           b_gate, conv_w, conv_b, w_rg, b_rg, w_ig, b_ig, lru_lambda, q_norm_g, k_norm_g, w_pool, pool_scale,
           w_br_a, w_br_b, w_br_c, w_o, w_up, ffn_conv_w, ffn_conv_b, w_down, final_norm_g):
    xs, new_k, new_v, new_s = _trunk(
        x_prompt, x_sample, cache_k, cache_v, state_lru, c, c_ctx, norm1_g, norm2_g, w_mod, b_mod, w_in,
        b_gate, conv_w, conv_b, w_rg, b_rg, w_ig, b_ig, lru_lambda, q_norm_g, k_norm_g, w_pool, pool_scale,
        w_br_a, w_br_b, w_br_c, w_o, w_up, ffn_conv_w, ffn_conv_b, w_down, final_norm_g)
    y_prompt = _deinterleave(xs["ctx"], N_CTX, S_CTX)
    y_sample = _deinterleave(xs["lat"], N_LAT, S_LAT)
    return (y_prompt, y_sample, new_k, new_v, jnp.stack(new_s, axis=1))
```

```python
import functools

import numpy as np
import jax
import jax.numpy as jnp
from jax import lax
from jax.experimental import pallas as pl
from jax.experimental.pallas import tpu as pltpu

F32 = jnp.float32
BF16 = jnp.bfloat16

D = 1024
DEPTH = 2
N_CTX, S_CTX = 16, 256
N_LAT, S_LAT = 8, 1024
PAST = 256
GRID_W = 64
LRU_W = 1024
LRU_BW = 64
LRU_C = 8.0
N_HEADS, N_KV, HEAD_DIM = 16, 4, 64
KV_W = N_KV * HEAD_DIM
ROPE_BASE = 10000.0
D_FF = 3 * D
EPS = 1e-6
SUB = 8
LANE = 128
ROPE_SHIFT = HEAD_DIM // 4
C_XR, C_YR, C_Q, C_K, C_V, C_UP, C_GL, C_END = 0, 1024, 2048, 3072, 3328, 3584, 4608, 7680

LOG2E = float(np.log2(np.e))
Q_PRESCALE = LOG2E * HEAD_DIM ** -0.5

VMEM_LIMIT = 56 * 1024 * 1024


def _cparams(n_axes):
    return pltpu.CompilerParams(dimension_semantics=("arbitrary",) * n_axes,
                                vmem_limit_bytes=VMEM_LIMIT)


def _resident(shape, layer):
    return pl.BlockSpec((None,) + shape, lambda *_: (layer,) + (0,) * len(shape), pipeline_mode=pl.Buffered(1))


def _dot(a, b):
    return jnp.dot(a, b, preferred_element_type=F32)


def _dot_nt(a, b):
    return lax.dot_general(a, b, (((1,), (1,)), ((), ())), preferred_element_type=F32)


def _rms(x):
    return x * lax.rsqrt(jnp.mean(x * x, axis=-1, keepdims=True) + EPS)


def _shift_prev(x, rm, n=1):
    r, w = x.shape
    sub = lax.broadcasted_iota(jnp.int32, (SUB, w), 0)
    pieces = []
    for m in range(r // rm):
        base = m * rm
        for t in range(n):
            lo = base + rm - SUB * (n - t)
            pieces.append(jnp.where(sub == 0, 0.0, pltpu.roll(x[lo:lo + SUB], 1, 0)))
        pieces.append(x[base:base + rm - SUB * n])
    return jnp.concatenate(pieces, axis=0)


def _shift_next(x, rm, n=1):
    r, w = x.shape
    sub = lax.broadcasted_iota(jnp.int32, (SUB, w), 0)
    pieces = []
    for m in range(r // rm):
        base = m * rm
        pieces.append(x[base + SUB * n:base + rm])
        for t in range(n):
            lo = base + SUB * t
            pieces.append(jnp.where(sub == SUB - 1, 0.0, pltpu.roll(x[lo:lo + SUB], SUB - 1, 0)))
    return jnp.concatenate(pieces, axis=0)


def _mod_kernel(cond_ref, w_ref, b_ref, out_ref):
    c = cond_ref[...]
    s = (c * jax.nn.sigmoid(c)).astype(BF16)
    out_ref[...] = _dot(s, w_ref[...].astype(BF16)) + b_ref[...]


def _modulation(cond, w_mod, b_mod):
    tn = 1536
    return pl.pallas_call(
        _mod_kernel,
        grid=(DEPTH, 6 * D // tn),
        in_specs=[pl.BlockSpec((16, D), lambda l, j: (0, 0)),
                  pl.BlockSpec((None, D, tn), lambda l, j: (l, 0, j)),
                  pl.BlockSpec((None, 1, tn), lambda l, j: (l, 0, j))],
        out_specs=pl.BlockSpec((None, 16, tn), lambda l, j: (l, 0, j)),
        out_shape=jax.ShapeDtypeStruct((DEPTH, 16, 6 * D), F32),
        compiler_params=_cparams(2),
        name="modulation",
    )(cond, w_mod, b_mod.reshape(DEPTH, 1, 6 * D))


def _inproj_kernel(*refs, rope, layer):
    if rope:
        (x_ref, mod_ref, n1g_ref, w_ref, bgate_ref, qg_ref, kg_ref, gmat_ref, cos_ref, sin_ref,
         xr_ref, gy_ref, q_ref, k_ref, v_ref, up_ref, gate_ref) = refs
    else:
        if layer == 0:
            (x_ref, mod_ref, n1g_ref, w_ref, bgate_ref, qg_ref, kg_ref, gmat_ref,
             xr_ref, gy_ref, q_ref, kc_ref, vc_ref, up_ref, gate_ref, stage_ref) = refs
            for dst in (kc_ref, vc_ref):
                for other in range(1, DEPTH):
                    dst[:, other] = jnp.zeros((dst.shape[0],) + dst.shape[2:], F32)
            kc_ref, vc_ref = kc_ref.at[:, 0], vc_ref.at[:, 0]
        else:
            (x_ref, mod_ref, n1g_ref, w_ref, bgate_ref, qg_ref, kg_ref, gmat_ref, _, _,
             xr_ref, gy_ref, q_ref, kc_ref, vc_ref, up_ref, gate_ref, stage_ref) = refs
    mod = mod_ref[...]
    sh1, sc1 = mod[:, 0:D], mod[:, D:2 * D]
    h = ((_rms(x_ref[...]) * n1g_ref[...]) * (1.0 + sc1) + sh1).astype(BF16)

    def mm(c0, c1):
        return _dot(h, w_ref[:, c0:c1])

    for g in range(3):
        z = mm(C_GL + g * D, C_GL + (g + 1) * D) + bgate_ref[:, g * D:(g + 1) * D]
        gate_ref[:, g * D:(g + 1) * D] = jax.nn.sigmoid(z).astype(BF16)
    gy_ref[...] = jax.nn.gelu(mm(C_YR, C_Q)).astype(BF16)

    lane = lax.broadcasted_iota(jnp.int32, (1, LANE), 1)
    first = (lane & ROPE_SHIFT) == 0

    def head_norm(xb, g):
        ms = _dot((xb * xb).astype(BF16), gmat_ref[...])
        y = (xb * lax.rsqrt(ms + EPS)) * g
        if rope:
            partner = jnp.where(first, pltpu.roll(y, LANE - ROPE_SHIFT, 1), pltpu.roll(y, ROPE_SHIFT, 1))
            y = y * cos_ref[...] + partner * sin_ref[...]
        return y

    qa = mm(C_Q, C_K)
    for c in range(N_HEADS * HEAD_DIM // LANE):
        qn = head_norm(qa[:, c * LANE:(c + 1) * LANE], qg_ref[...])
        q_ref[:, c * LANE:(c + 1) * LANE] = (qn * Q_PRESCALE).astype(BF16)
    kv_slabs = KV_W // LANE
    ka = mm(C_K, C_V)
    for c in range(kv_slabs):
        kn = head_norm(ka[:, c * LANE:(c + 1) * LANE], kg_ref[...])
        if rope:
            k_ref[:, c * LANE:(c + 1) * LANE] = kn
        else:
            stage_ref[c] = kn
    up_ref[...] = mm(C_UP, C_GL).astype(BF16)
    xr_ref[...] = mm(C_XR, C_YR).astype(BF16)
    va = mm(C_V, C_UP)
    if rope:
        v_ref[...] = va
    else:
        steps = S_CTX // SUB
        for c in range(kv_slabs):
            stage_ref[kv_slabs + c] = va[:, c * LANE:(c + 1) * LANE]
        for slab, dst in enumerate([kc_ref] * kv_slabs + [vc_ref] * kv_slabs):
            cols = slice((slab % kv_slabs) * LANE, (slab % kv_slabs + 1) * LANE)
            for s in range(x_ref.shape[0] // S_CTX):
                for j in range(SUB):
                    dst[s, j * steps:(j + 1) * steps, cols] = (
                        stage_ref[slab, pl.ds(s * S_CTX + j, steps, stride=SUB), :])


def _inproj(x, mod_l, n1g, w_in, b_gate, qg, kg, gmat, rope_tabs, kv_bufs, *, layer, tm, row_of_tile, name):
    t = x.shape[0]
    rope = rope_tabs is not None
    full = lambda shape: pl.BlockSpec(shape, lambda i: (0,) * len(shape))
    in_specs = [pl.BlockSpec((tm, D), lambda i: (i, 0)),
                pl.BlockSpec((None, 1, 6 * D), lambda i: (row_of_tile(i), 0, 0)),
                full((1, D)), _resident((D, C_END), layer), full((1, 3 * D)), full((1, LANE)), full((1, LANE)),
                full((LANE, LANE))]
    args = [x, mod_l, n1g, w_in, b_gate, qg, kg, gmat]
    if rope:
        per_seq = S_LAT // tm
        in_specs += [pl.BlockSpec((tm, LANE), lambda i: (i % per_seq, 0))] * 2
        args += list(rope_tabs)
    row = lambda w: pl.BlockSpec((tm, w), lambda i: (i, 0))
    act = lambda w: jax.ShapeDtypeStruct((t, w), BF16)
    if rope:
        kv_specs = [row(KV_W), row(KV_W)]
        kv_shapes = [jax.ShapeDtypeStruct((t, KV_W), F32)] * 2
        aliases, scratch = {}, []
    else:
        kv_shapes = [jax.ShapeDtypeStruct((N_CTX, DEPTH, S_CTX, KV_W), F32)] * 2
        if layer == 0:
            kv_specs = [pl.BlockSpec((tm // S_CTX, DEPTH, S_CTX, KV_W), lambda i: (i, 0, 0, 0))] * 2
            aliases = {}
        else:
            in_specs += [pl.BlockSpec(memory_space=pl.ANY)] * 2
            args += list(kv_bufs)
            kv_specs = [pl.BlockSpec((tm // S_CTX, None, S_CTX, KV_W), lambda i: (i, layer, 0, 0))] * 2
            aliases = {len(args) - 2: 3, len(args) - 1: 4}
        scratch = [pltpu.VMEM((2 * KV_W // LANE, tm, LANE), F32)]
    out_specs = [row(D), row(D), row(D)] + kv_specs + [row(D), row(3 * D)]
    out_shape = [act(D), act(D), act(D)] + kv_shapes + [act(D), act(3 * D)]
    return pl.pallas_call(
        functools.partial(_inproj_kernel, rope=rope, layer=layer),
        grid=(t // tm,), in_specs=in_specs, out_specs=out_specs, out_shape=out_shape,
        input_output_aliases=aliases, scratch_shapes=scratch,
        compiler_params=_cparams(1), name=name,
    )(*args)


GB = 256
SCAN_CHUNK = 8


def _lru_kernel(xr_ref, gy_ref, h0_ref, cw_ref, cb_ref, wg_ref, bg_ref, lam_ref,
                out_ref, ends_ref, a_s, u_s, h_s, p_s, xs_ref, xc_s):
    r = xr_ref.shape[0]
    steps = r // SUB
    sub = lax.broadcasted_iota(jnp.int32, (SUB, GB), 0)
    for cb in range(LRU_W // GB):
        cols = slice(cb * GB, (cb + 1) * GB)
        x = xr_ref[:, cols].astype(F32)
        xs_ref[SUB:r + SUB, :] = x
        xs_ref[0:SUB, :] = jnp.where(sub == 0, 0.0, pltpu.roll(x[r - SUB:r], 1, 0))
        for t in range(2):
            xs_ref[r + (1 + t) * SUB:r + (2 + t) * SUB, :] = jnp.where(
                sub == SUB - 1, 0.0, pltpu.roll(x[t * SUB:(t + 1) * SUB], SUB - 1, 0))
        xc = cb_ref[:, cols] + xs_ref[0:r, :] * cw_ref[0:1, cols]
        for t in range(1, 4):
            xc = xc + xs_ref[t * SUB:r + t * SUB, :] * cw_ref[t:t + 1, cols]
        xc_s[...] = xc
        lhs = xc_s[...].astype(BF16)
        xh = 0.5 * xc_s[...]
        for d in range(2):
            g = _dot(lhs, wg_ref[d, cb])
            tr = jnp.tanh(g[:, :GB] + bg_ref[d, 0:1, cols])
            ti = jnp.tanh(g[:, GB:] + bg_ref[d, 1:2, cols])
            lam = lam_ref[d:d + 1, cols]
            log_sig = jnp.minimum(lam, 0.0) - jnp.log(1.0 + jnp.exp(-jnp.abs(lam)))
            ch = (0.5 * LRU_C) * log_sig
            w = tr * (ch * LOG2E) + ch * LOG2E
            a = jnp.exp2(w)
            m2 = jnp.tanh(w * (-1.0 / LOG2E)) * (1.0 + a * a)
            mult = jnp.where(m2 > 0.0, m2 * lax.rsqrt(m2), 0.0)
            u = mult * (ti * xh + xh)
            a_s[d] = a
            u_s[d] = u
            e0 = 0 if d == 0 else r - SUB
            edge = sub == (0 if d == 0 else SUB - 1)
            ae = a_s[d, e0:e0 + SUB, :]
            u_s[d, e0:e0 + SUB, :] = u_s[d, e0:e0 + SUB, :] + jnp.where(edge, ae * h0_ref[d:d + 1, cols], 0.0)
            a_s[d, e0:e0 + SUB, :] = jnp.where(edge, 0.0, ae)

        def chunk(i, carry):
            hf, pf, hb, pb = carry
            base_f = pl.multiple_of(i * (SCAN_CHUNK * SUB), SCAN_CHUNK * SUB)
            base_b = pl.multiple_of((steps - SCAN_CHUNK) * SUB - i * (SCAN_CHUNK * SUB), SCAN_CHUNK * SUB)
            for s in range(SCAN_CHUNK):
                rf = pl.ds(base_f + s * SUB, SUB)
                rb = pl.ds(base_b + (SCAN_CHUNK - 1 - s) * SUB, SUB)
                af = a_s[0, rf, :]
                hf = af * hf + u_s[0, rf, :]
                pf = af * pf
                h_s[0, rf, :] = hf
                p_s[0, rf, :] = pf
                ab = a_s[1, rb, :]
                hb = ab * hb + u_s[1, rb, :]
                pb = ab * pb
                h_s[1, rb, :] = hb
                p_s[1, rb, :] = pb
            return hf, pf, hb, pb

        zero = jnp.zeros((SUB, GB), F32)
        one = jnp.ones((SUB, GB), F32)
        hf, pf, hb, pb = lax.fori_loop(0, steps // SCAN_CHUNK, chunk, (zero, one, zero, one))

        ef, eb = hf, hb
        for sh in (1, 2, 4):
            keep_f = sub >= sh
            ef = ef + pf * jnp.where(keep_f, pltpu.roll(ef, sh, 0), 0.0)
            pf = pf * jnp.where(keep_f, pltpu.roll(pf, sh, 0), 1.0)
            keep_b = sub < SUB - sh
            eb = eb + pb * jnp.where(keep_b, pltpu.roll(eb, SUB - sh, 0), 0.0)
            pb = pb * jnp.where(keep_b, pltpu.roll(pb, SUB - sh, 0), 1.0)
        ends_ref[0:SUB, cols] = ef
        ends_ref[SUB:2 * SUB, cols] = eb
        cf = jnp.where(sub >= 1, pltpu.roll(ef, 1, 0), 0.0)
        cbk = jnp.where(sub < SUB - 1, pltpu.roll(eb, SUB - 1, 0), 0.0)
        cf2 = jnp.concatenate([cf, cf], axis=0)
        cb2 = jnp.concatenate([cbk, cbk], axis=0)

        def fix(m, _):
            rows = pl.ds(pl.multiple_of(m * 2 * SUB, 2 * SUB), 2 * SUB)
            hft = h_s[0, rows, :] + p_s[0, rows, :] * cf2
            hbt = h_s[1, rows, :] + p_s[1, rows, :] * cb2
            out_ref[rows, cols] = ((hft + hbt) * gy_ref[rows, cols].astype(F32)).astype(BF16)
            return 0

        lax.fori_loop(0, r // (2 * SUB), fix, 0, unroll=4)


def _lru(xr, gy, h0, conv_w, conv_b, wg, bg, lam, *, seq, name):
    t = xr.shape[0]
    nseq = t // seq
    full = lambda shape: pl.BlockSpec(shape, lambda i: (0,) * len(shape))
    return pl.pallas_call(
        _lru_kernel,
        grid=(nseq,),
        in_specs=[pl.BlockSpec((seq, D), lambda i: (i, 0)), pl.BlockSpec((seq, D), lambda i: (i, 0)),
                  pl.BlockSpec((None, 2, LRU_W), lambda i: (i, 0, 0)),
                  full((4, LRU_W)), full((1, LRU_W)), full((2, LRU_W // GB, GB, 2 * GB)),
                  full((2, 2, LRU_W)), full((2, LRU_W))],
        out_specs=[pl.BlockSpec((seq, D), lambda i: (i, 0)),
                   pl.BlockSpec((None, 2 * SUB, LRU_W), lambda i: (i, 0, 0))],
        out_shape=[jax.ShapeDtypeStruct((t, D), BF16), jax.ShapeDtypeStruct((nseq, 2 * SUB, LRU_W), F32)],
        scratch_shapes=[pltpu.VMEM((2, seq, GB), F32)] * 4 +
                       [pltpu.VMEM((seq + 3 * SUB, GB), F32), pltpu.VMEM((seq, GB), F32)],
        compiler_params=_cparams(1), name=name,
    )(xr, gy, h0, conv_w, conv_b, wg, bg, lam)


POOL_G = 256


def _pool_kernel(up_ref, wp_ref, scale_ref, out_ref):
    r = up_ref.shape[0]
    steps = r // SUB
    row = lax.broadcasted_iota(jnp.int32, (r, 1), 0)
    t = (row & (SUB - 1)) * steps + jnp.right_shift(row, 3)
    for g in range(4):
        cols = slice(g * POOL_G, (g + 1) * POOL_G)
        x = up_ref[:, cols].astype(F32)
        half = 1 << g
        back, fwd = x, x
        for lvl in range(g):
            n = 1 << lvl
            back = back + _shift_prev(back, r, n)
            fwd = fwd + _shift_next(fwd, r, n)
        win = _shift_prev(back, r) + fwd
        cnt = jnp.minimum(t + half, r) - jnp.maximum(t - half, 0)
        d = (win / cnt.astype(F32) - x).astype(BF16)
        out_ref[:, cols] = (_dot(d, wp_ref[g]) * scale_ref[:, cols]).astype(BF16)


def _pool(up, w_pool, scale, *, layer, seq, name):
    t = up.shape[0]
    return pl.pallas_call(
        _pool_kernel,
        grid=(t // seq,),
        in_specs=[pl.BlockSpec((seq, D), lambda i: (i, 0)),
                  pl.BlockSpec((None, 4, POOL_G, POOL_G), lambda i: (layer, 0, 0, 0)),
                  pl.BlockSpec((1, D), lambda i: (0, 0))],
        out_specs=pl.BlockSpec((seq, D), lambda i: (i, 0)),
        out_shape=jax.ShapeDtypeStruct((t, D), BF16),
        compiler_params=_cparams(1), name=name,
    )(up, w_pool, scale)


ONES_ROWS = 16


def _attn_kernel(*refs, cached):
    transposed = cached
    if cached:
        q_ref, k_ref, v_ref, ck_ref, cv_ref, o_ref = refs
        kall = jnp.concatenate([ck_ref[...], k_ref[...]], axis=0)
        vall = jnp.concatenate([cv_ref[...], v_ref[...]], axis=0)
    else:
        q_ref, k_ref, v_ref, o_ref = refs
        kall = k_ref[...]
        vall = v_ref[...]
    half = HEAD_DIM
    lane = lax.broadcasted_iota(jnp.int32, (1, LANE), 1)

    def both_halves(x, own_low):
        own = jnp.where((lane < half) if own_low else (lane >= half), x, 0.0)
        oth = pltpu.roll(own, half, 1)
        lo, hi = (own, oth) if own_low else (oth, own)
        return lo.astype(BF16), hi.astype(BF16)

    if transposed:
        heads = []
        for kvh in range(N_KV):
            blk = slice((kvh // 2) * LANE, (kvh // 2 + 1) * LANE)
            ks = both_halves(kall[:, blk], kvh % 2 == 0)
            r0 = (kvh % 2) * HEAD_DIM
            v_t = vall[:, blk].T[r0:r0 + HEAD_DIM]
            v_t = jnp.concatenate([v_t, jnp.ones((ONES_ROWS, v_t.shape[1]), F32)], axis=0).astype(BF16)
            for pair in range(2):
                heads += [(kk, v_t, kvh * 4 * HEAD_DIM + pair * LANE) for kk in ks]

        def scores(h):
            kk, _, c0 = heads[h]
            return _dot_nt(kk, q_ref[:, c0:c0 + LANE])

        s_next = scores(0)
        outs = []
        for h, (_, v_t, c0) in enumerate(heads):
            s = s_next
            if h + 1 < len(heads):
                s_next = scores(h + 1)
            e = jnp.exp2(s - jnp.max(s, axis=0, keepdims=True)).astype(BF16)
            pv = _dot(v_t, e)
            outs.append(pv[:HEAD_DIM] / pv[HEAD_DIM:HEAD_DIM + 1])
            if h % 2 == 1:
                o_ref[:, c0:c0 + LANE] = jnp.concatenate(outs, axis=0).T.astype(BF16)
                outs = []
        return

    for kvh in range(N_KV):
        blk = slice((kvh // 2) * LANE, (kvh // 2 + 1) * LANE)
        ks = both_halves(kall[:, blk], kvh % 2 == 0)
        vs = both_halves(vall[:, blk], kvh % 2 == 0)
        for pair in range(2):
            c0 = kvh * 4 * HEAD_DIM + pair * LANE
            qp = q_ref[:, c0:c0 + LANE]
            acc = jnp.zeros((qp.shape[0], LANE), F32)
            for kk, vv in zip(ks, vs):
                s = _dot_nt(qp, kk)
                e = jnp.exp2(s - jnp.max(s, axis=-1, keepdims=True))
                l = jnp.sum(e, axis=-1, keepdims=True)
                acc = acc + _dot(e.astype(BF16), vv) / l
            o_ref[:, c0:c0 + LANE] = acc.astype(BF16)


def _attention(q, k, v, cache_k, cache_v, layer, *, seq, name):
    t = q.shape[0]
    cached = cache_k is not None
    if cached:
        kv_spec = pl.BlockSpec((seq, KV_W), lambda i: (i, 0))
    else:
        kv_spec = pl.BlockSpec((None, None, seq, KV_W), lambda i: (i, layer, 0, 0))
    in_specs = [pl.BlockSpec((seq, D), lambda i: (i, 0)), kv_spec, kv_spec]
    args = [q, k, v]
    if cached:
        in_specs += [pl.BlockSpec((None, None, PAST, KV_W), lambda i: (i, layer, 0, 0))] * 2
        args += [cache_k, cache_v]
    return pl.pallas_call(
        functools.partial(_attn_kernel, cached=cached),
        grid=(t // seq,), in_specs=in_specs,
        out_specs=pl.BlockSpec((seq, D), lambda i: (i, 0)),
        out_shape=jax.ShapeDtypeStruct((t, D), BF16),
        compiler_params=_cparams(1), name=name,
    )(*args)


def _merge_kernel(a_ref, o_ref, c_ref, gate_ref, x_ref, mod_ref, n2g_ref, wa_ref, wb_ref, wc_ref, wo_ref,
                  x1_ref, h2_ref):
    mix = gate_ref[:, 0:D].astype(F32) * _dot(a_ref[...], wa_ref[...])
    mix = mix + gate_ref[:, D:2 * D].astype(F32) * _dot(o_ref[...], wb_ref[...])
    mix = mix + gate_ref[:, 2 * D:3 * D].astype(F32) * _dot(c_ref[...], wc_ref[...])
    out = _dot(mix.astype(BF16), wo_ref[...])
    mod = mod_ref[...]
    g1, sh2, sc2 = mod[:, 2 * D:3 * D], mod[:, 3 * D:4 * D], mod[:, 4 * D:5 * D]
    x1 = x_ref[...] + g1 * out
    x1_ref[...] = x1
    h2_ref[...] = ((_rms(x1) * n2g_ref[...]) * (1.0 + sc2) + sh2).astype(BF16)


def _merge(a, o, c, gates, x, mod_l, n2g, wa, wb, wc, wo, *, layer, tm, row_of_tile, name):
    t = x.shape[0]
    row = lambda w: pl.BlockSpec((tm, w), lambda i: (i, 0))
    wspec = _resident((D, D), layer)
    return pl.pallas_call(
        _merge_kernel,
        grid=(t // tm,),
        in_specs=[row(D), row(D), row(D), row(3 * D), row(D),
                  pl.BlockSpec((None, 1, 6 * D), lambda i: (row_of_tile(i), 0, 0)),
                  pl.BlockSpec((1, D), lambda i: (0, 0)), wspec, wspec, wspec, wspec],
        out_specs=[row(D), row(D)],
        out_shape=[jax.ShapeDtypeStruct((t, D), F32), jax.ShapeDtypeStruct((t, D), BF16)],
        compiler_params=_cparams(1), name=name,
    )(a, o, c, gates, x, mod_l, n2g, wa, wb, wc, wo)


FF_TN = 1024
FF_SUB = 512
FF_ROWS = 1024
DOWN_TM = 1024


def _ffn_up_kernel(h_ref, wg_ref, wv_ref, cwg_ref, cwv_ref, cbg_ref, cbv_ref, out_ref, *, seq):
    h = h_ref[...]

    def conv(z, cw_ref, cb_ref, cols):
        y = cb_ref[:, cols] + _shift_prev(z, seq) * cw_ref[0:1, cols]
        y = y + z * cw_ref[1:2, cols]
        return y + _shift_next(z, seq) * cw_ref[2:3, cols]

    for c in range(FF_TN // FF_SUB):
        cols = slice(c * FF_SUB, (c + 1) * FF_SUB)
        g = conv(_dot(h, wg_ref[:, cols]), cwg_ref, cbg_ref, cols)
        v = conv(_dot(h, wv_ref[:, cols]), cwv_ref, cbv_ref, cols)
        hg = 0.5 * g
        out_ref[:, cols] = ((hg * jnp.tanh(hg) + hg) * v).astype(BF16)


def _ffn_up(h2, w_up, cw, cb, *, layer, seq, name):
    t = h2.shape[0]
    nj = D_FF // FF_TN
    return pl.pallas_call(
        functools.partial(_ffn_up_kernel, seq=seq),
        grid=(t // FF_ROWS, nj),
        in_specs=[pl.BlockSpec((FF_ROWS, D), lambda i, j: (i, 0)),
                  pl.BlockSpec((None, D, FF_TN), lambda i, j: (layer, 0, j)),
                  pl.BlockSpec((None, D, FF_TN), lambda i, j: (layer, 0, nj + j)),
                  pl.BlockSpec((3, FF_TN), lambda i, j: (0, j)),
                  pl.BlockSpec((3, FF_TN), lambda i, j: (0, nj + j)),
                  pl.BlockSpec((1, FF_TN), lambda i, j: (0, j)),
                  pl.BlockSpec((1, FF_TN), lambda i, j: (0, nj + j))],
        out_specs=pl.BlockSpec((FF_ROWS, FF_TN), lambda i, j: (i, j)),
        out_shape=jax.ShapeDtypeStruct((t, D_FF), BF16),
        compiler_params=_cparams(2), name=name,
    )(h2, w_up, w_up, cw, cw, cb, cb)


def _ffn_down_kernel(act_ref, w_ref, x1_ref, mod_ref, fg_ref, out_ref, *, final):
    g2 = mod_ref[:, 5 * D:6 * D]
    x2 = x1_ref[...] + g2 * _dot(act_ref[...], w_ref[...])
    out_ref[...] = _rms(x2) * fg_ref[...] if final else x2


def _ffn_down(act, w_down, x1, mod_l, fg, *, layer, tm, row_of_tile, final, name):
    t = x1.shape[0]
    return pl.pallas_call(
        functools.partial(_ffn_down_kernel, final=final),
        grid=(t // tm,),
        in_specs=[pl.BlockSpec((tm, D_FF), lambda i: (i, 0)),
                  _resident((D_FF, D), layer),
                  pl.BlockSpec((tm, D), lambda i: (i, 0)),
                  pl.BlockSpec((None, 1, 6 * D), lambda i: (row_of_tile(i), 0, 0)),
                  pl.BlockSpec((1, D), lambda i: (0, 0))],
        out_specs=pl.BlockSpec((tm, D), lambda i: (i, 0)),
        out_shape=jax.ShapeDtypeStruct((t, D), F32),
        compiler_params=_cparams(1), name=name,
    )(act, w_down, x1, mod_l, fg)


def _interleave(x):
    b, s, w = x.shape
    return x.reshape(b, SUB, s // SUB, w).transpose(0, 2, 1, 3).reshape(b * s, w)


def _deinterleave(y, b, s):
    return y.reshape(b, s // SUB, SUB, -1).transpose(0, 2, 1, 3).reshape(b, s, -1)


def _rope_tables():
    p = np.arange(S_LAT)
    t = (p % SUB) * (S_LAT // SUB) + p // SUB
    pos = np.stack([t // GRID_W, t % GRID_W], axis=1).astype(np.float32)
    quarter = HEAD_DIM // 4
    inv = (ROPE_BASE ** (-np.arange(quarter, dtype=np.float32) / quarter)).astype(np.float32)
    d = np.arange(LANE) % HEAD_DIM
    which = d // (HEAD_DIM // 2)
    e = d % (HEAD_DIM // 2)
    ang = pos[:, which] * inv[e % quarter][None, :]
    sign = np.where(e < quarter, -1.0, 1.0).astype(np.float32)
    return jnp.asarray(np.cos(ang).astype(np.float32)), jnp.asarray((np.sin(ang) * sign).astype(np.float32))


def _block_diag_gates(w_r, w_i):
    eye = jnp.eye(4, dtype=F32)

    def bd(w):
        w = w.reshape(2, 4, 4, LRU_BW, LRU_BW)
        w = w[:, :, :, :, None, :] * eye[None, None, :, None, :, None]
        return w.reshape(2, 4, GB, GB)

    return jnp.concatenate([bd(w_r), bd(w_i)], axis=-1).astype(BF16)


def _trunk(x_prompt, x_sample, cache_k, cache_v, state_lru, c, c_ctx, norm1_g, norm2_g, w_mod, b_mod, w_in,
           b_gate, conv_w, conv_b, w_rg, b_rg, w_ig, b_ig, lru_lambda, q_norm_g, k_norm_g, w_pool, pool_scale,
           w_br_a, w_br_b, w_br_c, w_o, w_up, ffn_conv_w, ffn_conv_b, w_down, final_norm_g,
           paths=("ctx", "lat")):
    cond = jnp.zeros((16, D), F32).at[0:N_LAT].set(c).at[N_LAT].set(c_ctx)
    mod = _modulation(cond, w_mod, b_mod)
    rope_tabs = _rope_tables()
    gmat = jnp.asarray(np.kron(np.eye(2), np.full((HEAD_DIM, HEAD_DIM), 1.0 / HEAD_DIM)), BF16)
    ck = cache_k.reshape(N_LAT, DEPTH, PAST, KV_W)
    cv = cache_v.reshape(N_LAT, DEPTH, PAST, KV_W)
    fg = final_norm_g.reshape(1, D)

    xs = {"ctx": _interleave(x_prompt), "lat": _interleave(x_sample)}
    cfg = {"ctx": dict(seq=S_CTX, tm=512, row=lambda i: N_LAT),
           "lat": dict(seq=S_LAT, tm=512, row=lambda i: i // (S_LAT // 512))}
    new_s = []
    kv_bufs = None
    w_in_b, wa, wb, wc, wo, w_up_b, w_down_b, wp = (
        w.astype(BF16) for w in (w_in, w_br_a, w_br_b, w_br_c, w_o, w_up, w_down, w_pool))
    for l in range(DEPTH):
        mod_l = mod[l].reshape(16, 1, 6 * D)
        wg = _block_diag_gates(0.5 * w_rg[l], 0.5 * w_ig[l])
        bg = 0.5 * jnp.stack([b_rg[l], b_ig[l]], axis=1)
        qg = jnp.tile(q_norm_g[l], 2).reshape(1, LANE)
        kg = jnp.tile(k_norm_g[l], 2).reshape(1, LANE)
        for path in paths:
            seq, tm, row = cfg[path]["seq"], cfg[path]["tm"], cfg[path]["row"]
            lat = path == "lat"
            x = xs[path]
            xr, gy, q, k, v, up, gates = _inproj(
                x, mod_l, norm1_g[l].reshape(1, D), w_in_b, b_gate[l].reshape(1, 3 * D), qg, kg, gmat,
                rope_tabs if lat else None, None if lat else kv_bufs, layer=l, tm=tm, row_of_tile=row,
                name=f"inproj_{path}{l}")
            if not lat:
                kv_bufs = (k, v)
            h0 = state_lru[:, l] if lat else jnp.zeros((N_CTX, 2, LRU_W), F32)
            a_pre, ends = _lru(xr, gy, h0, conv_w[l], conv_b[l].reshape(1, LRU_W), wg, bg, lru_lambda[l],
                               seq=seq, name=f"lru_{path}{l}")
            c_pre = _pool(up, wp, pool_scale[l].reshape(1, D), layer=l, seq=seq, name=f"pool_{path}{l}")
            o = _attention(q, k, v, ck if lat else None, cv if lat else None, l, seq=seq,
                           name=f"attn_{path}{l}")
            x1, h2 = _merge(a_pre, o, c_pre, gates, x, mod_l, norm2_g[l].reshape(1, D), wa, wb, wc, wo,
                            layer=l, tm=tm, row_of_tile=row, name=f"merge_{path}{l}")
            act = _ffn_up(h2, w_up_b, ffn_conv_w[l], ffn_conv_b[l].reshape(1, 2 * D_FF), layer=l, seq=seq,
                          name=f"ffn_up_{path}{l}")
            dn_row = (lambda i: i) if lat else row
            xs[path] = _ffn_down(act, w_down_b, x1, mod_l, fg, layer=l, tm=DOWN_TM, row_of_tile=dn_row,
                                 final=(l == DEPTH - 1), name=f"ffn_down_{path}{l}")
            if not lat:
                new_s.append(jnp.stack([ends[:, SUB - 1], ends[:, SUB]], axis=1))
    new_k, new_v = (b.reshape(N_CTX, DEPTH, S_CTX, N_KV, HEAD_DIM) for b in kv_bufs)
    return xs, new_k, new_v, new_s


def kernel(x_prompt, x_sample, cache_k, cache_v, state_lru, c, c_ctx, norm1_g, norm2_g, w_mod, b_mod, w_in,
           b_gate, conv_w, conv_b, w_rg, b_rg, w_ig, b_ig, lru_lambda, q_norm_g, k_norm_g, w_pool, pool_scale,
           w_br_a, w_br_b, w_br_c, w_o, w_up, ffn_conv_w, ffn_conv_b, w_down, final_norm_g):
    xs, new_k, new_v, new_s = _trunk(
        x_prompt, x_sample, cache_k, cache_v, state_lru, c, c_ctx, norm1_g, norm2_g, w_mod, b_mod, w_in,
        b_gate, conv_w, conv_b, w_rg, b_rg, w_ig, b_ig, lru_lambda, q_norm_g, k_norm_g, w_pool, pool_scale,
        w_br_a, w_br_b, w_br_c, w_o, w_up, ffn_conv_w, ffn_conv_b, w_down, final_norm_g)
    y_prompt = _deinterleave(xs["ctx"], N_CTX, S_CTX)
    y_sample = _deinterleave(xs["lat"], N_LAT, S_LAT)
    return (y_prompt, y_sample, new_k, new_v, jnp.stack(new_s, axis=1))
```

```python
import functools

import numpy as np
import jax
import jax.numpy as jnp
from jax import lax
from jax.experimental import pallas as pl
from jax.experimental.pallas import tpu as pltpu

F32 = jnp.float32
BF16 = jnp.bfloat16

D = 1024
DEPTH = 2
N_CTX, S_CTX = 16, 256
N_LAT, S_LAT = 8, 1024
PAST = 256
GRID_W = 64
LRU_W = 1024
LRU_BW = 64
LRU_C = 8.0
N_HEADS, N_KV, HEAD_DIM = 16, 4, 64
KV_W = N_KV * HEAD_DIM
ROPE_BASE = 10000.0
D_FF = 3 * D
EPS = 1e-6
SUB = 8
LANE = 128
ROPE_SHIFT = HEAD_DIM // 4
C_XR, C_YR, C_Q, C_K, C_V, C_UP, C_GL, C_END = 0, 1024, 2048, 3072, 3328, 3584, 4608, 7680

LOG2E = float(np.log2(np.e))
Q_PRESCALE = LOG2E * HEAD_DIM ** -0.5

VMEM_LIMIT = 56 * 1024 * 1024


def _cparams(n_axes):
    return pltpu.CompilerParams(dimension_semantics=("arbitrary",) * n_axes,
                                vmem_limit_bytes=VMEM_LIMIT)


def _resident(shape, layer):
    return pl.BlockSpec((None,) + shape, lambda *_: (layer,) + (0,) * len(shape), pipeline_mode=pl.Buffered(1))


def _dot(a, b):
    return jnp.dot(a, b, preferred_element_type=F32)


def _dot_nt(a, b):
    return lax.dot_general(a, b, (((1,), (1,)), ((), ())), preferred_element_type=F32)


def _rms(x):
    return x * lax.rsqrt(jnp.mean(x * x, axis=-1, keepdims=True) + EPS)


def _shift_prev(x, rm, n=1):
    r, w = x.shape
    sub = lax.broadcasted_iota(jnp.int32, (SUB, w), 0)
    pieces = []
    for m in range(r // rm):
        base = m * rm
        for t in range(n):
            lo = base + rm - SUB * (n - t)
            pieces.append(jnp.where(sub == 0, 0.0, pltpu.roll(x[lo:lo + SUB], 1, 0)))
        pieces.append(x[base:base + rm - SUB * n])
    return jnp.concatenate(pieces, axis=0)


def _shift_next(x, rm, n=1):
    r, w = x.shape
    sub = lax.broadcasted_iota(jnp.int32, (SUB, w), 0)
    pieces = []
    for m in range(r // rm):
        base = m * rm
        pieces.append(x[base + SUB * n:base + rm])
        for t in range(n):
            lo = base + SUB * t
            pieces.append(jnp.where(sub == SUB - 1, 0.0, pltpu.roll(x[lo:lo + SUB], SUB - 1, 0)))
    return jnp.concatenate(pieces, axis=0)


def _mod_kernel(cond_ref, w_ref, b_ref, out_ref):
    c = cond_ref[...]
    s = (c * jax.nn.sigmoid(c)).astype(BF16)
    out_ref[...] = _dot(s, w_ref[...].astype(BF16)) + b_ref[...]


def _modulation(cond, w_mod, b_mod):
    tn = 1536
    return pl.pallas_call(
        _mod_kernel,
        grid=(DEPTH, 6 * D // tn),
        in_specs=[pl.BlockSpec((16, D), lambda l, j: (0, 0)),
                  pl.BlockSpec((None, D, tn), lambda l, j: (l, 0, j)),
                  pl.BlockSpec((None, 1, tn), lambda l, j: (l, 0, j))],
        out_specs=pl.BlockSpec((None, 16, tn), lambda l, j: (l, 0, j)),
        out_shape=jax.ShapeDtypeStruct((DEPTH, 16, 6 * D), F32),
        compiler_params=_cparams(2),
        name="modulation",
    )(cond, w_mod, b_mod.reshape(DEPTH, 1, 6 * D))


def _inproj_kernel(*refs, rope, layer):
    if rope:
        (x_ref, mod_ref, n1g_ref, w_ref, bgate_ref, qg_ref, kg_ref, gmat_ref, cos_ref, sin_ref,
         xr_ref, gy_ref, q_ref, k_ref, v_ref, up_ref, gate_ref) = refs
    else:
        if layer == 0:
            (x_ref, mod_ref, n1g_ref, w_ref, bgate_ref, qg_ref, kg_ref, gmat_ref,
             xr_ref, gy_ref, q_ref, kc_ref, vc_ref, up_ref, gate_ref, stage_ref) = refs
            for dst in (kc_ref, vc_ref):
                for other in range(1, DEPTH):
                    dst[:, other] = jnp.zeros((dst.shape[0],) + dst.shape[2:], F32)
            kc_ref, vc_ref = kc_ref.at[:, 0], vc_ref.at[:, 0]
        else:
            (x_ref, mod_ref, n1g_ref, w_ref, bgate_ref, qg_ref, kg_ref, gmat_ref, _, _,
             xr_ref, gy_ref, q_ref, kc_ref, vc_ref, up_ref, gate_ref, stage_ref) = refs
    mod = mod_ref[...]
    sh1, sc1 = mod[:, 0:D], mod[:, D:2 * D]
    h = ((_rms(x_ref[...]) * n1g_ref[...]) * (1.0 + sc1) + sh1).astype(BF16)

    def mm(c0, c1):
        return _dot(h, w_ref[:, c0:c1])

    for g in range(3):
        z = mm(C_GL + g * D, C_GL + (g + 1) * D) + bgate_ref[:, g * D:(g + 1) * D]
        gate_ref[:, g * D:(g + 1) * D] = jax.nn.sigmoid(z).astype(BF16)
    gy_ref[...] = jax.nn.gelu(mm(C_YR, C_Q)).astype(BF16)

    lane = lax.broadcasted_iota(jnp.int32, (1, LANE), 1)
    first = (lane & ROPE_SHIFT) == 0

    def head_norm(xb, g):
        ms = _dot((xb * xb).astype(BF16), gmat_ref[...])
        y = (xb * lax.rsqrt(ms + EPS)) * g
        if rope:
            partner = jnp.where(first, pltpu.roll(y, LANE - ROPE_SHIFT, 1), pltpu.roll(y, ROPE_SHIFT, 1))
            y = y * cos_ref[...] + partner * sin_ref[...]
        return y

    qa = mm(C_Q, C_K)
    for c in range(N_HEADS * HEAD_DIM // LANE):
        qn = head_norm(qa[:, c * LANE:(c + 1) * LANE], qg_ref[...])
        q_ref[:, c * LANE:(c + 1) * LANE] = (qn * Q_PRESCALE).astype(BF16)
    kv_slabs = KV_W // LANE
    ka = mm(C_K, C_V)
    for c in range(kv_slabs):
        kn = head_norm(ka[:, c * LANE:(c + 1) * LANE], kg_ref[...])
        if rope:
            k_ref[:, c * LANE:(c + 1) * LANE] = kn
        else:
            stage_ref[c] = kn
    up_ref[...] = mm(C_UP, C_GL).astype(BF16)
    xr_ref[...] = mm(C_XR, C_YR).astype(BF16)
    va = mm(C_V, C_UP)
    if rope:
        v_ref[...] = va
    else:
        steps = S_CTX // SUB
        for c in range(kv_slabs):
            stage_ref[kv_slabs + c] = va[:, c * LANE:(c + 1) * LANE]
        for slab, dst in enumerate([kc_ref] * kv_slabs + [vc_ref] * kv_slabs):
            cols = slice((slab % kv_slabs) * LANE, (slab % kv_slabs + 1) * LANE)
            for s in range(x_ref.shape[0] // S_CTX):
                for j in range(SUB):
                    dst[s, j * steps:(j + 1) * steps, cols] = (
                        stage_ref[slab, pl.ds(s * S_CTX + j, steps, stride=SUB), :])


def _inproj(x, mod_l, n1g, w_in, b_gate, qg, kg, gmat, rope_tabs, kv_bufs, *, layer, tm, row_of_tile, name):
    t = x.shape[0]
    rope = rope_tabs is not None
    full = lambda shape: pl.BlockSpec(shape, lambda i: (0,) * len(shape))
    in_specs = [pl.BlockSpec((tm, D), lambda i: (i, 0)),
                pl.BlockSpec((None, 1, 6 * D), lambda i: (row_of_tile(i), 0, 0)),
                full((1, D)), _resident((D, C_END), layer), full((1, 3 * D)), full((1, LANE)), full((1, LANE)),
                full((LANE, LANE))]
    args = [x, mod_l, n1g, w_in, b_gate, qg, kg, gmat]
    if rope:
        per_seq = S_LAT // tm
        in_specs += [pl.BlockSpec((tm, LANE), lambda i: (i % per_seq, 0))] * 2
        args += list(rope_tabs)
    row = lambda w: pl.BlockSpec((tm, w), lambda i: (i, 0))
    act = lambda w: jax.ShapeDtypeStruct((t, w), BF16)
    if rope:
        kv_specs = [row(KV_W), row(KV_W)]
        kv_shapes = [jax.ShapeDtypeStruct((t, KV_W), F32)] * 2
        aliases, scratch = {}, []
    else:
        kv_shapes = [jax.ShapeDtypeStruct((N_CTX, DEPTH, S_CTX, KV_W), F32)] * 2
        if layer == 0:
            kv_specs = [pl.BlockSpec((tm // S_CTX, DEPTH, S_CTX, KV_W), lambda i: (i, 0, 0, 0))] * 2
            aliases = {}
        else:
            in_specs += [pl.BlockSpec(memory_space=pl.ANY)] * 2
            args += list(kv_bufs)
            kv_specs = [pl.BlockSpec((tm // S_CTX, None, S_CTX, KV_W), lambda i: (i, layer, 0, 0))] * 2
            aliases = {len(args) - 2: 3, len(args) - 1: 4}
        scratch = [pltpu.VMEM((2 * KV_W // LANE, tm, LANE), F32)]
    out_specs = [row(D), row(D), row(D)] + kv_specs + [row(D), row(3 * D)]
    out_shape = [act(D), act(D), act(D)] + kv_shapes + [act(D), act(3 * D)]
    return pl.pallas_call(
        functools.partial(_inproj_kernel, rope=rope, layer=layer),
        grid=(t // tm,), in_specs=in_specs, out_specs=out_specs, out_shape=out_shape,
        input_output_aliases=aliases, scratch_shapes=scratch,
        compiler_params=_cparams(1), name=name,
    )(*args)


GB = 256
SCAN_CHUNK = 8


def _lru_kernel(xr_ref, gy_ref, h0_ref, cw_ref, cb_ref, wg_ref, bg_ref, lam_ref,
                out_ref, ends_ref, a_s, u_s, h_s, p_s, xs_ref, xc_s):
    r = xr_ref.shape[0]
    steps = r // SUB
    sub = lax.broadcasted_iota(jnp.int32, (SUB, GB), 0)
    for cb in range(LRU_W // GB):
        cols = slice(cb * GB, (cb + 1) * GB)
        x = xr_ref[:, cols].astype(F32)
        xs_ref[SUB:r + SUB, :] = x
        xs_ref[0:SUB, :] = jnp.where(sub == 0, 0.0, pltpu.roll(x[r - SUB:r], 1, 0))
        for t in range(2):
            xs_ref[r + (1 + t) * SUB:r + (2 + t) * SUB, :] = jnp.where(
                sub == SUB - 1, 0.0, pltpu.roll(x[t * SUB:(t + 1) * SUB], SUB - 1, 0))
        xc = cb_ref[:, cols] + xs_ref[0:r, :] * cw_ref[0:1, cols]
        for t in range(1, 4):
            xc = xc + xs_ref[t * SUB:r + t * SUB, :] * cw_ref[t:t + 1, cols]
        xc_s[...] = xc
        lhs = xc_s[...].astype(BF16)
        xh = 0.5 * xc_s[...]
        for d in range(2):
            g = _dot(lhs, wg_ref[d, cb])
            tr = jnp.tanh(g[:, :GB] + bg_ref[d, 0:1, cols])
            ti = jnp.tanh(g[:, GB:] + bg_ref[d, 1:2, cols])
            lam = lam_ref[d:d + 1, cols]
            log_sig = jnp.minimum(lam, 0.0) - jnp.log(1.0 + jnp.exp(-jnp.abs(lam)))
            ch = (0.5 * LRU_C) * log_sig
            w = tr * (ch * LOG2E) + ch * LOG2E
            a = jnp.exp2(w)
            m2 = jnp.tanh(w * (-1.0 / LOG2E)) * (1.0 + a * a)
            mult = jnp.where(m2 > 0.0, m2 * lax.rsqrt(m2), 0.0)
            u = mult * (ti * xh + xh)
            a_s[d] = a
            u_s[d] = u
            e0 = 0 if d == 0 else r - SUB
            edge = sub == (0 if d == 0 else SUB - 1)
            ae = a_s[d, e0:e0 + SUB, :]
            u_s[d, e0:e0 + SUB, :] = u_s[d, e0:e0 + SUB, :] + jnp.where(edge, ae * h0_ref[d:d + 1, cols], 0.0)
            a_s[d, e0:e0 + SUB, :] = jnp.where(edge, 0.0, ae)

        def chunk(i, carry):
            hf, pf, hb, pb = carry
            base_f = pl.multiple_of(i * (SCAN_CHUNK * SUB), SCAN_CHUNK * SUB)
            base_b = pl.multiple_of((steps - SCAN_CHUNK) * SUB - i * (SCAN_CHUNK * SUB), SCAN_CHUNK * SUB)
            for s in range(SCAN_CHUNK):
                rf = pl.ds(base_f + s * SUB, SUB)
                rb = pl.ds(base_b + (SCAN_CHUNK - 1 - s) * SUB, SUB)
                af = a_s[0, rf, :]
                hf = af * hf + u_s[0, rf, :]
                pf = af * pf
                h_s[0, rf, :] = hf
                p_s[0, rf, :] = pf
                ab = a_s[1, rb, :]
                hb = ab * hb + u_s[1, rb, :]
                pb = ab * pb
                h_s[1, rb, :] = hb
                p_s[1, rb, :] = pb
            return hf, pf, hb, pb

        zero = jnp.zeros((SUB, GB), F32)
        one = jnp.ones((SUB, GB), F32)
        hf, pf, hb, pb = lax.fori_loop(0, steps // SCAN_CHUNK, chunk, (zero, one, zero, one))

        ef, eb = hf, hb
        for sh in (1, 2, 4):
            keep_f = sub >= sh
            ef = ef + pf * jnp.where(keep_f, pltpu.roll(ef, sh, 0), 0.0)
            pf = pf * jnp.where(keep_f, pltpu.roll(pf, sh, 0), 1.0)
            keep_b = sub < SUB - sh
            eb = eb + pb * jnp.where(keep_b, pltpu.roll(eb, SUB - sh, 0), 0.0)
            pb = pb * jnp.where(keep_b, pltpu.roll(pb, SUB - sh, 0), 1.0)
        ends_ref[0:SUB, cols] = ef
        ends_ref[SUB:2 * SUB, cols] = eb
        cf = jnp.where(sub >= 1, pltpu.roll(ef, 1, 0), 0.0)
        cbk = jnp.where(sub < SUB - 1, pltpu.roll(eb, SUB - 1, 0), 0.0)
        cf2 = jnp.concatenate([cf, cf], axis=0)
        cb2 = jnp.concatenate([cbk, cbk], axis=0)

        def fix(m, _):
            rows = pl.ds(pl.multiple_of(m * 2 * SUB, 2 * SUB), 2 * SUB)
            hft = h_s[0, rows, :] + p_s[0, rows, :] * cf2
            hbt = h_s[1, rows, :] + p_s[1, rows, :] * cb2
            out_ref[rows, cols] = ((hft + hbt) * gy_ref[rows, cols].astype(F32)).astype(BF16)
            return 0

        lax.fori_loop(0, r // (2 * SUB), fix, 0, unroll=4)


def _lru(xr, gy, h0, conv_w, conv_b, wg, bg, lam, *, seq, name):
    t = xr.shape[0]
    nseq = t // seq
    full = lambda shape: pl.BlockSpec(shape, lambda i: (0,) * len(shape))
    return pl.pallas_call(
        _lru_kernel,
        grid=(nseq,),
        in_specs=[pl.BlockSpec((seq, D), lambda i: (i, 0)), pl.BlockSpec((seq, D), lambda i: (i, 0)),
                  pl.BlockSpec((None, 2, LRU_W), lambda i: (i, 0, 0)),
                  full((4, LRU_W)), full((1, LRU_W)), full((2, LRU_W // GB, GB, 2 * GB)),
                  full((2, 2, LRU_W)), full((2, LRU_W))],
        out_specs=[pl.BlockSpec((seq, D), lambda i: (i, 0)),
                   pl.BlockSpec((None, 2 * SUB, LRU_W), lambda i: (i, 0, 0))],
        out_shape=[jax.ShapeDtypeStruct((t, D), BF16), jax.ShapeDtypeStruct((nseq, 2 * SUB, LRU_W), F32)],
        scratch_shapes=[pltpu.VMEM((2, seq, GB), F32)] * 4 +
                       [pltpu.VMEM((seq + 3 * SUB, GB), F32), pltpu.VMEM((seq, GB), F32)],
        compiler_params=_cparams(1), name=name,
    )(xr, gy, h0, conv_w, conv_b, wg, bg, lam)


POOL_G = 256


def _pool_kernel(up_ref, wp_ref, scale_ref, out_ref):
    r = up_ref.shape[0]
    steps = r // SUB
    row = lax.broadcasted_iota(jnp.int32, (r, 1), 0)
    t = (row & (SUB - 1)) * steps + jnp.right_shift(row, 3)
    for g in range(4):
        cols = slice(g * POOL_G, (g + 1) * POOL_G)
        x = up_ref[:, cols].astype(F32)
        half = 1 << g
        back, fwd = x, x
        for lvl in range(g):
            n = 1 << lvl
            back = back + _shift_prev(back, r, n)
            fwd = fwd + _shift_next(fwd, r, n)
        win = _shift_prev(back, r) + fwd
        cnt = jnp.minimum(t + half, r) - jnp.maximum(t - half, 0)
        d = (win / cnt.astype(F32) - x).astype(BF16)
        out_ref[:, cols] = (_dot(d, wp_ref[g]) * scale_ref[:, cols]).astype(BF16)


def _pool(up, w_pool, scale, *, layer, seq, name):
    t = up.shape[0]
    return pl.pallas_call(
        _pool_kernel,
        grid=(t // seq,),
        in_specs=[pl.BlockSpec((seq, D), lambda i: (i, 0)),
                  pl.BlockSpec((None, 4, POOL_G, POOL_G), lambda i: (layer, 0, 0, 0)),
                  pl.BlockSpec((1, D), lambda i: (0, 0))],
        out_specs=pl.BlockSpec((seq, D), lambda i: (i, 0)),
        out_shape=jax.ShapeDtypeStruct((t, D), BF16),
        compiler_params=_cparams(1), name=name,
    )(up, w_pool, scale)


ONES_ROWS = 16


def _attn_kernel(*refs, cached):
    transposed = cached
    if cached:
        q_ref, k_ref, v_ref, ck_ref, cv_ref, o_ref = refs
        kall = jnp.concatenate([ck_ref[...], k_ref[...]], axis=0)
        vall = jnp.concatenate([cv_ref[...], v_ref[...]], axis=0)
    else:
        q_ref, k_ref, v_ref, o_ref = refs
        kall = k_ref[...]
        vall = v_ref[...]
    half = HEAD_DIM
    lane = lax.broadcasted_iota(jnp.int32, (1, LANE), 1)

    def both_halves(x, own_low):
        own = jnp.where((lane < half) if own_low else (lane >= half), x, 0.0)
        oth = pltpu.roll(own, half, 1)
        lo, hi = (own, oth) if own_low else (oth, own)
        return lo.astype(BF16), hi.astype(BF16)

    if transposed:
        heads = []
        for kvh in range(N_KV):
            blk = slice((kvh // 2) * LANE, (kvh // 2 + 1) * LANE)
            ks = both_halves(kall[:, blk], kvh % 2 == 0)
            r0 = (kvh % 2) * HEAD_DIM
            v_t = vall[:, blk].T[r0:r0 + HEAD_DIM]
            v_t = jnp.concatenate([v_t, jnp.ones((ONES_ROWS, v_t.shape[1]), F32)], axis=0).astype(BF16)
            for pair in range(2):
                heads += [(kk, v_t, kvh * 4 * HEAD_DIM + pair * LANE) for kk in ks]

        def scores(h):
            kk, _, c0 = heads[h]
            return _dot_nt(kk, q_ref[:, c0:c0 + LANE])

        s_next = scores(0)
        outs = []
        for h, (_, v_t, c0) in enumerate(heads):
            s = s_next
            if h + 1 < len(heads):
                s_next = scores(h + 1)
            e = jnp.exp2((s - jnp.max(s, axis=0, keepdims=True)).astype(BF16))
            pv = _dot(v_t, e)
            outs.append(pv[:HEAD_DIM] / pv[HEAD_DIM:HEAD_DIM + 1])
            if h % 2 == 1:
                o_ref[:, c0:c0 + LANE] = jnp.concatenate(outs, axis=0).T.astype(BF16)
                outs = []
        return

    for kvh in range(N_KV):
        blk = slice((kvh // 2) * LANE, (kvh // 2 + 1) * LANE)
        ks = both_halves(kall[:, blk], kvh % 2 == 0)
        vs = both_halves(vall[:, blk], kvh % 2 == 0)
        for pair in range(2):
            c0 = kvh * 4 * HEAD_DIM + pair * LANE
            qp = q_ref[:, c0:c0 + LANE]
            acc = jnp.zeros((qp.shape[0], LANE), F32)
            for kk, vv in zip(ks, vs):
                s = _dot_nt(qp, kk)
                e = jnp.exp2(s - jnp.max(s, axis=-1, keepdims=True))
                l = jnp.sum(e, axis=-1, keepdims=True)
                acc = acc + _dot(e.astype(BF16), vv) / l
            o_ref[:, c0:c0 + LANE] = acc.astype(BF16)


def _attention(q, k, v, cache_k, cache_v, layer, *, seq, name):
    t = q.shape[0]
    cached = cache_k is not None
    if cached:
        kv_spec = pl.BlockSpec((seq, KV_W), lambda i: (i, 0))
    else:
        kv_spec = pl.BlockSpec((None, None, seq, KV_W), lambda i: (i, layer, 0, 0))
    in_specs = [pl.BlockSpec((seq, D), lambda i: (i, 0)), kv_spec, kv_spec]
    args = [q, k, v]
    if cached:
        in_specs += [pl.BlockSpec((None, None, PAST, KV_W), lambda i: (i, layer, 0, 0))] * 2
        args += [cache_k, cache_v]
    return pl.pallas_call(
        functools.partial(_attn_kernel, cached=cached),
        grid=(t // seq,), in_specs=in_specs,
        out_specs=pl.BlockSpec((seq, D), lambda i: (i, 0)),
        out_shape=jax.ShapeDtypeStruct((t, D), BF16),
        compiler_params=_cparams(1), name=name,
    )(*args)


def _merge_kernel(a_ref, o_ref, c_ref, gate_ref, x_ref, mod_ref, n2g_ref, wa_ref, wb_ref, wc_ref, wo_ref,
                  x1_ref, h2_ref):
    mix = gate_ref[:, 0:D].astype(F32) * _dot(a_ref[...], wa_ref[...])
    mix = mix + gate_ref[:, D:2 * D].astype(F32) * _dot(o_ref[...], wb_ref[...])
    mix = mix + gate_ref[:, 2 * D:3 * D].astype(F32) * _dot(c_ref[...], wc_ref[...])
    out = _dot(mix.astype(BF16), wo_ref[...])
    mod = mod_ref[...]
    g1, sh2, sc2 = mod[:, 2 * D:3 * D], mod[:, 3 * D:4 * D], mod[:, 4 * D:5 * D]
    x1 = x_ref[...] + g1 * out
    x1_ref[...] = x1
    h2_ref[...] = ((_rms(x1) * n2g_ref[...]) * (1.0 + sc2) + sh2).astype(BF16)


def _merge(a, o, c, gates, x, mod_l, n2g, wa, wb, wc, wo, *, layer, tm, row_of_tile, name):
    t = x.shape[0]
    row = lambda w: pl.BlockSpec((tm, w), lambda i: (i, 0))
    wspec = _resident((D, D), layer)
    return pl.pallas_call(
        _merge_kernel,
        grid=(t // tm,),
        in_specs=[row(D), row(D), row(D), row(3 * D), row(D),
                  pl.BlockSpec((None, 1, 6 * D), lambda i: (row_of_tile(i), 0, 0)),
                  pl.BlockSpec((1, D), lambda i: (0, 0)), wspec, wspec, wspec, wspec],
        out_specs=[row(D), row(D)],
        out_shape=[jax.ShapeDtypeStruct((t, D), F32), jax.ShapeDtypeStruct((t, D), BF16)],
        compiler_params=_cparams(1), name=name,
    )(a, o, c, gates, x, mod_l, n2g, wa, wb, wc, wo)


FF_TN = 1024
FF_SUB = 512
FF_ROWS = 1024


def _ffn_up_kernel(h_ref, wg_ref, wv_ref, cwg_ref, cwv_ref, cbg_ref, cbv_ref, out_ref, *, seq):
    h = h_ref[...]

    def conv(z, cw_ref, cb_ref, cols):
        y = cb_ref[:, cols] + _shift_prev(z, seq) * cw_ref[0:1, cols]
        y = y + z * cw_ref[1:2, cols]
        return y + _shift_next(z, seq) * cw_ref[2:3, cols]

    for c in range(FF_TN // FF_SUB):
        cols = slice(c * FF_SUB, (c + 1) * FF_SUB)
        g = conv(_dot(h, wg_ref[:, cols]), cwg_ref, cbg_ref, cols)
        v = conv(_dot(h, wv_ref[:, cols]), cwv_ref, cbv_ref, cols)
        hg = 0.5 * g
        out_ref[:, cols] = ((hg * jnp.tanh(hg) + hg) * v).astype(BF16)


def _ffn_up(h2, w_up, cw, cb, *, layer, seq, name):
    t = h2.shape[0]
    nj = D_FF // FF_TN
    return pl.pallas_call(
        functools.partial(_ffn_up_kernel, seq=seq),
        grid=(t // FF_ROWS, nj),
        in_specs=[pl.BlockSpec((FF_ROWS, D), lambda i, j: (i, 0)),
                  pl.BlockSpec((None, D, FF_TN), lambda i, j: (layer, 0, j)),
                  pl.BlockSpec((None, D, FF_TN), lambda i, j: (layer, 0, nj + j)),
                  pl.BlockSpec((3, FF_TN), lambda i, j: (0, j)),
                  pl.BlockSpec((3, FF_TN), lambda i, j: (0, nj + j)),
                  pl.BlockSpec((1, FF_TN), lambda i, j: (0, j)),
                  pl.BlockSpec((1, FF_TN), lambda i, j: (0, nj + j))],
        out_specs=pl.BlockSpec((FF_ROWS, FF_TN), lambda i, j: (i, j)),
        out_shape=jax.ShapeDtypeStruct((t, D_FF), BF16),
        compiler_params=_cparams(2), name=name,
    )(h2, w_up, w_up, cw, cw, cb, cb)


def _ffn_down_kernel(act_ref, w_ref, x1_ref, mod_ref, fg_ref, out_ref, *, final):
    g2 = mod_ref[:, 5 * D:6 * D]
    x2 = x1_ref[...] + g2 * _dot(act_ref[...], w_ref[...])
    out_ref[...] = _rms(x2) * fg_ref[...] if final else x2


def _ffn_down(act, w_down, x1, mod_l, fg, *, layer, tm, row_of_tile, final, name):
    t = x1.shape[0]
    return pl.pallas_call(
        functools.partial(_ffn_down_kernel, final=final),
        grid=(t // tm,),
        in_specs=[pl.BlockSpec((tm, D_FF), lambda i: (i, 0)),
                  _resident((D_FF, D), layer),
                  pl.BlockSpec((tm, D), lambda i: (i, 0)),
                  pl.BlockSpec((None, 1, 6 * D), lambda i: (row_of_tile(i), 0, 0)),
                  pl.BlockSpec((1, D), lambda i: (0, 0))],
        out_specs=pl.BlockSpec((tm, D), lambda i: (i, 0)),
        out_shape=jax.ShapeDtypeStruct((t, D), F32),
        compiler_params=_cparams(1), name=name,
    )(act, w_down, x1, mod_l, fg)


def _interleave(x):
    b, s, w = x.shape
    return x.reshape(b, SUB, s // SUB, w).transpose(0, 2, 1, 3).reshape(b * s, w)


def _deinterleave(y, b, s):
    return y.reshape(b, s // SUB, SUB, -1).transpose(0, 2, 1, 3).reshape(b, s, -1)


def _rope_tables():
    p = np.arange(S_LAT)
    t = (p % SUB) * (S_LAT // SUB) + p // SUB
    pos = np.stack([t // GRID_W, t % GRID_W], axis=1).astype(np.float32)
    quarter = HEAD_DIM // 4
    inv = (ROPE_BASE ** (-np.arange(quarter, dtype=np.float32) / quarter)).astype(np.float32)
    d = np.arange(LANE) % HEAD_DIM
    which = d // (HEAD_DIM // 2)
    e = d % (HEAD_DIM // 2)
    ang = pos[:, which] * inv[e % quarter][None, :]
    sign = np.where(e < quarter, -1.0, 1.0).astype(np.float32)
    return jnp.asarray(np.cos(ang).astype(np.float32)), jnp.asarray((np.sin(ang) * sign).astype(np.float32))


def _block_diag_gates(w_r, w_i):
    eye = jnp.eye(4, dtype=F32)

    def bd(w):
        w = w.reshape(2, 4, 4, LRU_BW, LRU_BW)
        w = w[:, :, :, :, None, :] * eye[None, None, :, None, :, None]
        return w.reshape(2, 4, GB, GB)

    return jnp.concatenate([bd(w_r), bd(w_i)], axis=-1).astype(BF16)


def _trunk(x_prompt, x_sample, cache_k, cache_v, state_lru, c, c_ctx, norm1_g, norm2_g, w_mod, b_mod, w_in,
           b_gate, conv_w, conv_b, w_rg, b_rg, w_ig, b_ig, lru_lambda, q_norm_g, k_norm_g, w_pool, pool_scale,
           w_br_a, w_br_b, w_br_c, w_o, w_up, ffn_conv_w, ffn_conv_b, w_down, final_norm_g,
           paths=("ctx", "lat")):
    cond = jnp.zeros((16, D), F32).at[0:N_LAT].set(c).at[N_LAT].set(c_ctx)
    mod = _modulation(cond, w_mod, b_mod)
    rope_tabs = _rope_tables()
    gmat = jnp.asarray(np.kron(np.eye(2), np.full((HEAD_DIM, HEAD_DIM), 1.0 / HEAD_DIM)), BF16)
    ck = cache_k.reshape(N_LAT, DEPTH, PAST, KV_W)
    cv = cache_v.reshape(N_LAT, DEPTH, PAST, KV_W)
    fg = final_norm_g.reshape(1, D)

    xs = {"ctx": _interleave(x_prompt), "lat": _interleave(x_sample)}
    cfg = {"ctx": dict(seq=S_CTX, tm=512, row=lambda i: N_LAT),
           "lat": dict(seq=S_LAT, tm=512, row=lambda i: i // (S_LAT // 512))}
    new_s = []
    kv_bufs = None
    w_in_b, wa, wb, wc, wo, w_up_b, w_down_b, wp = (
        w.astype(BF16) for w in (w_in, w_br_a, w_br_b, w_br_c, w_o, w_up, w_down, w_pool))
    for l in range(DEPTH):
        mod_l = mod[l].reshape(16, 1, 6 * D)
        wg = _block_diag_gates(0.5 * w_rg[l], 0.5 * w_ig[l])
        bg = 0.5 * jnp.stack([b_rg[l], b_ig[l]], axis=1)
        qg = jnp.tile(q_norm_g[l], 2).reshape(1, LANE)
        kg = jnp.tile(k_norm_g[l], 2).reshape(1, LANE)
        for path in paths:
            seq, tm, row = cfg[path]["seq"], cfg[path]["tm"], cfg[path]["row"]
            lat = path == "lat"
            x = xs[path]
            xr, gy, q, k, v, up, gates = _inproj(
                x, mod_l, norm1_g[l].reshape(1, D), w_in_b, b_gate[l].reshape(1, 3 * D), qg, kg, gmat,
                rope_tabs if lat else None, None if lat else kv_bufs, layer=l, tm=tm, row_of_tile=row,
                name=f"inproj_{path}{l}")
            if not lat:
                kv_bufs = (k, v)
            h0 = state_lru[:, l] if lat else jnp.zeros((N_CTX, 2, LRU_W), F32)
            a_pre, ends = _lru(xr, gy, h0, conv_w[l], conv_b[l].reshape(1, LRU_W), wg, bg, lru_lambda[l],
                               seq=seq, name=f"lru_{path}{l}")
            c_pre = _pool(up, wp, pool_scale[l].reshape(1, D), layer=l, seq=seq, name=f"pool_{path}{l}")
            o = _attention(q, k, v, ck if lat else None, cv if lat else None, l, seq=seq,
                           name=f"attn_{path}{l}")
            x1, h2 = _merge(a_pre, o, c_pre, gates, x, mod_l, norm2_g[l].reshape(1, D), wa, wb, wc, wo,
                            layer=l, tm=tm, row_of_tile=row, name=f"merge_{path}{l}")
            act = _ffn_up(h2, w_up_b, ffn_conv_w[l], ffn_conv_b[l].reshape(1, 2 * D_FF), layer=l, seq=seq,
                          name=f"ffn_up_{path}{l}")
            xs[path] = _ffn_down(act, w_down_b, x1, mod_l, fg, layer=l, tm=tm, row_of_tile=row,
                                 final=(l == DEPTH - 1), name=f"ffn_down_{path}{l}")
            if not lat:
                new_s.append(jnp.stack([ends[:, SUB - 1], ends[:, SUB]], axis=1))
    new_k, new_v = (b.reshape(N_CTX, DEPTH, S_CTX, N_KV, HEAD_DIM) for b in kv_bufs)
    return xs, new_k, new_v, new_s


def kernel(x_prompt, x_sample, cache_k, cache_v, state_lru, c, c_ctx, norm1_g, norm2_g, w_mod, b_mod, w_in,
           b_gate, conv_w, conv_b, w_rg, b_rg, w_ig, b_ig, lru_lambda, q_norm_g, k_norm_g, w_pool, pool_scale,
           w_br_a, w_br_b, w_br_c, w_o, w_up, ffn_conv_w, ffn_conv_b, w_down, final_norm_g):
    xs, new_k, new_v, new_s = _trunk(
        x_prompt, x_sample, cache_k, cache_v, state_lru, c, c_ctx, norm1_g, norm2_g, w_mod, b_mod, w_in,
        b_gate, conv_w, conv_b, w_rg, b_rg, w_ig, b_ig, lru_lambda, q_norm_g, k_norm_g, w_pool, pool_scale,
        w_br_a, w_br_b, w_br_c, w_o, w_up, ffn_conv_w, ffn_conv_b, w_down, final_norm_g)
    y_prompt = _deinterleave(xs["ctx"], N_CTX, S_CTX)
    y_sample = _deinterleave(xs["lat"], N_LAT, S_LAT)
    return (y_prompt, y_sample, new_k, new_v, jnp.stack(new_s, axis=1))
```

```python
import functools

import numpy as np
import jax
import jax.numpy as jnp
from jax import lax
from jax.experimental import pallas as pl
from jax.experimental.pallas import tpu as pltpu

F32 = jnp.float32
BF16 = jnp.bfloat16

D = 1024
DEPTH = 2
N_CTX, S_CTX = 16, 256
N_LAT, S_LAT = 8, 1024
PAST = 256
GRID_W = 64
LRU_W = 1024
LRU_BW = 64
LRU_C = 8.0
N_HEADS, N_KV, HEAD_DIM = 16, 4, 64
KV_W = N_KV * HEAD_DIM
ROPE_BASE = 10000.0
D_FF = 3 * D
EPS = 1e-6
SUB = 8
LANE = 128
ROPE_SHIFT = HEAD_DIM // 4
C_XR, C_YR, C_Q, C_K, C_V, C_UP, C_GL, C_END = 0, 1024, 2048, 3072, 3328, 3584, 4608, 7680

LOG2E = float(np.log2(np.e))
Q_PRESCALE = LOG2E * HEAD_DIM ** -0.5

VMEM_LIMIT = 56 * 1024 * 1024


def _cparams(n_axes):
    return pltpu.CompilerParams(dimension_semantics=("arbitrary",) * n_axes,
                                vmem_limit_bytes=VMEM_LIMIT)


def _resident(shape, layer):
    return pl.BlockSpec((None,) + shape, lambda *_: (layer,) + (0,) * len(shape), pipeline_mode=pl.Buffered(1))


def _dot(a, b):
    return jnp.dot(a, b, preferred_element_type=F32)


def _dot_nt(a, b):
    return lax.dot_general(a, b, (((1,), (1,)), ((), ())), preferred_element_type=F32)


def _rms(x):
    return x * lax.rsqrt(jnp.mean(x * x, axis=-1, keepdims=True) + EPS)


def _shift_prev(x, rm, n=1):
    r, w = x.shape
    sub = lax.broadcasted_iota(jnp.int32, (SUB, w), 0)
    pieces = []
    for m in range(r // rm):
        base = m * rm
        for t in range(n):
            lo = base + rm - SUB * (n - t)
            pieces.append(jnp.where(sub == 0, 0.0, pltpu.roll(x[lo:lo + SUB], 1, 0)))
        pieces.append(x[base:base + rm - SUB * n])
    return jnp.concatenate(pieces, axis=0)


def _shift_next(x, rm, n=1):
    r, w = x.shape
    sub = lax.broadcasted_iota(jnp.int32, (SUB, w), 0)
    pieces = []
    for m in range(r // rm):
        base = m * rm
        pieces.append(x[base + SUB * n:base + rm])
        for t in range(n):
            lo = base + SUB * t
            pieces.append(jnp.where(sub == SUB - 1, 0.0, pltpu.roll(x[lo:lo + SUB], SUB - 1, 0)))
    return jnp.concatenate(pieces, axis=0)


def _mod_kernel(cond_ref, w_ref, b_ref, out_ref):
    c = cond_ref[...]
    s = (c * jax.nn.sigmoid(c)).astype(BF16)
    out_ref[...] = _dot(s, w_ref[...].astype(BF16)) + b_ref[...]


def _modulation(cond, w_mod, b_mod):
    tn = 1536
    return pl.pallas_call(
        _mod_kernel,
        grid=(DEPTH, 6 * D // tn),
        in_specs=[pl.BlockSpec((16, D), lambda l, j: (0, 0)),
                  pl.BlockSpec((None, D, tn), lambda l, j: (l, 0, j)),
                  pl.BlockSpec((None, 1, tn), lambda l, j: (l, 0, j))],
        out_specs=pl.BlockSpec((None, 16, tn), lambda l, j: (l, 0, j)),
        out_shape=jax.ShapeDtypeStruct((DEPTH, 16, 6 * D), F32),
        compiler_params=_cparams(2),
        name="modulation",
    )(cond, w_mod, b_mod.reshape(DEPTH, 1, 6 * D))


def _inproj_kernel(*refs, rope, layer):
    if rope:
        (x_ref, mod_ref, n1g_ref, w_ref, bgate_ref, qg_ref, kg_ref, gmat_ref, cos_ref, sin_ref,
         xr_ref, gy_ref, q_ref, k_ref, v_ref, up_ref, gate_ref) = refs
    else:
        if layer == 0:
            (x_ref, mod_ref, n1g_ref, w_ref, bgate_ref, qg_ref, kg_ref, gmat_ref,
             xr_ref, gy_ref, q_ref, kc_ref, vc_ref, up_ref, gate_ref, stage_ref) = refs
            for dst in (kc_ref, vc_ref):
                for other in range(1, DEPTH):
                    dst[:, other] = jnp.zeros((dst.shape[0],) + dst.shape[2:], F32)
            kc_ref, vc_ref = kc_ref.at[:, 0], vc_ref.at[:, 0]
        else:
            (x_ref, mod_ref, n1g_ref, w_ref, bgate_ref, qg_ref, kg_ref, gmat_ref, _, _,
             xr_ref, gy_ref, q_ref, kc_ref, vc_ref, up_ref, gate_ref, stage_ref) = refs
    mod = mod_ref[...]
    sh1, sc1 = mod[:, 0:D], mod[:, D:2 * D]
    h = ((_rms(x_ref[...]) * n1g_ref[...]) * (1.0 + sc1) + sh1).astype(BF16)

    def mm(c0, c1):
        return _dot(h, w_ref[:, c0:c1])

    for g in range(3):
        hz = 0.5 * (mm(C_GL + g * D, C_GL + (g + 1) * D) + bgate_ref[:, g * D:(g + 1) * D])
        gate_ref[:, g * D:(g + 1) * D] = (0.5 * jnp.tanh(hz) + 0.5).astype(BF16)
    gy_ref[...] = jax.nn.gelu(mm(C_YR, C_Q)).astype(BF16)

    lane = lax.broadcasted_iota(jnp.int32, (1, LANE), 1)
    first = (lane & ROPE_SHIFT) == 0

    def head_norm(xb, g):
        ms = _dot((xb * xb).astype(BF16), gmat_ref[...])
        y = (xb * lax.rsqrt(ms + EPS)) * g
        if rope:
            partner = jnp.where(first, pltpu.roll(y, LANE - ROPE_SHIFT, 1), pltpu.roll(y, ROPE_SHIFT, 1))
            y = y * cos_ref[...] + partner * sin_ref[...]
        return y

    qa = mm(C_Q, C_K)
    for c in range(N_HEADS * HEAD_DIM // LANE):
        qn = head_norm(qa[:, c * LANE:(c + 1) * LANE], qg_ref[...])
        q_ref[:, c * LANE:(c + 1) * LANE] = (qn * Q_PRESCALE).astype(BF16)
    kv_slabs = KV_W // LANE
    ka = mm(C_K, C_V)
    for c in range(kv_slabs):
        kn = head_norm(ka[:, c * LANE:(c + 1) * LANE], kg_ref[...])
        if rope:
            k_ref[:, c * LANE:(c + 1) * LANE] = kn
        else:
            stage_ref[c] = kn
    up_ref[...] = mm(C_UP, C_GL).astype(BF16)
    xr_ref[...] = mm(C_XR, C_YR).astype(BF16)
    va = mm(C_V, C_UP)
    if rope:
        v_ref[...] = va
    else:
        steps = S_CTX // SUB
        for c in range(kv_slabs):
            stage_ref[kv_slabs + c] = va[:, c * LANE:(c + 1) * LANE]
        for slab, dst in enumerate([kc_ref] * kv_slabs + [vc_ref] * kv_slabs):
            cols = slice((slab % kv_slabs) * LANE, (slab % kv_slabs + 1) * LANE)
            for s in range(x_ref.shape[0] // S_CTX):
                for j in range(SUB):
                    dst[s, j * steps:(j + 1) * steps, cols] = (
                        stage_ref[slab, pl.ds(s * S_CTX + j, steps, stride=SUB), :])


def _inproj(x, mod_l, n1g, w_in, b_gate, qg, kg, gmat, rope_tabs, kv_bufs, *, layer, tm, row_of_tile, name):
    t = x.shape[0]
    rope = rope_tabs is not None
    full = lambda shape: pl.BlockSpec(shape, lambda i: (0,) * len(shape))
    in_specs = [pl.BlockSpec((tm, D), lambda i: (i, 0)),
                pl.BlockSpec((None, 1, 6 * D), lambda i: (row_of_tile(i), 0, 0)),
                full((1, D)), _resident((D, C_END), layer), full((1, 3 * D)), full((1, LANE)), full((1, LANE)),
                full((LANE, LANE))]
    args = [x, mod_l, n1g, w_in, b_gate, qg, kg, gmat]
    if rope:
        per_seq = S_LAT // tm
        in_specs += [pl.BlockSpec((tm, LANE), lambda i: (i % per_seq, 0))] * 2
        args += list(rope_tabs)
    row = lambda w: pl.BlockSpec((tm, w), lambda i: (i, 0))
    act = lambda w: jax.ShapeDtypeStruct((t, w), BF16)
    if rope:
        kv_specs = [row(KV_W), row(KV_W)]
        kv_shapes = [jax.ShapeDtypeStruct((t, KV_W), F32)] * 2
        aliases, scratch = {}, []
    else:
        kv_shapes = [jax.ShapeDtypeStruct((N_CTX, DEPTH, S_CTX, KV_W), F32)] * 2
        if layer == 0:
            kv_specs = [pl.BlockSpec((tm // S_CTX, DEPTH, S_CTX, KV_W), lambda i: (i, 0, 0, 0))] * 2
            aliases = {}
        else:
            in_specs += [pl.BlockSpec(memory_space=pl.ANY)] * 2
            args += list(kv_bufs)
            kv_specs = [pl.BlockSpec((tm // S_CTX, None, S_CTX, KV_W), lambda i: (i, layer, 0, 0))] * 2
            aliases = {len(args) - 2: 3, len(args) - 1: 4}
        scratch = [pltpu.VMEM((2 * KV_W // LANE, tm, LANE), F32)]
    out_specs = [row(D), row(D), row(D)] + kv_specs + [row(D), row(3 * D)]
    out_shape = [act(D), act(D), act(D)] + kv_shapes + [act(D), act(3 * D)]
    return pl.pallas_call(
        functools.partial(_inproj_kernel, rope=rope, layer=layer),
        grid=(t // tm,), in_specs=in_specs, out_specs=out_specs, out_shape=out_shape,
        input_output_aliases=aliases, scratch_shapes=scratch,
        compiler_params=_cparams(1), name=name,
    )(*args)


GB = 256
SCAN_CHUNK = 8


def _lru_kernel(xr_ref, gy_ref, h0_ref, cw_ref, cb_ref, wg_ref, bg_ref, lam_ref,
                out_ref, ends_ref, a_s, u_s, h_s, p_s, xs_ref, xc_s):
    r = xr_ref.shape[0]
    steps = r // SUB
    sub = lax.broadcasted_iota(jnp.int32, (SUB, GB), 0)
    for cb in range(LRU_W // GB):
        cols = slice(cb * GB, (cb + 1) * GB)
        x = xr_ref[:, cols].astype(F32)
        xs_ref[SUB:r + SUB, :] = x
        xs_ref[0:SUB, :] = jnp.where(sub == 0, 0.0, pltpu.roll(x[r - SUB:r], 1, 0))
        for t in range(2):
            xs_ref[r + (1 + t) * SUB:r + (2 + t) * SUB, :] = jnp.where(
                sub == SUB - 1, 0.0, pltpu.roll(x[t * SUB:(t + 1) * SUB], SUB - 1, 0))
        xc = cb_ref[:, cols] + xs_ref[0:r, :] * cw_ref[0:1, cols]
        for t in range(1, 4):
            xc = xc + xs_ref[t * SUB:r + t * SUB, :] * cw_ref[t:t + 1, cols]
        xc_s[...] = xc
        lhs = xc_s[...].astype(BF16)
        xh = 0.5 * xc_s[...]
        for d in range(2):
            g = _dot(lhs, wg_ref[d, cb])
            tr = jnp.tanh(g[:, :GB] + bg_ref[d, 0:1, cols])
            ti = jnp.tanh(g[:, GB:] + bg_ref[d, 1:2, cols])
            lam = lam_ref[d:d + 1, cols]
            log_sig = jnp.minimum(lam, 0.0) - jnp.log(1.0 + jnp.exp(-jnp.abs(lam)))
            ch = (0.5 * LRU_C) * log_sig
            w = tr * (ch * LOG2E) + ch * LOG2E
            a = jnp.exp2(w)
            m2 = jnp.tanh(w * (-1.0 / LOG2E)) * (1.0 + a * a)
            mult = jnp.where(m2 > 0.0, m2 * lax.rsqrt(m2), 0.0)
            u = mult * (ti * xh + xh)
            a_s[d] = a
            u_s[d] = u
            e0 = 0 if d == 0 else r - SUB
            edge = sub == (0 if d == 0 else SUB - 1)
            ae = a_s[d, e0:e0 + SUB, :]
            u_s[d, e0:e0 + SUB, :] = u_s[d, e0:e0 + SUB, :] + jnp.where(edge, ae * h0_ref[d:d + 1, cols], 0.0)
            a_s[d, e0:e0 + SUB, :] = jnp.where(edge, 0.0, ae)

        def chunk(i, carry):
            hf, pf, hb, pb = carry
            base_f = pl.multiple_of(i * (SCAN_CHUNK * SUB), SCAN_CHUNK * SUB)
            base_b = pl.multiple_of((steps - SCAN_CHUNK) * SUB - i * (SCAN_CHUNK * SUB), SCAN_CHUNK * SUB)
            for s in range(SCAN_CHUNK):
                rf = pl.ds(base_f + s * SUB, SUB)
                rb = pl.ds(base_b + (SCAN_CHUNK - 1 - s) * SUB, SUB)
                af = a_s[0, rf, :]
                hf = af * hf + u_s[0, rf, :]
                pf = af * pf
                h_s[0, rf, :] = hf
                p_s[0, rf, :] = pf
                ab = a_s[1, rb, :]
                hb = ab * hb + u_s[1, rb, :]
                pb = ab * pb
                h_s[1, rb, :] = hb
                p_s[1, rb, :] = pb
            return hf, pf, hb, pb

        zero = jnp.zeros((SUB, GB), F32)
        one = jnp.ones((SUB, GB), F32)
        hf, pf, hb, pb = lax.fori_loop(0, steps // SCAN_CHUNK, chunk, (zero, one, zero, one))

        ef, eb = hf, hb
        for sh in (1, 2, 4):
            keep_f = sub >= sh
            ef = ef + pf * jnp.where(keep_f, pltpu.roll(ef, sh, 0), 0.0)
            pf = pf * jnp.where(keep_f, pltpu.roll(pf, sh, 0), 1.0)
            keep_b = sub < SUB - sh
            eb = eb + pb * jnp.where(keep_b, pltpu.roll(eb, SUB - sh, 0), 0.0)
            pb = pb * jnp.where(keep_b, pltpu.roll(pb, SUB - sh, 0), 1.0)
        ends_ref[0:SUB, cols] = ef
        ends_ref[SUB:2 * SUB, cols] = eb
        cf = jnp.where(sub >= 1, pltpu.roll(ef, 1, 0), 0.0)
        cbk = jnp.where(sub < SUB - 1, pltpu.roll(eb, SUB - 1, 0), 0.0)
        cf2 = jnp.concatenate([cf, cf], axis=0)
        cb2 = jnp.concatenate([cbk, cbk], axis=0)

        def fix(m, _):
            rows = pl.ds(pl.multiple_of(m * 2 * SUB, 2 * SUB), 2 * SUB)
            hft = h_s[0, rows, :] + p_s[0, rows, :] * cf2
            hbt = h_s[1, rows, :] + p_s[1, rows, :] * cb2
            out_ref[rows, cols] = ((hft + hbt) * gy_ref[rows, cols].astype(F32)).astype(BF16)
            return 0

        lax.fori_loop(0, r // (2 * SUB), fix, 0, unroll=4)


def _lru(xr, gy, h0, conv_w, conv_b, wg, bg, lam, *, seq, name):
    t = xr.shape[0]
    nseq = t // seq
    full = lambda shape: pl.BlockSpec(shape, lambda i: (0,) * len(shape))
    return pl.pallas_call(
        _lru_kernel,
        grid=(nseq,),
        in_specs=[pl.BlockSpec((seq, D), lambda i: (i, 0)), pl.BlockSpec((seq, D), lambda i: (i, 0)),
                  pl.BlockSpec((None, 2, LRU_W), lambda i: (i, 0, 0)),
                  full((4, LRU_W)), full((1, LRU_W)), full((2, LRU_W // GB, GB, 2 * GB)),
                  full((2, 2, LRU_W)), full((2, LRU_W))],
        out_specs=[pl.BlockSpec((seq, D), lambda i: (i, 0)),
                   pl.BlockSpec((None, 2 * SUB, LRU_W), lambda i: (i, 0, 0))],
        out_shape=[jax.ShapeDtypeStruct((t, D), BF16), jax.ShapeDtypeStruct((nseq, 2 * SUB, LRU_W), F32)],
        scratch_shapes=[pltpu.VMEM((2, seq, GB), F32)] * 4 +
                       [pltpu.VMEM((seq + 3 * SUB, GB), F32), pltpu.VMEM((seq, GB), F32)],
        compiler_params=_cparams(1), name=name,
    )(xr, gy, h0, conv_w, conv_b, wg, bg, lam)


POOL_G = 256


def _pool_kernel(up_ref, wp_ref, scale_ref, out_ref):
    r = up_ref.shape[0]
    steps = r // SUB
    row = lax.broadcasted_iota(jnp.int32, (r, 1), 0)
    t = (row & (SUB - 1)) * steps + jnp.right_shift(row, 3)
    for g in range(4):
        cols = slice(g * POOL_G, (g + 1) * POOL_G)
        x = up_ref[:, cols].astype(F32)
        half = 1 << g
        back, fwd = x, x
        for lvl in range(g):
            n = 1 << lvl
            back = back + _shift_prev(back, r, n)
            fwd = fwd + _shift_next(fwd, r, n)
        win = _shift_prev(back, r) + fwd
        cnt = jnp.minimum(t + half, r) - jnp.maximum(t - half, 0)
        d = (win / cnt.astype(F32) - x).astype(BF16)
        out_ref[:, cols] = (_dot(d, wp_ref[g]) * scale_ref[:, cols]).astype(BF16)


def _pool(up, w_pool, scale, *, layer, seq, name):
    t = up.shape[0]
    return pl.pallas_call(
        _pool_kernel,
        grid=(t // seq,),
        in_specs=[pl.BlockSpec((seq, D), lambda i: (i, 0)),
                  pl.BlockSpec((None, 4, POOL_G, POOL_G), lambda i: (layer, 0, 0, 0)),
                  pl.BlockSpec((1, D), lambda i: (0, 0))],
        out_specs=pl.BlockSpec((seq, D), lambda i: (i, 0)),
        out_shape=jax.ShapeDtypeStruct((t, D), BF16),
        compiler_params=_cparams(1), name=name,
    )(up, w_pool, scale)


ONES_ROWS = 16


def _attn_kernel(*refs, cached):
    transposed = cached
    if cached:
        q_ref, k_ref, v_ref, ck_ref, cv_ref, o_ref = refs
        kall = jnp.concatenate([ck_ref[...], k_ref[...]], axis=0)
        vall = jnp.concatenate([cv_ref[...], v_ref[...]], axis=0)
    else:
        q_ref, k_ref, v_ref, o_ref = refs
        kall = k_ref[...]
        vall = v_ref[...]
    half = HEAD_DIM
    lane = lax.broadcasted_iota(jnp.int32, (1, LANE), 1)

    def both_halves(x, own_low):
        own = jnp.where((lane < half) if own_low else (lane >= half), x, 0.0)
        oth = pltpu.roll(own, half, 1)
        lo, hi = (own, oth) if own_low else (oth, own)
        return lo.astype(BF16), hi.astype(BF16)

    if transposed:
        heads = []
        for kvh in range(N_KV):
            blk = slice((kvh // 2) * LANE, (kvh // 2 + 1) * LANE)
            ks = both_halves(kall[:, blk], kvh % 2 == 0)
            r0 = (kvh % 2) * HEAD_DIM
            v_t = vall[:, blk].T[r0:r0 + HEAD_DIM]
            v_t = jnp.concatenate([v_t, jnp.ones((ONES_ROWS, v_t.shape[1]), F32)], axis=0).astype(BF16)
            for pair in range(2):
                heads += [(kk, v_t, kvh * 4 * HEAD_DIM + pair * LANE) for kk in ks]

        def scores(h):
            kk, _, c0 = heads[h]
            return _dot_nt(kk, q_ref[:, c0:c0 + LANE])

        s_next = scores(0)
        outs = []
        for h, (_, v_t, c0) in enumerate(heads):
            s = s_next
            if h + 1 < len(heads):
                s_next = scores(h + 1)
            e = jnp.exp2(s - jnp.max(s, axis=0, keepdims=True)).astype(BF16)
            pv = _dot(v_t, e)
            outs.append(pv[:HEAD_DIM] / pv[HEAD_DIM:HEAD_DIM + 1])
            if h % 2 == 1:
                o_ref[:, c0:c0 + LANE] = jnp.concatenate(outs, axis=0).T.astype(BF16)
                outs = []
        return

    for kvh in range(N_KV):
        blk = slice((kvh // 2) * LANE, (kvh // 2 + 1) * LANE)
        ks = both_halves(kall[:, blk], kvh % 2 == 0)
        vs = both_halves(vall[:, blk], kvh % 2 == 0)
        c0 = kvh * 4 * HEAD_DIM
        sq = q_ref.shape[0]
        qq = jnp.concatenate([q_ref[:, c0:c0 + LANE], q_ref[:, c0 + LANE:c0 + 2 * LANE]], axis=0)
        acc = jnp.zeros((2 * sq, LANE), F32)
        for kk, vv in zip(ks, vs):
            s = _dot_nt(qq, kk)
            e = jnp.exp2(s - jnp.max(s, axis=-1, keepdims=True))
            l = jnp.sum(e, axis=-1, keepdims=True)
            acc = acc + _dot(e.astype(BF16), vv) / l
        o_ref[:, c0:c0 + LANE] = acc[:sq].astype(BF16)
        o_ref[:, c0 + LANE:c0 + 2 * LANE] = acc[sq:].astype(BF16)


def _attention(q, k, v, cache_k, cache_v, layer, *, seq, name):
    t = q.shape[0]
    cached = cache_k is not None
    if cached:
        kv_spec = pl.BlockSpec((seq, KV_W), lambda i: (i, 0))
    else:
        kv_spec = pl.BlockSpec((None, None, seq, KV_W), lambda i: (i, layer, 0, 0))
    in_specs = [pl.BlockSpec((seq, D), lambda i: (i, 0)), kv_spec, kv_spec]
    args = [q, k, v]
    if cached:
        in_specs += [pl.BlockSpec((None, None, PAST, KV_W), lambda i: (i, layer, 0, 0))] * 2
        args += [cache_k, cache_v]
    return pl.pallas_call(
        functools.partial(_attn_kernel, cached=cached),
        grid=(t // seq,), in_specs=in_specs,
        out_specs=pl.BlockSpec((seq, D), lambda i: (i, 0)),
        out_shape=jax.ShapeDtypeStruct((t, D), BF16),
        compiler_params=_cparams(1), name=name,
    )(*args)


def _merge_kernel(a_ref, o_ref, c_ref, gate_ref, x_ref, mod_ref, n2g_ref, wa_ref, wb_ref, wc_ref, wo_ref,
                  x1_ref, h2_ref):
    mix = gate_ref[:, 0:D].astype(F32) * _dot(a_ref[...], wa_ref[...])
    mix = mix + gate_ref[:, D:2 * D].astype(F32) * _dot(o_ref[...], wb_ref[...])
    mix = mix + gate_ref[:, 2 * D:3 * D].astype(F32) * _dot(c_ref[...], wc_ref[...])
    out = _dot(mix.astype(BF16), wo_ref[...])
    mod = mod_ref[...]
    g1, sh2, sc2 = mod[:, 2 * D:3 * D], mod[:, 3 * D:4 * D], mod[:, 4 * D:5 * D]
    x1 = x_ref[...] + g1 * out
    x1_ref[...] = x1
    h2_ref[...] = ((_rms(x1) * n2g_ref[...]) * (1.0 + sc2) + sh2).astype(BF16)


def _merge(a, o, c, gates, x, mod_l, n2g, wa, wb, wc, wo, *, layer, tm, row_of_tile, name):
    t = x.shape[0]
    row = lambda w: pl.BlockSpec((tm, w), lambda i: (i, 0))
    wspec = _resident((D, D), layer)
    return pl.pallas_call(
        _merge_kernel,
        grid=(t // tm,),
        in_specs=[row(D), row(D), row(D), row(3 * D), row(D),
                  pl.BlockSpec((None, 1, 6 * D), lambda i: (row_of_tile(i), 0, 0)),
                  pl.BlockSpec((1, D), lambda i: (0, 0)), wspec, wspec, wspec, wspec],
        out_specs=[row(D), row(D)],
        out_shape=[jax.ShapeDtypeStruct((t, D), F32), jax.ShapeDtypeStruct((t, D), BF16)],
        compiler_params=_cparams(1), name=name,
    )(a, o, c, gates, x, mod_l, n2g, wa, wb, wc, wo)


FF_TN = 1024
FF_ROWS = 1024


def _ffn_up_kernel(h_ref, wg_ref, wv_ref, cwg_ref, cwv_ref, cbg_ref, cbv_ref, out_ref, *, seq):
    h = h_ref[...]

    def conv(z, cw_ref, cb_ref):
        y = cb_ref[...] + _shift_prev(z, seq) * cw_ref[0:1, :]
        y = y + z * cw_ref[1:2, :]
        return y + _shift_next(z, seq) * cw_ref[2:3, :]

    g = conv(_dot(h, wg_ref[...]), cwg_ref, cbg_ref)
    v = conv(_dot(h, wv_ref[...]), cwv_ref, cbv_ref)
    hg = 0.5 * g
    out_ref[...] = ((hg * jnp.tanh(hg) + hg) * v).astype(BF16)


def _ffn_up(h2, w_up, cw, cb, *, layer, seq, name):
    t = h2.shape[0]
    nj = D_FF // FF_TN
    return pl.pallas_call(
        functools.partial(_ffn_up_kernel, seq=seq),
        grid=(t // FF_ROWS, nj),
        in_specs=[pl.BlockSpec((FF_ROWS, D), lambda i, j: (i, 0)),
                  pl.BlockSpec((None, D, FF_TN), lambda i, j: (layer, 0, j)),
                  pl.BlockSpec((None, D, FF_TN), lambda i, j: (layer, 0, nj + j)),
                  pl.BlockSpec((3, FF_TN), lambda i, j: (0, j)),
                  pl.BlockSpec((3, FF_TN), lambda i, j: (0, nj + j)),
                  pl.BlockSpec((1, FF_TN), lambda i, j: (0, j)),
                  pl.BlockSpec((1, FF_TN), lambda i, j: (0, nj + j))],
        out_specs=pl.BlockSpec((FF_ROWS, FF_TN), lambda i, j: (i, j)),
        out_shape=jax.ShapeDtypeStruct((t, D_FF), BF16),
        compiler_params=_cparams(2), name=name,
    )(h2, w_up, w_up, cw, cw, cb, cb)


def _ffn_down_kernel(act_ref, w_ref, x1_ref, mod_ref, fg_ref, out_ref, *, final):
    g2 = mod_ref[:, 5 * D:6 * D]
    x2 = x1_ref[...] + g2 * _dot(act_ref[...], w_ref[...])
    out_ref[...] = _rms(x2) * fg_ref[...] if final else x2


def _ffn_down(act, w_down, x1, mod_l, fg, *, layer, tm, row_of_tile, final, name):
    t = x1.shape[0]
    return pl.pallas_call(
        functools.partial(_ffn_down_kernel, final=final),
        grid=(t // tm,),
        in_specs=[pl.BlockSpec((tm, D_FF), lambda i: (i, 0)),
                  _resident((D_FF, D), layer),
                  pl.BlockSpec((tm, D), lambda i: (i, 0)),
                  pl.BlockSpec((None, 1, 6 * D), lambda i: (row_of_tile(i), 0, 0)),
                  pl.BlockSpec((1, D), lambda i: (0, 0))],
        out_specs=pl.BlockSpec((tm, D), lambda i: (i, 0)),
        out_shape=jax.ShapeDtypeStruct((t, D), F32),
        compiler_params=_cparams(1), name=name,
    )(act, w_down, x1, mod_l, fg)


def _interleave(x):
    b, s, w = x.shape
    return x.reshape(b, SUB, s // SUB, w).transpose(0, 2, 1, 3).reshape(b * s, w)


def _deinterleave(y, b, s):
    return y.reshape(b, s // SUB, SUB, -1).transpose(0, 2, 1, 3).reshape(b, s, -1)


def _rope_tables():
    p = np.arange(S_LAT)
    t = (p % SUB) * (S_LAT // SUB) + p // SUB
    pos = np.stack([t // GRID_W, t % GRID_W], axis=1).astype(np.float32)
    quarter = HEAD_DIM // 4
    inv = (ROPE_BASE ** (-np.arange(quarter, dtype=np.float32) / quarter)).astype(np.float32)
    d = np.arange(LANE) % HEAD_DIM
    which = d // (HEAD_DIM // 2)
    e = d % (HEAD_DIM // 2)
    ang = pos[:, which] * inv[e % quarter][None, :]
    sign = np.where(e < quarter, -1.0, 1.0).astype(np.float32)
    return jnp.asarray(np.cos(ang).astype(np.float32)), jnp.asarray((np.sin(ang) * sign).astype(np.float32))


def _block_diag_gates(w_r, w_i):
    eye = jnp.eye(4, dtype=F32)

    def bd(w):
        w = w.reshape(2, 4, 4, LRU_BW, LRU_BW)
        w = w[:, :, :, :, None, :] * eye[None, None, :, None, :, None]
        return w.reshape(2, 4, GB, GB)

    return jnp.concatenate([bd(w_r), bd(w_i)], axis=-1).astype(BF16)


def _trunk(x_prompt, x_sample, cache_k, cache_v, state_lru, c, c_ctx, norm1_g, norm2_g, w_mod, b_mod, w_in,
           b_gate, conv_w, conv_b, w_rg, b_rg, w_ig, b_ig, lru_lambda, q_norm_g, k_norm_g, w_pool, pool_scale,
           w_br_a, w_br_b, w_br_c, w_o, w_up, ffn_conv_w, ffn_conv_b, w_down, final_norm_g,
           paths=("ctx", "lat")):
    cond = jnp.zeros((16, D), F32).at[0:N_LAT].set(c).at[N_LAT].set(c_ctx)
    mod = _modulation(cond, w_mod, b_mod)
    rope_tabs = _rope_tables()
    gmat = jnp.asarray(np.kron(np.eye(2), np.full((HEAD_DIM, HEAD_DIM), 1.0 / HEAD_DIM)), BF16)
    ck = cache_k.reshape(N_LAT, DEPTH, PAST, KV_W)
    cv = cache_v.reshape(N_LAT, DEPTH, PAST, KV_W)
    fg = final_norm_g.reshape(1, D)

    xs = {"ctx": _interleave(x_prompt), "lat": _interleave(x_sample)}
    cfg = {"ctx": dict(seq=S_CTX, tm=512, row=lambda i: N_LAT),
           "lat": dict(seq=S_LAT, tm=512, row=lambda i: i // (S_LAT // 512))}
    new_s = []
    kv_bufs = None
    w_in_b, wa, wb, wc, wo, w_up_b, w_down_b, wp = (
        w.astype(BF16) for w in (w_in, w_br_a, w_br_b, w_br_c, w_o, w_up, w_down, w_pool))
    for l in range(DEPTH):
        mod_l = mod[l].reshape(16, 1, 6 * D)
        wg = _block_diag_gates(0.5 * w_rg[l], 0.5 * w_ig[l])
        bg = 0.5 * jnp.stack([b_rg[l], b_ig[l]], axis=1)
        qg = jnp.tile(q_norm_g[l], 2).reshape(1, LANE)
        kg = jnp.tile(k_norm_g[l], 2).reshape(1, LANE)
        for path in paths:
            seq, tm, row = cfg[path]["seq"], cfg[path]["tm"], cfg[path]["row"]
            lat = path == "lat"
            x = xs[path]
            xr, gy, q, k, v, up, gates = _inproj(
                x, mod_l, norm1_g[l].reshape(1, D), w_in_b, b_gate[l].reshape(1, 3 * D), qg, kg, gmat,
                rope_tabs if lat else None, None if lat else kv_bufs, layer=l, tm=tm, row_of_tile=row,
                name=f"inproj_{path}{l}")
            if not lat:
                kv_bufs = (k, v)
            h0 = state_lru[:, l] if lat else jnp.zeros((N_CTX, 2, LRU_W), F32)
            a_pre, ends = _lru(xr, gy, h0, conv_w[l], conv_b[l].reshape(1, LRU_W), wg, bg, lru_lambda[l],
                               seq=seq, name=f"lru_{path}{l}")
            c_pre = _pool(up, wp, pool_scale[l].reshape(1, D), layer=l, seq=seq, name=f"pool_{path}{l}")
            o = _attention(q, k, v, ck if lat else None, cv if lat else None, l, seq=seq,
                           name=f"attn_{path}{l}")
            x1, h2 = _merge(a_pre, o, c_pre, gates, x, mod_l, norm2_g[l].reshape(1, D), wa, wb, wc, wo,
                            layer=l, tm=tm, row_of_tile=row, name=f"merge_{path}{l}")
            act = _ffn_up(h2, w_up_b, ffn_conv_w[l], ffn_conv_b[l].reshape(1, 2 * D_FF), layer=l, seq=seq,
                          name=f"ffn_up_{path}{l}")
            xs[path] = _ffn_down(act, w_down_b, x1, mod_l, fg, layer=l, tm=tm, row_of_tile=row,
                                 final=(l == DEPTH - 1), name=f"ffn_down_{path}{l}")
            if not lat:
                new_s.append(jnp.stack([ends[:, SUB - 1], ends[:, SUB]], axis=1))
    new_k, new_v = (b.reshape(N_CTX, DEPTH, S_CTX, N_KV, HEAD_DIM) for b in kv_bufs)
    return xs, new_k, new_v, new_s


def kernel(x_prompt, x_sample, cache_k, cache_v, state_lru, c, c_ctx, norm1_g, norm2_g, w_mod, b_mod, w_in,
           b_gate, conv_w, conv_b, w_rg, b_rg, w_ig, b_ig, lru_lambda, q_norm_g, k_norm_g, w_pool, pool_scale,
           w_br_a, w_br_b, w_br_c, w_o, w_up, ffn_conv_w, ffn_conv_b, w_down, final_norm_g):
    xs, new_k, new_v, new_s = _trunk(
        x_prompt, x_sample, cache_k, cache_v, state_lru, c, c_ctx, norm1_g, norm2_g, w_mod, b_mod, w_in,
        b_gate, conv_w, conv_b, w_rg, b_rg, w_ig, b_ig, lru_lambda, q_norm_g, k_norm_g, w_pool, pool_scale,
        w_br_a, w_br_b, w_br_c, w_o, w_up, ffn_conv_w, ffn_conv_b, w_down, final_norm_g)
    y_prompt = _deinterleave(xs["ctx"], N_CTX, S_CTX)
    y_sample = _deinterleave(xs["lat"], N_LAT, S_LAT)
    return (y_prompt, y_sample, new_k, new_v, jnp.stack(new_s, axis=1))
```

```python
import functools

import numpy as np
import jax
import jax.numpy as jnp
from jax import lax
from jax.experimental import pallas as pl
from jax.experimental.pallas import tpu as pltpu

F32 = jnp.float32
BF16 = jnp.bfloat16

D = 1024
DEPTH = 2
N_CTX, S_CTX = 16, 256
N_LAT, S_LAT = 8, 1024
PAST = 256
GRID_W = 64
LRU_W = 1024
LRU_BW = 64
LRU_C = 8.0
N_HEADS, N_KV, HEAD_DIM = 16, 4, 64
KV_W = N_KV * HEAD_DIM
ROPE_BASE = 10000.0
D_FF = 3 * D
EPS = 1e-6
SUB = 8
LANE = 128
ROPE_SHIFT = HEAD_DIM // 4
C_XR, C_YR, C_Q, C_K, C_V, C_UP, C_GL, C_END = 0, 1024, 2048, 3072, 3328, 3584, 4608, 7680

LOG2E = float(np.log2(np.e))
Q_PRESCALE = LOG2E * HEAD_DIM ** -0.5

VMEM_LIMIT = 56 * 1024 * 1024


def _cparams(n_axes):
    return pltpu.CompilerParams(dimension_semantics=("arbitrary",) * n_axes,
                                vmem_limit_bytes=VMEM_LIMIT)


def _resident(shape, layer):
    return pl.BlockSpec((None,) + shape, lambda *_: (layer,) + (0,) * len(shape), pipeline_mode=pl.Buffered(1))


def _dot(a, b):
    return jnp.dot(a, b, preferred_element_type=F32)


def _dot_nt(a, b):
    return lax.dot_general(a, b, (((1,), (1,)), ((), ())), preferred_element_type=F32)


def _rms(x):
    return x * lax.rsqrt(jnp.mean(x * x, axis=-1, keepdims=True) + EPS)


def _shift_prev(x, rm, n=1):
    r, w = x.shape
    sub = lax.broadcasted_iota(jnp.int32, (SUB, w), 0)
    pieces = []
    for m in range(r // rm):
        base = m * rm
        for t in range(n):
            lo = base + rm - SUB * (n - t)
            pieces.append(jnp.where(sub == 0, 0.0, pltpu.roll(x[lo:lo + SUB], 1, 0)))
        pieces.append(x[base:base + rm - SUB * n])
    return jnp.concatenate(pieces, axis=0)


def _shift_next(x, rm, n=1):
    r, w = x.shape
    sub = lax.broadcasted_iota(jnp.int32, (SUB, w), 0)
    pieces = []
    for m in range(r // rm):
        base = m * rm
        pieces.append(x[base + SUB * n:base + rm])
        for t in range(n):
            lo = base + SUB * t
            pieces.append(jnp.where(sub == SUB - 1, 0.0, pltpu.roll(x[lo:lo + SUB], SUB - 1, 0)))
    return jnp.concatenate(pieces, axis=0)


def _mod_kernel(cond_ref, w_ref, b_ref, out_ref):
    c = cond_ref[...]
    s = (c * jax.nn.sigmoid(c)).astype(BF16)
    out_ref[...] = _dot(s, w_ref[...].astype(BF16)) + b_ref[...]


def _modulation(cond, w_mod, b_mod):
    tn = 1536
    return pl.pallas_call(
        _mod_kernel,
        grid=(DEPTH, 6 * D // tn),
        in_specs=[pl.BlockSpec((16, D), lambda l, j: (0, 0)),
                  pl.BlockSpec((None, D, tn), lambda l, j: (l, 0, j)),
                  pl.BlockSpec((None, 1, tn), lambda l, j: (l, 0, j))],
        out_specs=pl.BlockSpec((None, 16, tn), lambda l, j: (l, 0, j)),
        out_shape=jax.ShapeDtypeStruct((DEPTH, 16, 6 * D), F32),
        compiler_params=_cparams(2),
        name="modulation",
    )(cond, w_mod, b_mod.reshape(DEPTH, 1, 6 * D))


def _inproj_kernel(*refs, rope, layer):
    if rope:
        (x_ref, mod_ref, n1g_ref, w_ref, bgate_ref, qg_ref, kg_ref, gmat_ref, cos_ref, sin_ref,
         xr_ref, gy_ref, q_ref, k_ref, v_ref, up_ref, gate_ref) = refs
    else:
        if layer == 0:
            (x_ref, mod_ref, n1g_ref, w_ref, bgate_ref, qg_ref, kg_ref, gmat_ref,
             xr_ref, gy_ref, q_ref, kc_ref, vc_ref, up_ref, gate_ref, stage_ref) = refs
            for dst in (kc_ref, vc_ref):
                for other in range(1, DEPTH):
                    dst[:, other] = jnp.zeros((dst.shape[0],) + dst.shape[2:], F32)
            kc_ref, vc_ref = kc_ref.at[:, 0], vc_ref.at[:, 0]
        else:
            (x_ref, mod_ref, n1g_ref, w_ref, bgate_ref, qg_ref, kg_ref, gmat_ref, _, _,
             xr_ref, gy_ref, q_ref, kc_ref, vc_ref, up_ref, gate_ref, stage_ref) = refs
    mod = mod_ref[...]
    sh1, sc1 = mod[:, 0:D], mod[:, D:2 * D]
    h = ((_rms(x_ref[...]) * n1g_ref[...]) * (1.0 + sc1) + sh1).astype(BF16)

    def mm(c0, c1):
        return _dot(h, w_ref[:, c0:c1])

    for g in range(3):
        hz = 0.5 * (mm(C_GL + g * D, C_GL + (g + 1) * D) + bgate_ref[:, g * D:(g + 1) * D])
        gate_ref[:, g * D:(g + 1) * D] = (0.5 * jnp.tanh(hz) + 0.5).astype(BF16)
    gy_ref[...] = jax.nn.gelu(mm(C_YR, C_Q)).astype(BF16)

    lane = lax.broadcasted_iota(jnp.int32, (1, LANE), 1)
    first = (lane & ROPE_SHIFT) == 0

    def head_norm(xb, g):
        ms = _dot((xb * xb).astype(BF16), gmat_ref[...])
        y = (xb * lax.rsqrt(ms + EPS)) * g
        if rope:
            partner = jnp.where(first, pltpu.roll(y, LANE - ROPE_SHIFT, 1), pltpu.roll(y, ROPE_SHIFT, 1))
            y = y * cos_ref[...] + partner * sin_ref[...]
        return y

    qa = mm(C_Q, C_K)
    for c in range(N_HEADS * HEAD_DIM // LANE):
        qn = head_norm(qa[:, c * LANE:(c + 1) * LANE], qg_ref[...])
        q_ref[:, c * LANE:(c + 1) * LANE] = (qn * Q_PRESCALE).astype(BF16)
    kv_slabs = KV_W // LANE
    ka = mm(C_K, C_V)
    for c in range(kv_slabs):
        kn = head_norm(ka[:, c * LANE:(c + 1) * LANE], kg_ref[...])
        if rope:
            k_ref[:, c * LANE:(c + 1) * LANE] = kn
        else:
            stage_ref[c] = kn
    up_ref[...] = mm(C_UP, C_GL).astype(BF16)
    xr_ref[...] = mm(C_XR, C_YR).astype(BF16)
    va = mm(C_V, C_UP)
    if rope:
        v_ref[...] = va
    else:
        steps = S_CTX // SUB
        for c in range(kv_slabs):
            stage_ref[kv_slabs + c] = va[:, c * LANE:(c + 1) * LANE]
        for slab, dst in enumerate([kc_ref] * kv_slabs + [vc_ref] * kv_slabs):
            cols = slice((slab % kv_slabs) * LANE, (slab % kv_slabs + 1) * LANE)
            for s in range(x_ref.shape[0] // S_CTX):
                for j in range(SUB):
                    dst[s, j * steps:(j + 1) * steps, cols] = (
                        stage_ref[slab, pl.ds(s * S_CTX + j, steps, stride=SUB), :])


def _inproj(x, mod_l, n1g, w_in, b_gate, qg, kg, gmat, rope_tabs, kv_bufs, *, layer, tm, row_of_tile, name):
    t = x.shape[0]
    rope = rope_tabs is not None
    full = lambda shape: pl.BlockSpec(shape, lambda i: (0,) * len(shape))
    in_specs = [pl.BlockSpec((tm, D), lambda i: (i, 0)),
                pl.BlockSpec((None, 1, 6 * D), lambda i: (row_of_tile(i), 0, 0)),
                full((1, D)), _resident((D, C_END), layer), full((1, 3 * D)), full((1, LANE)), full((1, LANE)),
                full((LANE, LANE))]
    args = [x, mod_l, n1g, w_in, b_gate, qg, kg, gmat]
    if rope:
        per_seq = S_LAT // tm
        in_specs += [pl.BlockSpec((tm, LANE), lambda i: (i % per_seq, 0))] * 2
        args += list(rope_tabs)
    row = lambda w: pl.BlockSpec((tm, w), lambda i: (i, 0))
    act = lambda w: jax.ShapeDtypeStruct((t, w), BF16)
    if rope:
        kv_specs = [row(KV_W), row(KV_W)]
        kv_shapes = [jax.ShapeDtypeStruct((t, KV_W), F32)] * 2
        aliases, scratch = {}, []
    else:
        kv_shapes = [jax.ShapeDtypeStruct((N_CTX, DEPTH, S_CTX, KV_W), F32)] * 2
        if layer == 0:
            kv_specs = [pl.BlockSpec((tm // S_CTX, DEPTH, S_CTX, KV_W), lambda i: (i, 0, 0, 0))] * 2
            aliases = {}
        else:
            in_specs += [pl.BlockSpec(memory_space=pl.ANY)] * 2
            args += list(kv_bufs)
            kv_specs = [pl.BlockSpec((tm // S_CTX, None, S_CTX, KV_W), lambda i: (i, layer, 0, 0))] * 2
            aliases = {len(args) - 2: 3, len(args) - 1: 4}
        scratch = [pltpu.VMEM((2 * KV_W // LANE, tm, LANE), F32)]
    out_specs = [row(D), row(D), row(D)] + kv_specs + [row(D), row(3 * D)]
    out_shape = [act(D), act(D), act(D)] + kv_shapes + [act(D), act(3 * D)]
    return pl.pallas_call(
        functools.partial(_inproj_kernel, rope=rope, layer=layer),
        grid=(t // tm,), in_specs=in_specs, out_specs=out_specs, out_shape=out_shape,
        input_output_aliases=aliases, scratch_shapes=scratch,
        compiler_params=_cparams(1), name=name,
    )(*args)


GB = 256
SCAN_CHUNK = 8


def _lru_kernel(xr_ref, gy_ref, h0_ref, cw_ref, cb_ref, wg_ref, bg_ref, lam_ref,
                out_ref, ends_ref, a_s, u_s, h_s, p_s, xs_ref, xc_s):
    r = xr_ref.shape[0]
    steps = r // SUB
    sub = lax.broadcasted_iota(jnp.int32, (SUB, GB), 0)
    for cb in range(LRU_W // GB):
        cols = slice(cb * GB, (cb + 1) * GB)
        x = xr_ref[:, cols].astype(F32)
        xs_ref[SUB:r + SUB, :] = x
        xs_ref[0:SUB, :] = jnp.where(sub == 0, 0.0, pltpu.roll(x[r - SUB:r], 1, 0))
        for t in range(2):
            xs_ref[r + (1 + t) * SUB:r + (2 + t) * SUB, :] = jnp.where(
                sub == SUB - 1, 0.0, pltpu.roll(x[t * SUB:(t + 1) * SUB], SUB - 1, 0))
        xc = cb_ref[:, cols] + xs_ref[0:r, :] * cw_ref[0:1, cols]
        for t in range(1, 4):
            xc = xc + xs_ref[t * SUB:r + t * SUB, :] * cw_ref[t:t + 1, cols]
        xc_s[...] = xc
        lhs = xc_s[...].astype(BF16)
        xh = 0.5 * xc_s[...]
        for d in range(2):
            g = _dot(lhs, wg_ref[d, cb])
            tr = jnp.tanh(g[:, :GB] + bg_ref[d, 0:1, cols])
            ti = jnp.tanh(g[:, GB:] + bg_ref[d, 1:2, cols])
            lam = lam_ref[d:d + 1, cols]
            log_sig = jnp.minimum(lam, 0.0) - jnp.log(1.0 + jnp.exp(-jnp.abs(lam)))
            ch = (0.5 * LRU_C) * log_sig
            w = tr * (ch * LOG2E) + ch * LOG2E
            a = jnp.exp2(w)
            m2 = jnp.tanh(w * (-1.0 / LOG2E)) * (1.0 + a * a)
            mult = jnp.where(m2 > 0.0, m2 * lax.rsqrt(m2), 0.0)
            u = mult * (ti * xh + xh)
            a_s[d] = a
            u_s[d] = u
            e0 = 0 if d == 0 else r - SUB
            edge = sub == (0 if d == 0 else SUB - 1)
            ae = a_s[d, e0:e0 + SUB, :]
            u_s[d, e0:e0 + SUB, :] = u_s[d, e0:e0 + SUB, :] + jnp.where(edge, ae * h0_ref[d:d + 1, cols], 0.0)
            a_s[d, e0:e0 + SUB, :] = jnp.where(edge, 0.0, ae)

        def chunk(i, carry):
            hf, pf, hb, pb = carry
            base_f = pl.multiple_of(i * (SCAN_CHUNK * SUB), SCAN_CHUNK * SUB)
            base_b = pl.multiple_of((steps - SCAN_CHUNK) * SUB - i * (SCAN_CHUNK * SUB), SCAN_CHUNK * SUB)
            for s in range(SCAN_CHUNK):
                rf = pl.ds(base_f + s * SUB, SUB)
                rb = pl.ds(base_b + (SCAN_CHUNK - 1 - s) * SUB, SUB)
                af = a_s[0, rf, :]
                hf = af * hf + u_s[0, rf, :]
                pf = af * pf
                h_s[0, rf, :] = hf
                p_s[0, rf, :] = pf
                ab = a_s[1, rb, :]
                hb = ab * hb + u_s[1, rb, :]
                pb = ab * pb
                h_s[1, rb, :] = hb
                p_s[1, rb, :] = pb
            return hf, pf, hb, pb

        zero = jnp.zeros((SUB, GB), F32)
        one = jnp.ones((SUB, GB), F32)
        hf, pf, hb, pb = lax.fori_loop(0, steps // SCAN_CHUNK, chunk, (zero, one, zero, one))

        ef, eb = hf, hb
        for sh in (1, 2, 4):
            keep_f = sub >= sh
            ef = ef + pf * jnp.where(keep_f, pltpu.roll(ef, sh, 0), 0.0)
            pf = pf * jnp.where(keep_f, pltpu.roll(pf, sh, 0), 1.0)
            keep_b = sub < SUB - sh
            eb = eb + pb * jnp.where(keep_b, pltpu.roll(eb, SUB - sh, 0), 0.0)
            pb = pb * jnp.where(keep_b, pltpu.roll(pb, SUB - sh, 0), 1.0)
        ends_ref[0:SUB, cols] = ef
        ends_ref[SUB:2 * SUB, cols] = eb
        cf = jnp.where(sub >= 1, pltpu.roll(ef, 1, 0), 0.0)
        cbk = jnp.where(sub < SUB - 1, pltpu.roll(eb, SUB - 1, 0), 0.0)
        cf2 = jnp.concatenate([cf, cf], axis=0)
        cb2 = jnp.concatenate([cbk, cbk], axis=0)

        def fix(m, _):
            rows = pl.ds(pl.multiple_of(m * 2 * SUB, 2 * SUB), 2 * SUB)
            hft = h_s[0, rows, :] + p_s[0, rows, :] * cf2
            hbt = h_s[1, rows, :] + p_s[1, rows, :] * cb2
            out_ref[rows, cols] = ((hft + hbt) * gy_ref[rows, cols].astype(F32)).astype(BF16)
            return 0

        lax.fori_loop(0, r // (2 * SUB), fix, 0, unroll=4)


def _lru(xr, gy, h0, conv_w, conv_b, wg, bg, lam, *, seq, name):
    t = xr.shape[0]
    nseq = t // seq
    full = lambda shape: pl.BlockSpec(shape, lambda i: (0,) * len(shape))
    return pl.pallas_call(
        _lru_kernel,
        grid=(nseq,),
        in_specs=[pl.BlockSpec((seq, D), lambda i: (i, 0)), pl.BlockSpec((seq, D), lambda i: (i, 0)),
                  pl.BlockSpec((None, 2, LRU_W), lambda i: (i, 0, 0)),
                  full((4, LRU_W)), full((1, LRU_W)), full((2, LRU_W // GB, GB, 2 * GB)),
                  full((2, 2, LRU_W)), full((2, LRU_W))],
        out_specs=[pl.BlockSpec((seq, D), lambda i: (i, 0)),
                   pl.BlockSpec((None, 2 * SUB, LRU_W), lambda i: (i, 0, 0))],
        out_shape=[jax.ShapeDtypeStruct((t, D), BF16), jax.ShapeDtypeStruct((nseq, 2 * SUB, LRU_W), F32)],
        scratch_shapes=[pltpu.VMEM((2, seq, GB), F32)] * 4 +
                       [pltpu.VMEM((seq + 3 * SUB, GB), F32), pltpu.VMEM((seq, GB), F32)],
        compiler_params=_cparams(1), name=name,
    )(xr, gy, h0, conv_w, conv_b, wg, bg, lam)


POOL_G = 256


def _pool_kernel(up_ref, wp_ref, scale_ref, out_ref):
    r = up_ref.shape[0]
    steps = r // SUB
    row = lax.broadcasted_iota(jnp.int32, (r, 1), 0)
    t = (row & (SUB - 1)) * steps + jnp.right_shift(row, 3)
    for g in range(4):
        cols = slice(g * POOL_G, (g + 1) * POOL_G)
        x = up_ref[:, cols].astype(F32)
        half = 1 << g
        back, fwd = x, x
        for lvl in range(g):
            n = 1 << lvl
            back = back + _shift_prev(back, r, n)
            fwd = fwd + _shift_next(fwd, r, n)
        win = _shift_prev(back, r) + fwd
        cnt = jnp.minimum(t + half, r) - jnp.maximum(t - half, 0)
        inv = 1.0 / cnt.astype(F32)
        d = (win * inv - x).astype(BF16)
        out_ref[:, cols] = (_dot(d, wp_ref[g]) * scale_ref[:, cols]).astype(BF16)


def _pool(up, w_pool, scale, *, layer, seq, name):
    t = up.shape[0]
    return pl.pallas_call(
        _pool_kernel,
        grid=(t // seq,),
        in_specs=[pl.BlockSpec((seq, D), lambda i: (i, 0)),
                  pl.BlockSpec((None, 4, POOL_G, POOL_G), lambda i: (layer, 0, 0, 0)),
                  pl.BlockSpec((1, D), lambda i: (0, 0))],
        out_specs=pl.BlockSpec((seq, D), lambda i: (i, 0)),
        out_shape=jax.ShapeDtypeStruct((t, D), BF16),
        compiler_params=_cparams(1), name=name,
    )(up, w_pool, scale)


ONES_ROWS = 16


def _attn_kernel(*refs, cached):
    transposed = cached
    if cached:
        q_ref, k_ref, v_ref, ck_ref, cv_ref, o_ref = refs
        kall = jnp.concatenate([ck_ref[...], k_ref[...]], axis=0)
        vall = jnp.concatenate([cv_ref[...], v_ref[...]], axis=0)
    else:
        q_ref, k_ref, v_ref, o_ref = refs
        kall = k_ref[...]
        vall = v_ref[...]
    half = HEAD_DIM
    lane = lax.broadcasted_iota(jnp.int32, (1, LANE), 1)

    def both_halves(x, own_low):
        own = jnp.where((lane < half) if own_low else (lane >= half), x, 0.0)
        oth = pltpu.roll(own, half, 1)
        lo, hi = (own, oth) if own_low else (oth, own)
        return lo.astype(BF16), hi.astype(BF16)

    if transposed:
        heads = []
        for kvh in range(N_KV):
            blk = slice((kvh // 2) * LANE, (kvh // 2 + 1) * LANE)
            ks = both_halves(kall[:, blk], kvh % 2 == 0)
            r0 = (kvh % 2) * HEAD_DIM
            v_t = vall[:, blk].T[r0:r0 + HEAD_DIM]
            v_t = jnp.concatenate([v_t, jnp.ones((ONES_ROWS, v_t.shape[1]), F32)], axis=0).astype(BF16)
            for pair in range(2):
                heads += [(kk, v_t, kvh * 4 * HEAD_DIM + pair * LANE) for kk in ks]

        def scores(h):
            kk, _, c0 = heads[h]
            return _dot_nt(kk, q_ref[:, c0:c0 + LANE])

        s_next = scores(0)
        outs = []
        for h, (_, v_t, c0) in enumerate(heads):
            s = s_next
            if h + 1 < len(heads):
                s_next = scores(h + 1)
            e = jnp.exp2(s - jnp.max(s, axis=0, keepdims=True)).astype(BF16)
            pv = _dot(v_t, e)
            outs.append(pv[:HEAD_DIM] / pv[HEAD_DIM:HEAD_DIM + 1])
            if h % 2 == 1:
                o_ref[:, c0:c0 + LANE] = jnp.concatenate(outs, axis=0).T.astype(BF16)
                outs = []
        return

    for kvh in range(N_KV):
        blk = slice((kvh // 2) * LANE, (kvh // 2 + 1) * LANE)
        ks = both_halves(kall[:, blk], kvh % 2 == 0)
        vs = both_halves(vall[:, blk], kvh % 2 == 0)
        c0 = kvh * 4 * HEAD_DIM
        sq = q_ref.shape[0]
        qq = jnp.concatenate([q_ref[:, c0:c0 + LANE], q_ref[:, c0 + LANE:c0 + 2 * LANE]], axis=0)
        acc = jnp.zeros((2 * sq, LANE), F32)
        for kk, vv in zip(ks, vs):
            s = _dot_nt(qq, kk)
            e = jnp.exp2(s - jnp.max(s, axis=-1, keepdims=True))
            l = jnp.sum(e, axis=-1, keepdims=True)
            acc = acc + _dot(e.astype(BF16), vv) / l
        o_ref[:, c0:c0 + LANE] = acc[:sq].astype(BF16)
        o_ref[:, c0 + LANE:c0 + 2 * LANE] = acc[sq:].astype(BF16)


def _attention(q, k, v, cache_k, cache_v, layer, *, seq, name):
    t = q.shape[0]
    cached = cache_k is not None
    if cached:
        kv_spec = pl.BlockSpec((seq, KV_W), lambda i: (i, 0))
    else:
        kv_spec = pl.BlockSpec((None, None, seq, KV_W), lambda i: (i, layer, 0, 0))
    in_specs = [pl.BlockSpec((seq, D), lambda i: (i, 0)), kv_spec, kv_spec]
    args = [q, k, v]
    if cached:
        in_specs += [pl.BlockSpec((None, None, PAST, KV_W), lambda i: (i, layer, 0, 0))] * 2
        args += [cache_k, cache_v]
    return pl.pallas_call(
        functools.partial(_attn_kernel, cached=cached),
        grid=(t // seq,), in_specs=in_specs,
        out_specs=pl.BlockSpec((seq, D), lambda i: (i, 0)),
        out_shape=jax.ShapeDtypeStruct((t, D), BF16),
        compiler_params=_cparams(1), name=name,
    )(*args)


def _merge_kernel(a_ref, o_ref, c_ref, gate_ref, x_ref, mod_ref, n2g_ref, wa_ref, wb_ref, wc_ref, wo_ref,
                  x1_ref, h2_ref):
    mix = gate_ref[:, 0:D].astype(F32) * _dot(a_ref[...], wa_ref[...])
    mix = mix + gate_ref[:, D:2 * D].astype(F32) * _dot(o_ref[...], wb_ref[...])
    mix = mix + gate_ref[:, 2 * D:3 * D].astype(F32) * _dot(c_ref[...], wc_ref[...])
    out = _dot(mix.astype(BF16), wo_ref[...])
    mod = mod_ref[...]
    g1, sh2, sc2 = mod[:, 2 * D:3 * D], mod[:, 3 * D:4 * D], mod[:, 4 * D:5 * D]
    x1 = x_ref[...] + g1 * out
    x1_ref[...] = x1
    h2_ref[...] = ((_rms(x1) * n2g_ref[...]) * (1.0 + sc2) + sh2).astype(BF16)


def _merge(a, o, c, gates, x, mod_l, n2g, wa, wb, wc, wo, *, layer, tm, row_of_tile, name):
    t = x.shape[0]
    row = lambda w: pl.BlockSpec((tm, w), lambda i: (i, 0))
    wspec = _resident((D, D), layer)
    return pl.pallas_call(
        _merge_kernel,
        grid=(t // tm,),
        in_specs=[row(D), row(D), row(D), row(3 * D), row(D),
                  pl.BlockSpec((None, 1, 6 * D), lambda i: (row_of_tile(i), 0, 0)),
                  pl.BlockSpec((1, D), lambda i: (0, 0)), wspec, wspec, wspec, wspec],
        out_specs=[row(D), row(D)],
        out_shape=[jax.ShapeDtypeStruct((t, D), F32), jax.ShapeDtypeStruct((t, D), BF16)],
        compiler_params=_cparams(1), name=name,
    )(a, o, c, gates, x, mod_l, n2g, wa, wb, wc, wo)


FF_TN = 1024
FF_ROWS = 1024


def _ffn_up_kernel(h_ref, wg_ref, wv_ref, cwg_ref, cwv_ref, cbg_ref, cbv_ref, out_ref, *, seq):
    h = h_ref[...]

    def conv(z, cw_ref, cb_ref):
        y = cb_ref[...] + _shift_prev(z, seq) * cw_ref[0:1, :]
        y = y + z * cw_ref[1:2, :]
        return y + _shift_next(z, seq) * cw_ref[2:3, :]

    g = conv(_dot(h, wg_ref[...]), cwg_ref, cbg_ref)
    v = conv(_dot(h, wv_ref[...]), cwv_ref, cbv_ref)
    hg = 0.5 * g
    out_ref[...] = ((hg * jnp.tanh(hg) + hg) * v).astype(BF16)


def _ffn_up(h2, w_up, cw, cb, *, layer, seq, name):
    t = h2.shape[0]
    nj = D_FF // FF_TN
    return pl.pallas_call(
        functools.partial(_ffn_up_kernel, seq=seq),
        grid=(t // FF_ROWS, nj),
        in_specs=[pl.BlockSpec((FF_ROWS, D), lambda i, j: (i, 0)),
                  pl.BlockSpec((None, D, FF_TN), lambda i, j: (layer, 0, j)),
                  pl.BlockSpec((None, D, FF_TN), lambda i, j: (layer, 0, nj + j)),
                  pl.BlockSpec((3, FF_TN), lambda i, j: (0, j)),
                  pl.BlockSpec((3, FF_TN), lambda i, j: (0, nj + j)),
                  pl.BlockSpec((1, FF_TN), lambda i, j: (0, j)),
                  pl.BlockSpec((1, FF_TN), lambda i, j: (0, nj + j))],
        out_specs=pl.BlockSpec((FF_ROWS, FF_TN), lambda i, j: (i, j)),
        out_shape=jax.ShapeDtypeStruct((t, D_FF), BF16),
        compiler_params=_cparams(2), name=name,
    )(h2, w_up, w_up, cw, cw, cb, cb)


def _ffn_down_kernel(act_ref, w_ref, x1_ref, mod_ref, fg_ref, out_ref, *, final):
    g2 = mod_ref[:, 5 * D:6 * D]
    x2 = x1_ref[...] + g2 * _dot(act_ref[...], w_ref[...])
    out_ref[...] = _rms(x2) * fg_ref[...] if final else x2


def _ffn_down(act, w_down, x1, mod_l, fg, *, layer, tm, row_of_tile, final, name):
    t = x1.shape[0]
    return pl.pallas_call(
        functools.partial(_ffn_down_kernel, final=final),
        grid=(t // tm,),
        in_specs=[pl.BlockSpec((tm, D_FF), lambda i: (i, 0)),
                  _resident((D_FF, D), layer),
                  pl.BlockSpec((tm, D), lambda i: (i, 0)),
                  pl.BlockSpec((None, 1, 6 * D), lambda i: (row_of_tile(i), 0, 0)),
                  pl.BlockSpec((1, D), lambda i: (0, 0))],
        out_specs=pl.BlockSpec((tm, D), lambda i: (i, 0)),
        out_shape=jax.ShapeDtypeStruct((t, D), F32),
        compiler_params=_cparams(1), name=name,
    )(act, w_down, x1, mod_l, fg)


def _interleave(x):
    b, s, w = x.shape
    return x.reshape(b, SUB, s // SUB, w).transpose(0, 2, 1, 3).reshape(b * s, w)


def _deinterleave(y, b, s):
    return y.reshape(b, s // SUB, SUB, -1).transpose(0, 2, 1, 3).reshape(b, s, -1)


def _rope_tables():
    p = np.arange(S_LAT)
    t = (p % SUB) * (S_LAT // SUB) + p // SUB
    pos = np.stack([t // GRID_W, t % GRID_W], axis=1).astype(np.float32)
    quarter = HEAD_DIM // 4
    inv = (ROPE_BASE ** (-np.arange(quarter, dtype=np.float32) / quarter)).astype(np.float32)
    d = np.arange(LANE) % HEAD_DIM
    which = d // (HEAD_DIM // 2)
    e = d % (HEAD_DIM // 2)
    ang = pos[:, which] * inv[e % quarter][None, :]
    sign = np.where(e < quarter, -1.0, 1.0).astype(np.float32)
    return jnp.asarray(np.cos(ang).astype(np.float32)), jnp.asarray((np.sin(ang) * sign).astype(np.float32))


def _block_diag_gates(w_r, w_i):
    eye = jnp.eye(4, dtype=F32)

    def bd(w):
        w = w.reshape(2, 4, 4, LRU_BW, LRU_BW)
        w = w[:, :, :, :, None, :] * eye[None, None, :, None, :, None]
        return w.reshape(2, 4, GB, GB)

    return jnp.concatenate([bd(w_r), bd(w_i)], axis=-1).astype(BF16)


def _trunk(x_prompt, x_sample, cache_k, cache_v, state_lru, c, c_ctx, norm1_g, norm2_g, w_mod, b_mod, w_in,
           b_gate, conv_w, conv_b, w_rg, b_rg, w_ig, b_ig, lru_lambda, q_norm_g, k_norm_g, w_pool, pool_scale,
           w_br_a, w_br_b, w_br_c, w_o, w_up, ffn_conv_w, ffn_conv_b, w_down, final_norm_g,
           paths=("ctx", "lat")):
    cond = jnp.zeros((16, D), F32).at[0:N_LAT].set(c).at[N_LAT].set(c_ctx)
    mod = _modulation(cond, w_mod, b_mod)
    rope_tabs = _rope_tables()
    gmat = jnp.asarray(np.kron(np.eye(2), np.full((HEAD_DIM, HEAD_DIM), 1.0 / HEAD_DIM)), BF16)
    ck = cache_k.reshape(N_LAT, DEPTH, PAST, KV_W)
    cv = cache_v.reshape(N_LAT, DEPTH, PAST, KV_W)
    fg = final_norm_g.reshape(1, D)

    xs = {"ctx": _interleave(x_prompt), "lat": _interleave(x_sample)}
    cfg = {"ctx": dict(seq=S_CTX, tm=512, row=lambda i: N_LAT),
           "lat": dict(seq=S_LAT, tm=512, row=lambda i: i // (S_LAT // 512))}
    new_s = []
    kv_bufs = None
    w_in_b, wa, wb, wc, wo, w_up_b, w_down_b, wp = (
        w.astype(BF16) for w in (w_in, w_br_a, w_br_b, w_br_c, w_o, w_up, w_down, w_pool))
    for l in range(DEPTH):
        mod_l = mod[l].reshape(16, 1, 6 * D)
        wg = _block_diag_gates(0.5 * w_rg[l], 0.5 * w_ig[l])
        bg = 0.5 * jnp.stack([b_rg[l], b_ig[l]], axis=1)
        qg = jnp.tile(q_norm_g[l], 2).reshape(1, LANE)
        kg = jnp.tile(k_norm_g[l], 2).reshape(1, LANE)
        for path in paths:
            seq, tm, row = cfg[path]["seq"], cfg[path]["tm"], cfg[path]["row"]
            lat = path == "lat"
            x = xs[path]
            xr, gy, q, k, v, up, gates = _inproj(
                x, mod_l, norm1_g[l].reshape(1, D), w_in_b, b_gate[l].reshape(1, 3 * D), qg, kg, gmat,
                rope_tabs if lat else None, None if lat else kv_bufs, layer=l, tm=tm, row_of_tile=row,
                name=f"inproj_{path}{l}")
            if not lat:
                kv_bufs = (k, v)
            h0 = state_lru[:, l] if lat else jnp.zeros((N_CTX, 2, LRU_W), F32)
            a_pre, ends = _lru(xr, gy, h0, conv_w[l], conv_b[l].reshape(1, LRU_W), wg, bg, lru_lambda[l],
                               seq=seq, name=f"lru_{path}{l}")
            c_pre = _pool(up, wp, pool_scale[l].reshape(1, D), layer=l, seq=seq, name=f"pool_{path}{l}")
            o = _attention(q, k, v, ck if lat else None, cv if lat else None, l, seq=seq,
                           name=f"attn_{path}{l}")
            x1, h2 = _merge(a_pre, o, c_pre, gates, x, mod_l, norm2_g[l].reshape(1, D), wa, wb, wc, wo,
                            layer=l, tm=tm, row_of_tile=row, name=f"merge_{path}{l}")
            act = _ffn_up(h2, w_up_b, ffn_conv_w[l], ffn_conv_b[l].reshape(1, 2 * D_FF), layer=l, seq=seq,
                          name=f"ffn_up_{path}{l}")
            xs[path] = _ffn_down(act, w_down_b, x1, mod_l, fg, layer=l, tm=tm, row_of_tile=row,
                                 final=(l == DEPTH - 1), name=f"ffn_down_{path}{l}")
            if not lat:
                new_s.append(jnp.stack([ends[:, SUB - 1], ends[:, SUB]], axis=1))
    new_k, new_v = (b.reshape(N_CTX, DEPTH, S_CTX, N_KV, HEAD_DIM) for b in kv_bufs)
    return xs, new_k, new_v, new_s


def kernel(x_prompt, x_sample, cache_k, cache_v, state_lru, c, c_ctx, norm1_g, norm2_g, w_mod, b_mod, w_in,
           b_gate, conv_w, conv_b, w_rg, b_rg, w_ig, b_ig, lru_lambda, q_norm_g, k_norm_g, w_pool, pool_scale,
           w_br_a, w_br_b, w_br_c, w_o, w_up, ffn_conv_w, ffn_conv_b, w_down, final_norm_g):
    xs, new_k, new_v, new_s = _trunk(
        x_prompt, x_sample, cache_k, cache_v, state_lru, c, c_ctx, norm1_g, norm2_g, w_mod, b_mod, w_in,
        b_gate, conv_w, conv_b, w_rg, b_rg, w_ig, b_ig, lru_lambda, q_norm_g, k_norm_g, w_pool, pool_scale,
        w_br_a, w_br_b, w_br_c, w_o, w_up, ffn_conv_w, ffn_conv_b, w_down, final_norm_g)
    y_prompt = _deinterleave(xs["ctx"], N_CTX, S_CTX)
    y_sample = _deinterleave(xs["lat"], N_LAT, S_LAT)
    return (y_prompt, y_sample, new_k, new_v, jnp.stack(new_s, axis=1))
```

```python
import functools

import numpy as np
import jax
import jax.numpy as jnp
from jax import lax
from jax.experimental import pallas as pl
from jax.experimental.pallas import tpu as pltpu

F32 = jnp.float32
BF16 = jnp.bfloat16

D = 1024
DEPTH = 2
N_CTX, S_CTX = 16, 256
N_LAT, S_LAT = 8, 1024
PAST = 256
GRID_W = 64
LRU_W = 1024
LRU_BW = 64
LRU_C = 8.0
N_HEADS, N_KV, HEAD_DIM = 16, 4, 64
KV_W = N_KV * HEAD_DIM
ROPE_BASE = 10000.0
D_FF = 3 * D
EPS = 1e-6
SUB = 8
LANE = 128
ROPE_SHIFT = HEAD_DIM // 4
C_XR, C_YR, C_Q, C_K, C_V, C_UP, C_GL, C_END = 0, 1024, 2048, 3072, 3328, 3584, 4608, 7680

LOG2E = float(np.log2(np.e))
Q_PRESCALE = LOG2E * HEAD_DIM ** -0.5

VMEM_LIMIT = 56 * 1024 * 1024


def _cparams(n_axes):
    return pltpu.CompilerParams(dimension_semantics=("arbitrary",) * n_axes,
                                vmem_limit_bytes=VMEM_LIMIT)


def _resident(shape, layer):
    return pl.BlockSpec((None,) + shape, lambda *_: (layer,) + (0,) * len(shape), pipeline_mode=pl.Buffered(1))


def _dot(a, b):
    return jnp.dot(a, b, preferred_element_type=F32)


def _dot_nt(a, b):
    return lax.dot_general(a, b, (((1,), (1,)), ((), ())), preferred_element_type=F32)


def _rms(x):
    return x * lax.rsqrt(jnp.mean(x * x, axis=-1, keepdims=True) + EPS)


def _shift_prev(x, rm, n=1):
    r, w = x.shape
    sub = lax.broadcasted_iota(jnp.int32, (SUB, w), 0)
    pieces = []
    for m in range(r // rm):
        base = m * rm
        for t in range(n):
            lo = base + rm - SUB * (n - t)
            pieces.append(jnp.where(sub == 0, 0.0, pltpu.roll(x[lo:lo + SUB], 1, 0)))
        pieces.append(x[base:base + rm - SUB * n])
    return jnp.concatenate(pieces, axis=0)


def _shift_next(x, rm, n=1):
    r, w = x.shape
    sub = lax.broadcasted_iota(jnp.int32, (SUB, w), 0)
    pieces = []
    for m in range(r // rm):
        base = m * rm
        pieces.append(x[base + SUB * n:base + rm])
        for t in range(n):
            lo = base + SUB * t
            pieces.append(jnp.where(sub == SUB - 1, 0.0, pltpu.roll(x[lo:lo + SUB], SUB - 1, 0)))
    return jnp.concatenate(pieces, axis=0)


def _mod_kernel(cond_ref, w_ref, b_ref, out_ref):
    c = cond_ref[...]
    s = (c * jax.nn.sigmoid(c)).astype(BF16)
    out_ref[...] = _dot(s, w_ref[...].astype(BF16)) + b_ref[...]


def _modulation(cond, w_mod, b_mod):
    tn = 1536
    return pl.pallas_call(
        _mod_kernel,
        grid=(DEPTH, 6 * D // tn),
        in_specs=[pl.BlockSpec((16, D), lambda l, j: (0, 0)),
                  pl.BlockSpec((None, D, tn), lambda l, j: (l, 0, j)),
                  pl.BlockSpec((None, 1, tn), lambda l, j: (l, 0, j))],
        out_specs=pl.BlockSpec((None, 16, tn), lambda l, j: (l, 0, j)),
        out_shape=jax.ShapeDtypeStruct((DEPTH, 16, 6 * D), F32),
        compiler_params=_cparams(2),
        name="modulation",
    )(cond, w_mod, b_mod.reshape(DEPTH, 1, 6 * D))


def _inproj_kernel(*refs, rope, layer):
    if rope:
        (x_ref, mod_ref, n1g_ref, w_ref, bgate_ref, qg_ref, kg_ref, gmat_ref, cos_ref, sin_ref,
         xr_ref, gy_ref, q_ref, k_ref, v_ref, up_ref, gate_ref) = refs
    else:
        if layer == 0:
            (x_ref, mod_ref, n1g_ref, w_ref, bgate_ref, qg_ref, kg_ref, gmat_ref,
             xr_ref, gy_ref, q_ref, kc_ref, vc_ref, up_ref, gate_ref, stage_ref) = refs
            for dst in (kc_ref, vc_ref):
                for other in range(1, DEPTH):
                    dst[:, other] = jnp.zeros((dst.shape[0],) + dst.shape[2:], F32)
            kc_ref, vc_ref = kc_ref.at[:, 0], vc_ref.at[:, 0]
        else:
            (x_ref, mod_ref, n1g_ref, w_ref, bgate_ref, qg_ref, kg_ref, gmat_ref, _, _,
             xr_ref, gy_ref, q_ref, kc_ref, vc_ref, up_ref, gate_ref, stage_ref) = refs
    mod = mod_ref[...]
    sh1, sc1 = mod[:, 0:D], mod[:, D:2 * D]
    h = ((_rms(x_ref[...]) * n1g_ref[...]) * (1.0 + sc1) + sh1).astype(BF16)

    def mm(c0, c1):
        return _dot(h, w_ref[:, c0:c1])

    for g in range(3):
        hz = 0.5 * (mm(C_GL + g * D, C_GL + (g + 1) * D) + bgate_ref[:, g * D:(g + 1) * D])
        gate_ref[:, g * D:(g + 1) * D] = (0.5 * jnp.tanh(hz) + 0.5).astype(BF16)
    gy_ref[...] = jax.nn.gelu(mm(C_YR, C_Q)).astype(BF16)

    lane = lax.broadcasted_iota(jnp.int32, (1, LANE), 1)
    first = (lane & ROPE_SHIFT) == 0

    def head_norm(xb, g):
        ms = _dot((xb * xb).astype(BF16), gmat_ref[...])
        y = (xb * lax.rsqrt(ms + EPS)) * g
        if rope:
            partner = jnp.where(first, pltpu.roll(y, LANE - ROPE_SHIFT, 1), pltpu.roll(y, ROPE_SHIFT, 1))
            y = y * cos_ref[...] + partner * sin_ref[...]
        return y

    qa = mm(C_Q, C_K)
    for c in range(N_HEADS * HEAD_DIM // LANE):
        qn = head_norm(qa[:, c * LANE:(c + 1) * LANE], qg_ref[...])
        q_ref[:, c * LANE:(c + 1) * LANE] = (qn * Q_PRESCALE).astype(BF16)
    kv_slabs = KV_W // LANE
    ka = mm(C_K, C_V)
    for c in range(kv_slabs):
        kn = head_norm(ka[:, c * LANE:(c + 1) * LANE], kg_ref[...])
        if rope:
            k_ref[:, c * LANE:(c + 1) * LANE] = kn
        else:
            stage_ref[c] = kn
    up_ref[...] = mm(C_UP, C_GL).astype(BF16)
    xr_ref[...] = mm(C_XR, C_YR).astype(BF16)
    va = mm(C_V, C_UP)
    if rope:
        v_ref[...] = va
    else:
        steps = S_CTX // SUB
        for c in range(kv_slabs):
            stage_ref[kv_slabs + c] = va[:, c * LANE:(c + 1) * LANE]
        for slab, dst in enumerate([kc_ref] * kv_slabs + [vc_ref] * kv_slabs):
            cols = slice((slab % kv_slabs) * LANE, (slab % kv_slabs + 1) * LANE)
            for s in range(x_ref.shape[0] // S_CTX):
                for j in range(SUB):
                    dst[s, j * steps:(j + 1) * steps, cols] = (
                        stage_ref[slab, pl.ds(s * S_CTX + j, steps, stride=SUB), :])


def _inproj(x, mod_l, n1g, w_in, b_gate, qg, kg, gmat, rope_tabs, kv_bufs, *, layer, tm, row_of_tile, name):
    t = x.shape[0]
    rope = rope_tabs is not None
    full = lambda shape: pl.BlockSpec(shape, lambda i: (0,) * len(shape))
    in_specs = [pl.BlockSpec((tm, D), lambda i: (i, 0)),
                pl.BlockSpec((None, 1, 6 * D), lambda i: (row_of_tile(i), 0, 0)),
                full((1, D)), _resident((D, C_END), layer), full((1, 3 * D)), full((1, LANE)), full((1, LANE)),
                full((LANE, LANE))]
    args = [x, mod_l, n1g, w_in, b_gate, qg, kg, gmat]
    if rope:
        per_seq = S_LAT // tm
        in_specs += [pl.BlockSpec((tm, LANE), lambda i: (i % per_seq, 0))] * 2
        args += list(rope_tabs)
    row = lambda w: pl.BlockSpec((tm, w), lambda i: (i, 0))
    act = lambda w: jax.ShapeDtypeStruct((t, w), BF16)
    if rope:
        kv_specs = [row(KV_W), row(KV_W)]
        kv_shapes = [jax.ShapeDtypeStruct((t, KV_W), F32)] * 2
        aliases, scratch = {}, []
    else:
        kv_shapes = [jax.ShapeDtypeStruct((N_CTX, DEPTH, S_CTX, KV_W), F32)] * 2
        if layer == 0:
            kv_specs = [pl.BlockSpec((tm // S_CTX, DEPTH, S_CTX, KV_W), lambda i: (i, 0, 0, 0))] * 2
            aliases = {}
        else:
            in_specs += [pl.BlockSpec(memory_space=pl.ANY)] * 2
            args += list(kv_bufs)
            kv_specs = [pl.BlockSpec((tm // S_CTX, None, S_CTX, KV_W), lambda i: (i, layer, 0, 0))] * 2
            aliases = {len(args) - 2: 3, len(args) - 1: 4}
        scratch = [pltpu.VMEM((2 * KV_W // LANE, tm, LANE), F32)]
    out_specs = [row(D), row(D), row(D)] + kv_specs + [row(D), row(3 * D)]
    out_shape = [act(D), act(D), act(D)] + kv_shapes + [act(D), act(3 * D)]
    return pl.pallas_call(
        functools.partial(_inproj_kernel, rope=rope, layer=layer),
        grid=(t // tm,), in_specs=in_specs, out_specs=out_specs, out_shape=out_shape,
        input_output_aliases=aliases, scratch_shapes=scratch,
        compiler_params=_cparams(1), name=name,
    )(*args)


GB = 256
SCAN_CHUNK = 8


def _lru_kernel(xr_ref, gy_ref, h0_ref, cw_ref, cb_ref, wg_ref, bg_ref, lam_ref,
                out_ref, ends_ref, a_s, u_s, h_s, p_s, xs_ref, xc_s):
    r = xr_ref.shape[0]
    steps = r // SUB
    sub = lax.broadcasted_iota(jnp.int32, (SUB, GB), 0)
    for cb in range(LRU_W // GB):
        cols = slice(cb * GB, (cb + 1) * GB)
        x = xr_ref[:, cols].astype(F32)
        xs_ref[SUB:r + SUB, :] = x
        xs_ref[0:SUB, :] = jnp.where(sub == 0, 0.0, pltpu.roll(x[r - SUB:r], 1, 0))
        for t in range(2):
            xs_ref[r + (1 + t) * SUB:r + (2 + t) * SUB, :] = jnp.where(
                sub == SUB - 1, 0.0, pltpu.roll(x[t * SUB:(t + 1) * SUB], SUB - 1, 0))
        xc = cb_ref[:, cols] + xs_ref[0:r, :] * cw_ref[0:1, cols]
        for t in range(1, 4):
            xc = xc + xs_ref[t * SUB:r + t * SUB, :] * cw_ref[t:t + 1, cols]
        xc_s[...] = xc
        lhs = xc_s[...].astype(BF16)
        xh = 0.5 * xc_s[...]
        for d in range(2):
            g = _dot(lhs, wg_ref[d, cb])
            tr = jnp.tanh(g[:, :GB] + bg_ref[d, 0:1, cols])
            ti = jnp.tanh(g[:, GB:] + bg_ref[d, 1:2, cols])
            lam = lam_ref[d:d + 1, cols]
            log_sig = jnp.minimum(lam, 0.0) - jnp.log(1.0 + jnp.exp(-jnp.abs(lam)))
            ch = (0.5 * LRU_C) * log_sig
            w = tr * (ch * LOG2E) + ch * LOG2E
            a = jnp.exp2(w)
            m2 = jnp.tanh(w * (-1.0 / LOG2E)) * (1.0 + a * a)
            mult = jnp.where(m2 > 0.0, m2 * lax.rsqrt(m2), 0.0)
            u = mult * (ti * xh + xh)
            a_s[d] = a
            u_s[d] = u
            e0 = 0 if d == 0 else r - SUB
            edge = sub == (0 if d == 0 else SUB - 1)
            ae = a_s[d, e0:e0 + SUB, :]
            u_s[d, e0:e0 + SUB, :] = u_s[d, e0:e0 + SUB, :] + jnp.where(edge, ae * h0_ref[d:d + 1, cols], 0.0)
            a_s[d, e0:e0 + SUB, :] = jnp.where(edge, 0.0, ae)

        def chunk(i, carry):
            hf, pf, hb, pb = carry
            base_f = pl.multiple_of(i * (SCAN_CHUNK * SUB), SCAN_CHUNK * SUB)
            base_b = pl.multiple_of((steps - SCAN_CHUNK) * SUB - i * (SCAN_CHUNK * SUB), SCAN_CHUNK * SUB)
            for s in range(SCAN_CHUNK):
                rf = pl.ds(base_f + s * SUB, SUB)
                rb = pl.ds(base_b + (SCAN_CHUNK - 1 - s) * SUB, SUB)
                af = a_s[0, rf, :]
                hf = af * hf + u_s[0, rf, :]
                pf = af * pf
                h_s[0, rf, :] = hf
                p_s[0, rf, :] = pf
                ab = a_s[1, rb, :]
                hb = ab * hb + u_s[1, rb, :]
                pb = ab * pb
                h_s[1, rb, :] = hb
                p_s[1, rb, :] = pb
            return hf, pf, hb, pb

        zero = jnp.zeros((SUB, GB), F32)
        one = jnp.ones((SUB, GB), F32)
        hf, pf, hb, pb = lax.fori_loop(0, steps // SCAN_CHUNK, chunk, (zero, one, zero, one))

        ef, eb = hf, hb
        for sh in (1, 2, 4):
            keep_f = sub >= sh
            ef = ef + pf * jnp.where(keep_f, pltpu.roll(ef, sh, 0), 0.0)
            pf = pf * jnp.where(keep_f, pltpu.roll(pf, sh, 0), 1.0)
            keep_b = sub < SUB - sh
            eb = eb + pb * jnp.where(keep_b, pltpu.roll(eb, SUB - sh, 0), 0.0)
            pb = pb * jnp.where(keep_b, pltpu.roll(pb, SUB - sh, 0), 1.0)
        ends_ref[0:SUB, cols] = ef
        ends_ref[SUB:2 * SUB, cols] = eb
        cf = jnp.where(sub >= 1, pltpu.roll(ef, 1, 0), 0.0)
        cbk = jnp.where(sub < SUB - 1, pltpu.roll(eb, SUB - 1, 0), 0.0)
        cf2 = jnp.concatenate([cf, cf], axis=0)
        cb2 = jnp.concatenate([cbk, cbk], axis=0)

        def fix(m, _):
            rows = pl.ds(pl.multiple_of(m * 2 * SUB, 2 * SUB), 2 * SUB)
            hft = h_s[0, rows, :] + p_s[0, rows, :] * cf2
            hbt = h_s[1, rows, :] + p_s[1, rows, :] * cb2
            out_ref[rows, cols] = ((hft + hbt) * gy_ref[rows, cols].astype(F32)).astype(BF16)
            return 0

        lax.fori_loop(0, r // (2 * SUB), fix, 0, unroll=4)


def _lru(xr, gy, h0, conv_w, conv_b, wg, bg, lam, *, seq, name):
    t = xr.shape[0]
    nseq = t // seq
    full = lambda shape: pl.BlockSpec(shape, lambda i: (0,) * len(shape))
    return pl.pallas_call(
        _lru_kernel,
        grid=(nseq,),
        in_specs=[pl.BlockSpec((seq, D), lambda i: (i, 0)), pl.BlockSpec((seq, D), lambda i: (i, 0)),
                  pl.BlockSpec((None, 2, LRU_W), lambda i: (i, 0, 0)),
                  full((4, LRU_W)), full((1, LRU_W)), full((2, LRU_W // GB, GB, 2 * GB)),
                  full((2, 2, LRU_W)), full((2, LRU_W))],
        out_specs=[pl.BlockSpec((seq, D), lambda i: (i, 0)),
                   pl.BlockSpec((None, 2 * SUB, LRU_W), lambda i: (i, 0, 0))],
        out_shape=[jax.ShapeDtypeStruct((t, D), BF16), jax.ShapeDtypeStruct((nseq, 2 * SUB, LRU_W), F32)],
        scratch_shapes=[pltpu.VMEM((2, seq, GB), F32)] * 4 +
                       [pltpu.VMEM((seq + 3 * SUB, GB), F32), pltpu.VMEM((seq, GB), F32)],
        compiler_params=_cparams(1), name=name,
    )(xr, gy, h0, conv_w, conv_b, wg, bg, lam)


POOL_G = 256


def _pool_inv_counts(seq):
    p = np.arange(seq)
    t = (p % SUB) * (seq // SUB) + p // SUB
    cols = []
    for g in range(4):
        half = 1 << g
        cnt = np.minimum(t + half, seq) - np.maximum(t - half, 0)
        cols.append(np.repeat((1.0 / cnt.astype(np.float32))[:, None], POOL_G, axis=1))
    return jnp.asarray(np.concatenate(cols, axis=1).astype(np.float32))


def _pool_kernel(up_ref, wp_ref, scale_ref, inv_ref, out_ref):
    r = up_ref.shape[0]
    for g in range(4):
        cols = slice(g * POOL_G, (g + 1) * POOL_G)
        x = up_ref[:, cols].astype(F32)
        back, fwd = x, x
        for lvl in range(g):
            n = 1 << lvl
            back = back + _shift_prev(back, r, n)
            fwd = fwd + _shift_next(fwd, r, n)
        win = _shift_prev(back, r) + fwd
        d = (win * inv_ref[:, cols] - x).astype(BF16)
        out_ref[:, cols] = (_dot(d, wp_ref[g]) * scale_ref[:, cols]).astype(BF16)


def _pool(up, w_pool, scale, *, layer, seq, name):
    t = up.shape[0]
    return pl.pallas_call(
        _pool_kernel,
        grid=(t // seq,),
        in_specs=[pl.BlockSpec((seq, D), lambda i: (i, 0)),
                  pl.BlockSpec((None, 4, POOL_G, POOL_G), lambda i: (layer, 0, 0, 0)),
                  pl.BlockSpec((1, D), lambda i: (0, 0)),
                  pl.BlockSpec((seq, D), lambda i: (0, 0))],
        out_specs=pl.BlockSpec((seq, D), lambda i: (i, 0)),
        out_shape=jax.ShapeDtypeStruct((t, D), BF16),
        compiler_params=_cparams(1), name=name,
    )(up, w_pool, scale, _pool_inv_counts(seq))


ONES_ROWS = 16


def _attn_kernel(*refs, cached):
    transposed = cached
    if cached:
        q_ref, k_ref, v_ref, ck_ref, cv_ref, o_ref = refs
        kall = jnp.concatenate([ck_ref[...], k_ref[...]], axis=0)
        vall = jnp.concatenate([cv_ref[...], v_ref[...]], axis=0)
    else:
        q_ref, k_ref, v_ref, o_ref = refs
        kall = k_ref[...]
        vall = v_ref[...]
    half = HEAD_DIM
    lane = lax.broadcasted_iota(jnp.int32, (1, LANE), 1)

    def both_halves(x, own_low):
        own = jnp.where((lane < half) if own_low else (lane >= half), x, 0.0)
        oth = pltpu.roll(own, half, 1)
        lo, hi = (own, oth) if own_low else (oth, own)
        return lo.astype(BF16), hi.astype(BF16)

    if transposed:
        heads = []
        for kvh in range(N_KV):
            blk = slice((kvh // 2) * LANE, (kvh // 2 + 1) * LANE)
            ks = both_halves(kall[:, blk], kvh % 2 == 0)
            r0 = (kvh % 2) * HEAD_DIM
            v_t = vall[:, blk].T[r0:r0 + HEAD_DIM]
            v_t = jnp.concatenate([v_t, jnp.ones((ONES_ROWS, v_t.shape[1]), F32)], axis=0).astype(BF16)
            for pair in range(2):
                heads += [(kk, v_t, kvh * 4 * HEAD_DIM + pair * LANE) for kk in ks]

        def scores(h):
            kk, _, c0 = heads[h]
            return _dot_nt(kk, q_ref[:, c0:c0 + LANE])

        s_next = scores(0)
        outs = []
        for h, (_, v_t, c0) in enumerate(heads):
            s = s_next
            if h + 1 < len(heads):
                s_next = scores(h + 1)
            e = jnp.exp2(s - jnp.max(s, axis=0, keepdims=True)).astype(BF16)
            pv = _dot(v_t, e)
            outs.append(pv[:HEAD_DIM] / pv[HEAD_DIM:HEAD_DIM + 1])
            if h % 2 == 1:
                o_ref[:, c0:c0 + LANE] = jnp.concatenate(outs, axis=0).T.astype(BF16)
                outs = []
        return

    for kvh in range(N_KV):
        blk = slice((kvh // 2) * LANE, (kvh // 2 + 1) * LANE)
        ks = both_halves(kall[:, blk], kvh % 2 == 0)
        vs = both_halves(vall[:, blk], kvh % 2 == 0)
        c0 = kvh * 4 * HEAD_DIM
        sq = q_ref.shape[0]
        qq = jnp.concatenate([q_ref[:, c0:c0 + LANE], q_ref[:, c0 + LANE:c0 + 2 * LANE]], axis=0)
        acc = jnp.zeros((2 * sq, LANE), F32)
        for kk, vv in zip(ks, vs):
            s = _dot_nt(qq, kk)
            e = jnp.exp2(s - jnp.max(s, axis=-1, keepdims=True))
            l = jnp.sum(e, axis=-1, keepdims=True)
            acc = acc + _dot(e.astype(BF16), vv) / l
        o_ref[:, c0:c0 + LANE] = acc[:sq].astype(BF16)
        o_ref[:, c0 + LANE:c0 + 2 * LANE] = acc[sq:].astype(BF16)


def _attention(q, k, v, cache_k, cache_v, layer, *, seq, name):
    t = q.shape[0]
    cached = cache_k is not None
    if cached:
        kv_spec = pl.BlockSpec((seq, KV_W), lambda i: (i, 0))
    else:
        kv_spec = pl.BlockSpec((None, None, seq, KV_W), lambda i: (i, layer, 0, 0))
    in_specs = [pl.BlockSpec((seq, D), lambda i: (i, 0)), kv_spec, kv_spec]
    args = [q, k, v]
    if cached:
        in_specs += [pl.BlockSpec((None, None, PAST, KV_W), lambda i: (i, layer, 0, 0))] * 2
        args += [cache_k, cache_v]
    return pl.pallas_call(
        functools.partial(_attn_kernel, cached=cached),
        grid=(t // seq,), in_specs=in_specs,
        out_specs=pl.BlockSpec((seq, D), lambda i: (i, 0)),
        out_shape=jax.ShapeDtypeStruct((t, D), BF16),
        compiler_params=_cparams(1), name=name,
    )(*args)


def _merge_kernel(a_ref, o_ref, c_ref, gate_ref, x_ref, mod_ref, n2g_ref, wa_ref, wb_ref, wc_ref, wo_ref,
                  x1_ref, h2_ref):
    mix = gate_ref[:, 0:D].astype(F32) * _dot(a_ref[...], wa_ref[...])
    mix = mix + gate_ref[:, D:2 * D].astype(F32) * _dot(o_ref[...], wb_ref[...])
    mix = mix + gate_ref[:, 2 * D:3 * D].astype(F32) * _dot(c_ref[...], wc_ref[...])
    out = _dot(mix.astype(BF16), wo_ref[...])
    mod = mod_ref[...]
    g1, sh2, sc2 = mod[:, 2 * D:3 * D], mod[:, 3 * D:4 * D], mod[:, 4 * D:5 * D]
    x1 = x_ref[...] + g1 * out
    x1_ref[...] = x1
    h2_ref[...] = ((_rms(x1) * n2g_ref[...]) * (1.0 + sc2) + sh2).astype(BF16)


def _merge(a, o, c, gates, x, mod_l, n2g, wa, wb, wc, wo, *, layer, tm, row_of_tile, name):
    t = x.shape[0]
    row = lambda w: pl.BlockSpec((tm, w), lambda i: (i, 0))
    wspec = _resident((D, D), layer)
    return pl.pallas_call(
        _merge_kernel,
        grid=(t // tm,),
        in_specs=[row(D), row(D), row(D), row(3 * D), row(D),
                  pl.BlockSpec((None, 1, 6 * D), lambda i: (row_of_tile(i), 0, 0)),
                  pl.BlockSpec((1, D), lambda i: (0, 0)), wspec, wspec, wspec, wspec],
        out_specs=[row(D), row(D)],
        out_shape=[jax.ShapeDtypeStruct((t, D), F32), jax.ShapeDtypeStruct((t, D), BF16)],
        compiler_params=_cparams(1), name=name,
    )(a, o, c, gates, x, mod_l, n2g, wa, wb, wc, wo)


FF_TN = 1024
FF_ROWS = 1024


def _ffn_up_kernel(h_ref, wg_ref, wv_ref, cwg_ref, cwv_ref, cbg_ref, cbv_ref, out_ref, *, seq):
    h = h_ref[...]

    def conv(z, cw_ref, cb_ref):
        y = cb_ref[...] + _shift_prev(z, seq) * cw_ref[0:1, :]
        y = y + z * cw_ref[1:2, :]
        return y + _shift_next(z, seq) * cw_ref[2:3, :]

    g = conv(_dot(h, wg_ref[...]), cwg_ref, cbg_ref)
    v = conv(_dot(h, wv_ref[...]), cwv_ref, cbv_ref)
    hg = 0.5 * g
    out_ref[...] = ((hg * jnp.tanh(hg) + hg) * v).astype(BF16)


def _ffn_up(h2, w_up, cw, cb, *, layer, seq, name):
    t = h2.shape[0]
    nj = D_FF // FF_TN
    return pl.pallas_call(
        functools.partial(_ffn_up_kernel, seq=seq),
        grid=(t // FF_ROWS, nj),
        in_specs=[pl.BlockSpec((FF_ROWS, D), lambda i, j: (i, 0)),
                  pl.BlockSpec((None, D, FF_TN), lambda i, j: (layer, 0, j)),
                  pl.BlockSpec((None, D, FF_TN), lambda i, j: (layer, 0, nj + j)),
                  pl.BlockSpec((3, FF_TN), lambda i, j: (0, j)),
                  pl.BlockSpec((3, FF_TN), lambda i, j: (0, nj + j)),
                  pl.BlockSpec((1, FF_TN), lambda i, j: (0, j)),
                  pl.BlockSpec((1, FF_TN), lambda i, j: (0, nj + j))],
        out_specs=pl.BlockSpec((FF_ROWS, FF_TN), lambda i, j: (i, j)),
        out_shape=jax.ShapeDtypeStruct((t, D_FF), BF16),
        compiler_params=_cparams(2), name=name,
    )(h2, w_up, w_up, cw, cw, cb, cb)


def _ffn_down_kernel(act_ref, w_ref, x1_ref, mod_ref, fg_ref, out_ref, *, final):
    g2 = mod_ref[:, 5 * D:6 * D]
    x2 = x1_ref[...] + g2 * _dot(act_ref[...], w_ref[...])
    out_ref[...] = _rms(x2) * fg_ref[...] if final else x2


def _ffn_down(act, w_down, x1, mod_l, fg, *, layer, tm, row_of_tile, final, name):
    t = x1.shape[0]
    return pl.pallas_call(
        functools.partial(_ffn_down_kernel, final=final),
        grid=(t // tm,),
        in_specs=[pl.BlockSpec((tm, D_FF), lambda i: (i, 0)),
                  _resident((D_FF, D), layer),
                  pl.BlockSpec((tm, D), lambda i: (i, 0)),
                  pl.BlockSpec((None, 1, 6 * D), lambda i: (row_of_tile(i), 0, 0)),
                  pl.BlockSpec((1, D), lambda i: (0, 0))],
        out_specs=pl.BlockSpec((tm, D), lambda i: (i, 0)),
        out_shape=jax.ShapeDtypeStruct((t, D), F32),
        compiler_params=_cparams(1), name=name,
    )(act, w_down, x1, mod_l, fg)


def _interleave(x):
    b, s, w = x.shape
    return x.reshape(b, SUB, s // SUB, w).transpose(0, 2, 1, 3).reshape(b * s, w)


def _deinterleave(y, b, s):
    return y.reshape(b, s // SUB, SUB, -1).transpose(0, 2, 1, 3).reshape(b, s, -1)


def _rope_tables():
    p = np.arange(S_LAT)
    t = (p % SUB) * (S_LAT // SUB) + p // SUB
    pos = np.stack([t // GRID_W, t % GRID_W], axis=1).astype(np.float32)
    quarter = HEAD_DIM // 4
    inv = (ROPE_BASE ** (-np.arange(quarter, dtype=np.float32) / quarter)).astype(np.float32)
    d = np.arange(LANE) % HEAD_DIM
    which = d // (HEAD_DIM // 2)
    e = d % (HEAD_DIM // 2)
    ang = pos[:, which] * inv[e % quarter][None, :]
    sign = np.where(e < quarter, -1.0, 1.0).astype(np.float32)
    return jnp.asarray(np.cos(ang).astype(np.float32)), jnp.asarray((np.sin(ang) * sign).astype(np.float32))


def _block_diag_gates(w_r, w_i):
    eye = jnp.eye(4, dtype=F32)

    def bd(w):
        w = w.reshape(2, 4, 4, LRU_BW, LRU_BW)
        w = w[:, :, :, :, None, :] * eye[None, None, :, None, :, None]
        return w.reshape(2, 4, GB, GB)

    return jnp.concatenate([bd(w_r), bd(w_i)], axis=-1).astype(BF16)


def _trunk(x_prompt, x_sample, cache_k, cache_v, state_lru, c, c_ctx, norm1_g, norm2_g, w_mod, b_mod, w_in,
           b_gate, conv_w, conv_b, w_rg, b_rg, w_ig, b_ig, lru_lambda, q_norm_g, k_norm_g, w_pool, pool_scale,
           w_br_a, w_br_b, w_br_c, w_o, w_up, ffn_conv_w, ffn_conv_b, w_down, final_norm_g,
           paths=("ctx", "lat")):
    cond = jnp.zeros((16, D), F32).at[0:N_LAT].set(c).at[N_LAT].set(c_ctx)
    mod = _modulation(cond, w_mod, b_mod)
    rope_tabs = _rope_tables()
    gmat = jnp.asarray(np.kron(np.eye(2), np.full((HEAD_DIM, HEAD_DIM), 1.0 / HEAD_DIM)), BF16)
    ck = cache_k.reshape(N_LAT, DEPTH, PAST, KV_W)
    cv = cache_v.reshape(N_LAT, DEPTH, PAST, KV_W)
    fg = final_norm_g.reshape(1, D)

    xs = {"ctx": _interleave(x_prompt), "lat": _interleave(x_sample)}
    cfg = {"ctx": dict(seq=S_CTX, tm=512, row=lambda i: N_LAT),
           "lat": dict(seq=S_LAT, tm=512, row=lambda i: i // (S_LAT // 512))}
    new_s = []
    kv_bufs = None
    w_in_b, wa, wb, wc, wo, w_up_b, w_down_b, wp = (
        w.astype(BF16) for w in (w_in, w_br_a, w_br_b, w_br_c, w_o, w_up, w_down, w_pool))
    for l in range(DEPTH):
        mod_l = mod[l].reshape(16, 1, 6 * D)
        wg = _block_diag_gates(0.5 * w_rg[l], 0.5 * w_ig[l])
        bg = 0.5 * jnp.stack([b_rg[l], b_ig[l]], axis=1)
        qg = jnp.tile(q_norm_g[l], 2).reshape(1, LANE)
        kg = jnp.tile(k_norm_g[l], 2).reshape(1, LANE)
        for path in paths:
            seq, tm, row = cfg[path]["seq"], cfg[path]["tm"], cfg[path]["row"]
            lat = path == "lat"
            x = xs[path]
            xr, gy, q, k, v, up, gates = _inproj(
                x, mod_l, norm1_g[l].reshape(1, D), w_in_b, b_gate[l].reshape(1, 3 * D), qg, kg, gmat,
                rope_tabs if lat else None, None if lat else kv_bufs, layer=l, tm=tm, row_of_tile=row,
                name=f"inproj_{path}{l}")
            if not lat:
                kv_bufs = (k, v)
            h0 = state_lru[:, l] if lat else jnp.zeros((N_CTX, 2, LRU_W), F32)
            a_pre, ends = _lru(xr, gy, h0, conv_w[l], conv_b[l].reshape(1, LRU_W), wg, bg, lru_lambda[l],
                               seq=seq, name=f"lru_{path}{l}")
            c_pre = _pool(up, wp, pool_scale[l].reshape(1, D), layer=l, seq=seq, name=f"pool_{path}{l}")
            o = _attention(q, k, v, ck if lat else None, cv if lat else None, l, seq=seq,
                           name=f"attn_{path}{l}")
            x1, h2 = _merge(a_pre, o, c_pre, gates, x, mod_l, norm2_g[l].reshape(1, D), wa, wb, wc, wo,
                            layer=l, tm=tm, row_of_tile=row, name=f"merge_{path}{l}")
            act = _ffn_up(h2, w_up_b, ffn_conv_w[l], ffn_conv_b[l].reshape(1, 2 * D_FF), layer=l, seq=seq,
                          name=f"ffn_up_{path}{l}")
            xs[path] = _ffn_down(act, w_down_b, x1, mod_l, fg, layer=l, tm=tm, row_of_tile=row,
                                 final=(l == DEPTH - 1), name=f"ffn_down_{path}{l}")
            if not lat:
                new_s.append(jnp.stack([ends[:, SUB - 1], ends[:, SUB]], axis=1))
    new_k, new_v = (b.reshape(N_CTX, DEPTH, S_CTX, N_KV, HEAD_DIM) for b in kv_bufs)
    return xs, new_k, new_v, new_s


def kernel(x_prompt, x_sample, cache_k, cache_v, state_lru, c, c_ctx, norm1_g, norm2_g, w_mod, b_mod, w_in,
           b_gate, conv_w, conv_b, w_rg, b_rg, w_ig, b_ig, lru_lambda, q_norm_g, k_norm_g, w_pool, pool_scale,
           w_br_a, w_br_b, w_br_c, w_o, w_up, ffn_conv_w, ffn_conv_b, w_down, final_norm_g):
    xs, new_k, new_v, new_s = _trunk(
        x_prompt, x_sample, cache_k, cache_v, state_lru, c, c_ctx, norm1_g, norm2_g, w_mod, b_mod, w_in,
        b_gate, conv_w, conv_b, w_rg, b_rg, w_ig, b_ig, lru_lambda, q_norm_g, k_norm_g, w_pool, pool_scale,
        w_br_a, w_br_b, w_br_c, w_o, w_up, ffn_conv_w, ffn_conv_b, w_down, final_norm_g)
    y_prompt = _deinterleave(xs["ctx"], N_CTX, S_CTX)
    y_sample = _deinterleave(xs["lat"], N_LAT, S_LAT)
    return (y_prompt, y_sample, new_k, new_v, jnp.stack(new_s, axis=1))
```

```python
import functools

import numpy as np
import jax
import jax.numpy as jnp
from jax import lax
from jax.experimental import pallas as pl
from jax.experimental.pallas import tpu as pltpu

F32 = jnp.float32
BF16 = jnp.bfloat16

D = 1024
DEPTH = 2
N_CTX, S_CTX = 16, 256
N_LAT, S_LAT = 8, 1024
PAST = 256
GRID_W = 64
LRU_W = 1024
LRU_BW = 64
LRU_C = 8.0
N_HEADS, N_KV, HEAD_DIM = 16, 4, 64
KV_W = N_KV * HEAD_DIM
ROPE_BASE = 10000.0
D_FF = 3 * D
EPS = 1e-6
SUB = 8
LANE = 128
ROPE_SHIFT = HEAD_DIM // 4
C_XR, C_YR, C_Q, C_K, C_V, C_UP, C_GL, C_END = 0, 1024, 2048, 3072, 3328, 3584, 4608, 7680

LOG2E = float(np.log2(np.e))
Q_PRESCALE = LOG2E * HEAD_DIM ** -0.5

VMEM_LIMIT = 56 * 1024 * 1024


def _cparams(n_axes):
    return pltpu.CompilerParams(dimension_semantics=("arbitrary",) * n_axes,
                                vmem_limit_bytes=VMEM_LIMIT)


def _resident(shape, layer):
    return pl.BlockSpec((None,) + shape, lambda *_: (layer,) + (0,) * len(shape), pipeline_mode=pl.Buffered(1))


def _dot(a, b):
    return jnp.dot(a, b, preferred_element_type=F32)


def _dot_nt(a, b):
    return lax.dot_general(a, b, (((1,), (1,)), ((), ())), preferred_element_type=F32)


def _rms(x):
    return x * lax.rsqrt(jnp.mean(x * x, axis=-1, keepdims=True) + EPS)


def _shift_prev(x, rm, n=1):
    r, w = x.shape
    sub = lax.broadcasted_iota(jnp.int32, (SUB, w), 0)
    pieces = []
    for m in range(r // rm):
        base = m * rm
        for t in range(n):
            lo = base + rm - SUB * (n - t)
            pieces.append(jnp.where(sub == 0, 0.0, pltpu.roll(x[lo:lo + SUB], 1, 0)))
        pieces.append(x[base:base + rm - SUB * n])
    return jnp.concatenate(pieces, axis=0)


def _shift_next(x, rm, n=1):
    r, w = x.shape
    sub = lax.broadcasted_iota(jnp.int32, (SUB, w), 0)
    pieces = []
    for m in range(r // rm):
        base = m * rm
        pieces.append(x[base + SUB * n:base + rm])
        for t in range(n):
            lo = base + SUB * t
            pieces.append(jnp.where(sub == SUB - 1, 0.0, pltpu.roll(x[lo:lo + SUB], SUB - 1, 0)))
    return jnp.concatenate(pieces, axis=0)


def _mod_kernel(cond_ref, w_ref, b_ref, out_ref):
    c = cond_ref[...]
    s = (c * jax.nn.sigmoid(c)).astype(BF16)
    out_ref[...] = _dot(s, w_ref[...].astype(BF16)) + b_ref[...]


def _modulation(cond, w_mod, b_mod):
    tn = 1536
    return pl.pallas_call(
        _mod_kernel,
        grid=(DEPTH, 6 * D // tn),
        in_specs=[pl.BlockSpec((16, D), lambda l, j: (0, 0)),
                  pl.BlockSpec((None, D, tn), lambda l, j: (l, 0, j)),
                  pl.BlockSpec((None, 1, tn), lambda l, j: (l, 0, j))],
        out_specs=pl.BlockSpec((None, 16, tn), lambda l, j: (l, 0, j)),
        out_shape=jax.ShapeDtypeStruct((DEPTH, 16, 6 * D), F32),
        compiler_params=_cparams(2),
        name="modulation",
    )(cond, w_mod, b_mod.reshape(DEPTH, 1, 6 * D))


def _inproj_kernel(*refs, rope, layer):
    if rope:
        (x_ref, mod_ref, n1g_ref, w_ref, bgate_ref, qg_ref, kg_ref, gmat_ref, cos_ref, sin_ref,
         xr_ref, gy_ref, q_ref, k_ref, v_ref, up_ref, gate_ref) = refs
    else:
        if layer == 0:
            (x_ref, mod_ref, n1g_ref, w_ref, bgate_ref, qg_ref, kg_ref, gmat_ref,
             xr_ref, gy_ref, q_ref, kc_ref, vc_ref, up_ref, gate_ref, stage_ref) = refs
            for dst in (kc_ref, vc_ref):
                for other in range(1, DEPTH):
                    dst[:, other] = jnp.zeros((dst.shape[0],) + dst.shape[2:], F32)
            kc_ref, vc_ref = kc_ref.at[:, 0], vc_ref.at[:, 0]
        else:
            (x_ref, mod_ref, n1g_ref, w_ref, bgate_ref, qg_ref, kg_ref, gmat_ref, _, _,
             xr_ref, gy_ref, q_ref, kc_ref, vc_ref, up_ref, gate_ref, stage_ref) = refs
    mod = mod_ref[...]
    sh1, sc1 = mod[:, 0:D], mod[:, D:2 * D]
    h = ((_rms(x_ref[...]) * n1g_ref[...]) * (1.0 + sc1) + sh1).astype(BF16)

    def mm(c0, c1):
        return _dot(h, w_ref[:, c0:c1])

    for g in range(3):
        hz = 0.5 * (mm(C_GL + g * D, C_GL + (g + 1) * D) + bgate_ref[:, g * D:(g + 1) * D])
        gate_ref[:, g * D:(g + 1) * D] = (0.5 * jnp.tanh(hz) + 0.5).astype(BF16)
    gy_ref[...] = jax.nn.gelu(mm(C_YR, C_Q)).astype(BF16)

    lane = lax.broadcasted_iota(jnp.int32, (1, LANE), 1)
    first = (lane & ROPE_SHIFT) == 0

    def head_norm(xb, g):
        ms = _dot((xb * xb).astype(BF16), gmat_ref[...])
        y = (xb * lax.rsqrt(ms + EPS)) * g
        if rope:
            partner = jnp.where(first, pltpu.roll(y, LANE - ROPE_SHIFT, 1), pltpu.roll(y, ROPE_SHIFT, 1))
            y = y * cos_ref[...] + partner * sin_ref[...]
        return y

    qa = mm(C_Q, C_K)
    for c in range(N_HEADS * HEAD_DIM // LANE):
        qn = head_norm(qa[:, c * LANE:(c + 1) * LANE], qg_ref[...])
        q_ref[:, c * LANE:(c + 1) * LANE] = (qn * Q_PRESCALE).astype(BF16)
    kv_slabs = KV_W // LANE
    ka = mm(C_K, C_V)
    for c in range(kv_slabs):
        kn = head_norm(ka[:, c * LANE:(c + 1) * LANE], kg_ref[...])
        if rope:
            k_ref[:, c * LANE:(c + 1) * LANE] = kn
        else:
            stage_ref[c] = kn
    up_ref[...] = mm(C_UP, C_GL).astype(BF16)
    xr_ref[...] = mm(C_XR, C_YR).astype(BF16)
    va = mm(C_V, C_UP)
    if rope:
        v_ref[...] = va
    else:
        steps = S_CTX // SUB
        for c in range(kv_slabs):
            stage_ref[kv_slabs + c] = va[:, c * LANE:(c + 1) * LANE]
        for slab, dst in enumerate([kc_ref] * kv_slabs + [vc_ref] * kv_slabs):
            cols = slice((slab % kv_slabs) * LANE, (slab % kv_slabs + 1) * LANE)
            for s in range(x_ref.shape[0] // S_CTX):
                for j in range(SUB):
                    dst[s, j * steps:(j + 1) * steps, cols] = (
                        stage_ref[slab, pl.ds(s * S_CTX + j, steps, stride=SUB), :])


def _inproj(x, mod_l, n1g, w_in, b_gate, qg, kg, gmat, rope_tabs, kv_bufs, *, layer, tm, row_of_tile, name):
    t = x.shape[0]
    rope = rope_tabs is not None
    full = lambda shape: pl.BlockSpec(shape, lambda i: (0,) * len(shape))
    in_specs = [pl.BlockSpec((tm, D), lambda i: (i, 0)),
                pl.BlockSpec((None, 1, 6 * D), lambda i: (row_of_tile(i), 0, 0)),
                full((1, D)), _resident((D, C_END), layer), full((1, 3 * D)), full((1, LANE)), full((1, LANE)),
                full((LANE, LANE))]
    args = [x, mod_l, n1g, w_in, b_gate, qg, kg, gmat]
    if rope:
        per_seq = S_LAT // tm
        in_specs += [pl.BlockSpec((tm, LANE), lambda i: (i % per_seq, 0))] * 2
        args += list(rope_tabs)
    row = lambda w: pl.BlockSpec((tm, w), lambda i: (i, 0))
    act = lambda w: jax.ShapeDtypeStruct((t, w), BF16)
    if rope:
        kv_specs = [row(KV_W), row(KV_W)]
        kv_shapes = [jax.ShapeDtypeStruct((t, KV_W), F32)] * 2
        aliases, scratch = {}, []
    else:
        kv_shapes = [jax.ShapeDtypeStruct((N_CTX, DEPTH, S_CTX, KV_W), F32)] * 2
        if layer == 0:
            kv_specs = [pl.BlockSpec((tm // S_CTX, DEPTH, S_CTX, KV_W), lambda i: (i, 0, 0, 0))] * 2
            aliases = {}
        else:
            in_specs += [pl.BlockSpec(memory_space=pl.ANY)] * 2
            args += list(kv_bufs)
            kv_specs = [pl.BlockSpec((tm // S_CTX, None, S_CTX, KV_W), lambda i: (i, layer, 0, 0))] * 2
            aliases = {len(args) - 2: 3, len(args) - 1: 4}
        scratch = [pltpu.VMEM((2 * KV_W // LANE, tm, LANE), F32)]
    out_specs = [row(D), row(D), row(D)] + kv_specs + [row(D), row(3 * D)]
    out_shape = [act(D), act(D), act(D)] + kv_shapes + [act(D), act(3 * D)]
    return pl.pallas_call(
        functools.partial(_inproj_kernel, rope=rope, layer=layer),
        grid=(t // tm,), in_specs=in_specs, out_specs=out_specs, out_shape=out_shape,
        input_output_aliases=aliases, scratch_shapes=scratch,
        compiler_params=_cparams(1), name=name,
    )(*args)


GB = 256
SCAN_CHUNK = 8


def _lru_kernel(xr_ref, gy_ref, h0_ref, cw_ref, cb_ref, wg_ref, bg_ref, lam_ref,
                out_ref, ends_ref, a_s, u_s, h_s, p_s, xs_ref, xc_s):
    r = xr_ref.shape[0]
    steps = r // SUB
    sub = lax.broadcasted_iota(jnp.int32, (SUB, GB), 0)
    for cb in range(LRU_W // GB):
        cols = slice(cb * GB, (cb + 1) * GB)
        x = xr_ref[:, cols].astype(F32)
        xs_ref[SUB:r + SUB, :] = x
        xs_ref[0:SUB, :] = jnp.where(sub == 0, 0.0, pltpu.roll(x[r - SUB:r], 1, 0))
        for t in range(2):
            xs_ref[r + (1 + t) * SUB:r + (2 + t) * SUB, :] = jnp.where(
                sub == SUB - 1, 0.0, pltpu.roll(x[t * SUB:(t + 1) * SUB], SUB - 1, 0))
        xc = cb_ref[:, cols] + xs_ref[0:r, :] * cw_ref[0:1, cols]
        for t in range(1, 4):
            xc = xc + xs_ref[t * SUB:r + t * SUB, :] * cw_ref[t:t + 1, cols]
        xc_s[...] = xc
        lhs = xc_s[...].astype(BF16)
        xh = 0.5 * xc_s[...]
        for d in range(2):
            g = _dot(lhs, wg_ref[d, cb])
            tr = jnp.tanh(g[:, :GB] + bg_ref[d, 0:1, cols])
            ti = jnp.tanh(g[:, GB:] + bg_ref[d, 1:2, cols])
            lam = lam_ref[d:d + 1, cols]
            log_sig = jnp.minimum(lam, 0.0) - jnp.log(1.0 + jnp.exp(-jnp.abs(lam)))
            ch = (0.5 * LRU_C) * log_sig
            w = tr * (ch * LOG2E) + ch * LOG2E
            a = jnp.exp2(w)
            m2 = jnp.tanh(w * (-1.0 / LOG2E)) * (1.0 + a * a)
            mult = jnp.where(m2 > 0.0, m2 * lax.rsqrt(m2), 0.0)
            u = mult * (ti * xh + xh)
            a_s[d] = a
            u_s[d] = u
            e0 = 0 if d == 0 else r - SUB
            edge = sub == (0 if d == 0 else SUB - 1)
            ae = a_s[d, e0:e0 + SUB, :]
            u_s[d, e0:e0 + SUB, :] = u_s[d, e0:e0 + SUB, :] + jnp.where(edge, ae * h0_ref[d:d + 1, cols], 0.0)
            a_s[d, e0:e0 + SUB, :] = jnp.where(edge, 0.0, ae)

        def chunk(i, carry):
            hf, pf, hb, pb = carry
            base_f = pl.multiple_of(i * (SCAN_CHUNK * SUB), SCAN_CHUNK * SUB)
            base_b = pl.multiple_of((steps - SCAN_CHUNK) * SUB - i * (SCAN_CHUNK * SUB), SCAN_CHUNK * SUB)
            for s in range(SCAN_CHUNK):
                rf = pl.ds(base_f + s * SUB, SUB)
                rb = pl.ds(base_b + (SCAN_CHUNK - 1 - s) * SUB, SUB)
                af = a_s[0, rf, :]
                hf = af * hf + u_s[0, rf, :]
                pf = af * pf
                h_s[0, rf, :] = hf
                p_s[0, rf, :] = pf
                ab = a_s[1, rb, :]
                hb = ab * hb + u_s[1, rb, :]
                pb = ab * pb
                h_s[1, rb, :] = hb
                p_s[1, rb, :] = pb
            return hf, pf, hb, pb

        zero = jnp.zeros((SUB, GB), F32)
        one = jnp.ones((SUB, GB), F32)
        hf, pf, hb, pb = lax.fori_loop(0, steps // SCAN_CHUNK, chunk, (zero, one, zero, one))

        ef, eb = hf, hb
        for sh in (1, 2, 4):
            keep_f = sub >= sh
            ef = ef + pf * jnp.where(keep_f, pltpu.roll(ef, sh, 0), 0.0)
            pf = pf * jnp.where(keep_f, pltpu.roll(pf, sh, 0), 1.0)
            keep_b = sub < SUB - sh
            eb = eb + pb * jnp.where(keep_b, pltpu.roll(eb, SUB - sh, 0), 0.0)
            pb = pb * jnp.where(keep_b, pltpu.roll(pb, SUB - sh, 0), 1.0)
        ends_ref[0:SUB, cols] = ef
        ends_ref[SUB:2 * SUB, cols] = eb
        cf = jnp.where(sub >= 1, pltpu.roll(ef, 1, 0), 0.0)
        cbk = jnp.where(sub < SUB - 1, pltpu.roll(eb, SUB - 1, 0), 0.0)
        cf2 = jnp.concatenate([cf, cf], axis=0)
        cb2 = jnp.concatenate([cbk, cbk], axis=0)

        def fix(m, _):
            rows = pl.ds(pl.multiple_of(m * 2 * SUB, 2 * SUB), 2 * SUB)
            hft = h_s[0, rows, :] + p_s[0, rows, :] * cf2
            hbt = h_s[1, rows, :] + p_s[1, rows, :] * cb2
            out_ref[rows, cols] = ((hft + hbt) * gy_ref[rows, cols].astype(F32)).astype(BF16)
            return 0

        lax.fori_loop(0, r // (2 * SUB), fix, 0, unroll=4)


def _lru(xr, gy, h0, conv_w, conv_b, wg, bg, lam, *, seq, name):
    t = xr.shape[0]
    nseq = t // seq
    full = lambda shape: pl.BlockSpec(shape, lambda i: (0,) * len(shape))
    return pl.pallas_call(
        _lru_kernel,
        grid=(nseq,),
        in_specs=[pl.BlockSpec((seq, D), lambda i: (i, 0)), pl.BlockSpec((seq, D), lambda i: (i, 0)),
                  pl.BlockSpec((None, 2, LRU_W), lambda i: (i, 0, 0)),
                  full((4, LRU_W)), full((1, LRU_W)), full((2, LRU_W // GB, GB, 2 * GB)),
                  full((2, 2, LRU_W)), full((2, LRU_W))],
        out_specs=[pl.BlockSpec((seq, D), lambda i: (i, 0)),
                   pl.BlockSpec((None, 2 * SUB, LRU_W), lambda i: (i, 0, 0))],
        out_shape=[jax.ShapeDtypeStruct((t, D), BF16), jax.ShapeDtypeStruct((nseq, 2 * SUB, LRU_W), F32)],
        scratch_shapes=[pltpu.VMEM((2, seq, GB), F32)] * 4 +
                       [pltpu.VMEM((seq + 3 * SUB, GB), F32), pltpu.VMEM((seq, GB), F32)],
        compiler_params=_cparams(1), name=name,
    )(xr, gy, h0, conv_w, conv_b, wg, bg, lam)


POOL_G = 256


def _pool_inv_counts(seq):
    p = np.arange(seq)
    t = (p % SUB) * (seq // SUB) + p // SUB
    cols = []
    for g in range(4):
        half = 1 << g
        cnt = np.minimum(t + half, seq) - np.maximum(t - half, 0)
        cols.append(np.repeat((1.0 / cnt.astype(np.float32))[:, None], POOL_G, axis=1))
    return jnp.asarray(np.concatenate(cols, axis=1).astype(np.float32))


def _pool_kernel(up_ref, wp_ref, scale_ref, inv_ref, out_ref):
    r = up_ref.shape[0]
    for g in range(4):
        cols = slice(g * POOL_G, (g + 1) * POOL_G)
        x = up_ref[:, cols].astype(F32)
        back, fwd = x, x
        for lvl in range(g):
            n = 1 << lvl
            back = back + _shift_prev(back, r, n)
            fwd = fwd + _shift_next(fwd, r, n)
        win = _shift_prev(back, r) + fwd
        d = (win * inv_ref[:, cols] - x).astype(BF16)
        out_ref[:, cols] = (_dot(d, wp_ref[g]) * scale_ref[:, cols]).astype(BF16)


def _pool(up, w_pool, scale, *, layer, seq, name):
    t = up.shape[0]
    return pl.pallas_call(
        _pool_kernel,
        grid=(t // seq,),
        in_specs=[pl.BlockSpec((seq, D), lambda i: (i, 0)),
                  pl.BlockSpec((None, 4, POOL_G, POOL_G), lambda i: (layer, 0, 0, 0)),
                  pl.BlockSpec((1, D), lambda i: (0, 0)),
                  pl.BlockSpec((seq, D), lambda i: (0, 0))],
        out_specs=pl.BlockSpec((seq, D), lambda i: (i, 0)),
        out_shape=jax.ShapeDtypeStruct((t, D), BF16),
        compiler_params=_cparams(1), name=name,
    )(up, w_pool, scale, _pool_inv_counts(seq))


ONES_ROWS = 16
ATTN_TQ = 512


def _attn_kernel(*refs, cached):
    transposed = cached
    if cached:
        q_ref, k_ref, v_ref, ck_ref, cv_ref, o_ref = refs
        kall = jnp.concatenate([ck_ref[...], k_ref[...]], axis=0)
        vall = jnp.concatenate([cv_ref[...], v_ref[...]], axis=0)
    else:
        q_ref, k_ref, v_ref, o_ref = refs
        kall = k_ref[...]
        vall = v_ref[...]
    half = HEAD_DIM
    lane = lax.broadcasted_iota(jnp.int32, (1, LANE), 1)

    def both_halves(x, own_low):
        own = jnp.where((lane < half) if own_low else (lane >= half), x, 0.0)
        oth = pltpu.roll(own, half, 1)
        lo, hi = (own, oth) if own_low else (oth, own)
        return lo.astype(BF16), hi.astype(BF16)

    if transposed:
        heads = []
        for kvh in range(N_KV):
            blk = slice((kvh // 2) * LANE, (kvh // 2 + 1) * LANE)
            ks = both_halves(kall[:, blk], kvh % 2 == 0)
            r0 = (kvh % 2) * HEAD_DIM
            v_t = vall[:, blk].T[r0:r0 + HEAD_DIM]
            v_t = jnp.concatenate([v_t, jnp.ones((ONES_ROWS, v_t.shape[1]), F32)], axis=0).astype(BF16)
            for pair in range(2):
                c0 = kvh * 4 * HEAD_DIM + pair * LANE
                for qb in range(q_ref.shape[0] // ATTN_TQ):
                    rows = slice(qb * ATTN_TQ, (qb + 1) * ATTN_TQ)
                    heads += [(kk, v_t, c0, rows) for kk in ks]

        def scores(h):
            kk, _, c0, rows = heads[h]
            return _dot_nt(kk, q_ref[rows, c0:c0 + LANE])

        s_next = scores(0)
        outs = []
        for h, (_, v_t, c0, rows) in enumerate(heads):
            s = s_next
            if h + 1 < len(heads):
                s_next = scores(h + 1)
            e = jnp.exp2(s - jnp.max(s, axis=0, keepdims=True)).astype(BF16)
            pv = _dot(v_t, e)
            outs.append(pv[:HEAD_DIM] / pv[HEAD_DIM:HEAD_DIM + 1])
            if h % 2 == 1:
                o_ref[rows, c0:c0 + LANE] = jnp.concatenate(outs, axis=0).T.astype(BF16)
                outs = []
        return

    for kvh in range(N_KV):
        blk = slice((kvh // 2) * LANE, (kvh // 2 + 1) * LANE)
        ks = both_halves(kall[:, blk], kvh % 2 == 0)
        vs = both_halves(vall[:, blk], kvh % 2 == 0)
        c0 = kvh * 4 * HEAD_DIM
        sq = q_ref.shape[0]
        qq = jnp.concatenate([q_ref[:, c0:c0 + LANE], q_ref[:, c0 + LANE:c0 + 2 * LANE]], axis=0)
        acc = jnp.zeros((2 * sq, LANE), F32)
        for kk, vv in zip(ks, vs):
            s = _dot_nt(qq, kk)
            e = jnp.exp2(s - jnp.max(s, axis=-1, keepdims=True))
            l = jnp.sum(e, axis=-1, keepdims=True)
            acc = acc + _dot(e.astype(BF16), vv) / l
        o_ref[:, c0:c0 + LANE] = acc[:sq].astype(BF16)
        o_ref[:, c0 + LANE:c0 + 2 * LANE] = acc[sq:].astype(BF16)


def _attention(q, k, v, cache_k, cache_v, layer, *, seq, name):
    t = q.shape[0]
    cached = cache_k is not None
    if cached:
        kv_spec = pl.BlockSpec((seq, KV_W), lambda i: (i, 0))
    else:
        kv_spec = pl.BlockSpec((None, None, seq, KV_W), lambda i: (i, layer, 0, 0))
    in_specs = [pl.BlockSpec((seq, D), lambda i: (i, 0)), kv_spec, kv_spec]
    args = [q, k, v]
    if cached:
        in_specs += [pl.BlockSpec((None, None, PAST, KV_W), lambda i: (i, layer, 0, 0))] * 2
        args += [cache_k, cache_v]
    return pl.pallas_call(
        functools.partial(_attn_kernel, cached=cached),
        grid=(t // seq,), in_specs=in_specs,
        out_specs=pl.BlockSpec((seq, D), lambda i: (i, 0)),
        out_shape=jax.ShapeDtypeStruct((t, D), BF16),
        compiler_params=_cparams(1), name=name,
    )(*args)


def _merge_kernel(a_ref, o_ref, c_ref, gate_ref, x_ref, mod_ref, n2g_ref, wa_ref, wb_ref, wc_ref, wo_ref,
                  x1_ref, h2_ref):
    mix = gate_ref[:, 0:D].astype(F32) * _dot(a_ref[...], wa_ref[...])
    mix = mix + gate_ref[:, D:2 * D].astype(F32) * _dot(o_ref[...], wb_ref[...])
    mix = mix + gate_ref[:, 2 * D:3 * D].astype(F32) * _dot(c_ref[...], wc_ref[...])
    out = _dot(mix.astype(BF16), wo_ref[...])
    mod = mod_ref[...]
    g1, sh2, sc2 = mod[:, 2 * D:3 * D], mod[:, 3 * D:4 * D], mod[:, 4 * D:5 * D]
    x1 = x_ref[...] + g1 * out
    x1_ref[...] = x1
    h2_ref[...] = ((_rms(x1) * n2g_ref[...]) * (1.0 + sc2) + sh2).astype(BF16)


def _merge(a, o, c, gates, x, mod_l, n2g, wa, wb, wc, wo, *, layer, tm, row_of_tile, name):
    t = x.shape[0]
    row = lambda w: pl.BlockSpec((tm, w), lambda i: (i, 0))
    wspec = _resident((D, D), layer)
    return pl.pallas_call(
        _merge_kernel,
        grid=(t // tm,),
        in_specs=[row(D), row(D), row(D), row(3 * D), row(D),
                  pl.BlockSpec((None, 1, 6 * D), lambda i: (row_of_tile(i), 0, 0)),
                  pl.BlockSpec((1, D), lambda i: (0, 0)), wspec, wspec, wspec, wspec],
        out_specs=[row(D), row(D)],
        out_shape=[jax.ShapeDtypeStruct((t, D), F32), jax.ShapeDtypeStruct((t, D), BF16)],
        compiler_params=_cparams(1), name=name,
    )(a, o, c, gates, x, mod_l, n2g, wa, wb, wc, wo)


FF_TN = 1024
FF_ROWS = 1024


def _ffn_up_kernel(h_ref, wg_ref, wv_ref, cwg_ref, cwv_ref, cbg_ref, cbv_ref, out_ref, *, seq):
    h = h_ref[...]

    def conv(z, cw_ref, cb_ref):
        y = cb_ref[...] + _shift_prev(z, seq) * cw_ref[0:1, :]
        y = y + z * cw_ref[1:2, :]
        return y + _shift_next(z, seq) * cw_ref[2:3, :]

    g = conv(_dot(h, wg_ref[...]), cwg_ref, cbg_ref)
    v = conv(_dot(h, wv_ref[...]), cwv_ref, cbv_ref)
    hg = 0.5 * g
    out_ref[...] = ((hg * jnp.tanh(hg) + hg) * v).astype(BF16)


def _ffn_up(h2, w_up, cw, cb, *, layer, seq, name):
    t = h2.shape[0]
    nj = D_FF // FF_TN
    return pl.pallas_call(
        functools.partial(_ffn_up_kernel, seq=seq),
        grid=(t // FF_ROWS, nj),
        in_specs=[pl.BlockSpec((FF_ROWS, D), lambda i, j: (i, 0)),
                  pl.BlockSpec((None, D, FF_TN), lambda i, j: (layer, 0, j)),
                  pl.BlockSpec((None, D, FF_TN), lambda i, j: (layer, 0, nj + j)),
                  pl.BlockSpec((3, FF_TN), lambda i, j: (0, j)),
                  pl.BlockSpec((3, FF_TN), lambda i, j: (0, nj + j)),
                  pl.BlockSpec((1, FF_TN), lambda i, j: (0, j)),
                  pl.BlockSpec((1, FF_TN), lambda i, j: (0, nj + j))],
        out_specs=pl.BlockSpec((FF_ROWS, FF_TN), lambda i, j: (i, j)),
        out_shape=jax.ShapeDtypeStruct((t, D_FF), BF16),
        compiler_params=_cparams(2), name=name,
    )(h2, w_up, w_up, cw, cw, cb, cb)


def _ffn_down_kernel(act_ref, w_ref, x1_ref, mod_ref, fg_ref, out_ref, *, final):
    g2 = mod_ref[:, 5 * D:6 * D]
    x2 = x1_ref[...] + g2 * _dot(act_ref[...], w_ref[...])
    out_ref[...] = _rms(x2) * fg_ref[...] if final else x2


def _ffn_down(act, w_down, x1, mod_l, fg, *, layer, tm, row_of_tile, final, name):
    t = x1.shape[0]
    return pl.pallas_call(
        functools.partial(_ffn_down_kernel, final=final),
        grid=(t // tm,),
        in_specs=[pl.BlockSpec((tm, D_FF), lambda i: (i, 0)),
                  _resident((D_FF, D), layer),
                  pl.BlockSpec((tm, D), lambda i: (i, 0)),
                  pl.BlockSpec((None, 1, 6 * D), lambda i: (row_of_tile(i), 0, 0)),
                  pl.BlockSpec((1, D), lambda i: (0, 0))],
        out_specs=pl.BlockSpec((tm, D), lambda i: (i, 0)),
        out_shape=jax.ShapeDtypeStruct((t, D), F32),
        compiler_params=_cparams(1), name=name,
    )(act, w_down, x1, mod_l, fg)


def _interleave(x):
    b, s, w = x.shape
    return x.reshape(b, SUB, s // SUB, w).transpose(0, 2, 1, 3).reshape(b * s, w)


def _deinterleave(y, b, s):
    return y.reshape(b, s // SUB, SUB, -1).transpose(0, 2, 1, 3).reshape(b, s, -1)


def _rope_tables():
    p = np.arange(S_LAT)
    t = (p % SUB) * (S_LAT // SUB) + p // SUB
    pos = np.stack([t // GRID_W, t % GRID_W], axis=1).astype(np.float32)
    quarter = HEAD_DIM // 4
    inv = (ROPE_BASE ** (-np.arange(quarter, dtype=np.float32) / quarter)).astype(np.float32)
    d = np.arange(LANE) % HEAD_DIM
    which = d // (HEAD_DIM // 2)
    e = d % (HEAD_DIM // 2)
    ang = pos[:, which] * inv[e % quarter][None, :]
    sign = np.where(e < quarter, -1.0, 1.0).astype(np.float32)
    return jnp.asarray(np.cos(ang).astype(np.float32)), jnp.asarray((np.sin(ang) * sign).astype(np.float32))


def _block_diag_gates(w_r, w_i):
    eye = jnp.eye(4, dtype=F32)

    def bd(w):
        w = w.reshape(2, 4, 4, LRU_BW, LRU_BW)
        w = w[:, :, :, :, None, :] * eye[None, None, :, None, :, None]
        return w.reshape(2, 4, GB, GB)

    return jnp.concatenate([bd(w_r), bd(w_i)], axis=-1).astype(BF16)


def _trunk(x_prompt, x_sample, cache_k, cache_v, state_lru, c, c_ctx, norm1_g, norm2_g, w_mod, b_mod, w_in,
           b_gate, conv_w, conv_b, w_rg, b_rg, w_ig, b_ig, lru_lambda, q_norm_g, k_norm_g, w_pool, pool_scale,
           w_br_a, w_br_b, w_br_c, w_o, w_up, ffn_conv_w, ffn_conv_b, w_down, final_norm_g,
           paths=("ctx", "lat")):
    cond = jnp.zeros((16, D), F32).at[0:N_LAT].set(c).at[N_LAT].set(c_ctx)
    mod = _modulation(cond, w_mod, b_mod)
    rope_tabs = _rope_tables()
    gmat = jnp.asarray(np.kron(np.eye(2), np.full((HEAD_DIM, HEAD_DIM), 1.0 / HEAD_DIM)), BF16)
    ck = cache_k.reshape(N_LAT, DEPTH, PAST, KV_W)
    cv = cache_v.reshape(N_LAT, DEPTH, PAST, KV_W)
    fg = final_norm_g.reshape(1, D)

    xs = {"ctx": _interleave(x_prompt), "lat": _interleave(x_sample)}
    cfg = {"ctx": dict(seq=S_CTX, tm=512, row=lambda i: N_LAT),
           "lat": dict(seq=S_LAT, tm=512, row=lambda i: i // (S_LAT // 512))}
    new_s = []
    kv_bufs = None
    w_in_b, wa, wb, wc, wo, w_up_b, w_down_b, wp = (
        w.astype(BF16) for w in (w_in, w_br_a, w_br_b, w_br_c, w_o, w_up, w_down, w_pool))
    for l in range(DEPTH):
        mod_l = mod[l].reshape(16, 1, 6 * D)
        wg = _block_diag_gates(0.5 * w_rg[l], 0.5 * w_ig[l])
        bg = 0.5 * jnp.stack([b_rg[l], b_ig[l]], axis=1)
        qg = jnp.tile(q_norm_g[l], 2).reshape(1, LANE)
        kg = jnp.tile(k_norm_g[l], 2).reshape(1, LANE)
        for path in paths:
            seq, tm, row = cfg[path]["seq"], cfg[path]["tm"], cfg[path]["row"]
            lat = path == "lat"
            x = xs[path]
            xr, gy, q, k, v, up, gates = _inproj(
                x, mod_l, norm1_g[l].reshape(1, D), w_in_b, b_gate[l].reshape(1, 3 * D), qg, kg, gmat,
                rope_tabs if lat else None, None if lat else kv_bufs, layer=l, tm=tm, row_of_tile=row,
                name=f"inproj_{path}{l}")
            if not lat:
                kv_bufs = (k, v)
            h0 = state_lru[:, l] if lat else jnp.zeros((N_CTX, 2, LRU_W), F32)
            a_pre, ends = _lru(xr, gy, h0, conv_w[l], conv_b[l].reshape(1, LRU_W), wg, bg, lru_lambda[l],
                               seq=seq, name=f"lru_{path}{l}")
            c_pre = _pool(up, wp, pool_scale[l].reshape(1, D), layer=l, seq=seq, name=f"pool_{path}{l}")
            o = _attention(q, k, v, ck if lat else None, cv if lat else None, l, seq=seq,
                           name=f"attn_{path}{l}")
            x1, h2 = _merge(a_pre, o, c_pre, gates, x, mod_l, norm2_g[l].reshape(1, D), wa, wb, wc, wo,
                            layer=l, tm=tm, row_of_tile=row, name=f"merge_{path}{l}")
            act = _ffn_up(h2, w_up_b, ffn_conv_w[l], ffn_conv_b[l].reshape(1, 2 * D_FF), layer=l, seq=seq,
                          name=f"ffn_up_{path}{l}")
            xs[path] = _ffn_down(act, w_down_b, x1, mod_l, fg, layer=l, tm=tm, row_of_tile=row,
                                 final=(l == DEPTH - 1), name=f"ffn_down_{path}{l}")
            if not lat:
                new_s.append(jnp.stack([ends[:, SUB - 1], ends[:, SUB]], axis=1))
    new_k, new_v = (b.reshape(N_CTX, DEPTH, S_CTX, N_KV, HEAD_DIM) for b in kv_bufs)
    return xs, new_k, new_v, new_s


def kernel(x_prompt, x_sample, cache_k, cache_v, state_lru, c, c_ctx, norm1_g, norm2_g, w_mod, b_mod, w_in,
           b_gate, conv_w, conv_b, w_rg, b_rg, w_ig, b_ig, lru_lambda, q_norm_g, k_norm_g, w_pool, pool_scale,
           w_br_a, w_br_b, w_br_c, w_o, w_up, ffn_conv_w, ffn_conv_b, w_down, final_norm_g):
    xs, new_k, new_v, new_s = _trunk(
        x_prompt, x_sample, cache_k, cache_v, state_lru, c, c_ctx, norm1_g, norm2_g, w_mod, b_mod, w_in,
        b_gate, conv_w, conv_b, w_rg, b_rg, w_ig, b_ig, lru_lambda, q_norm_g, k_norm_g, w_pool, pool_scale,
        w_br_a, w_br_b, w_br_c, w_o, w_up, ffn_conv_w, ffn_conv_b, w_down, final_norm_g)
    y_prompt = _deinterleave(xs["ctx"], N_CTX, S_CTX)
    y_sample = _deinterleave(xs["lat"], N_LAT, S_LAT)
    return (y_prompt, y_sample, new_k, new_v, jnp.stack(new_s, axis=1))
```
